```python
import math
import jax, jax.numpy as jnp
from jax import lax
import numpy as np

D_MODEL = 1024
BATCH = 4
SEQ = 4096
DEPTH = 2

GRID_W = 64
CTX_LEN = 256
HEAD_DIM = 64
QBLK = 128
ROPE_THETA = 10000.0
EPS = 1e-6
NEG_INF = -1e30

A_HEADS = 8
A_KV_HEADS = 2
B_HEADS = 8
B_KV_HEADS = 2
B_WINDOW = 128
AB_SIZES = (A_HEADS * HEAD_DIM, A_KV_HEADS * HEAD_DIM, A_KV_HEADS * HEAD_DIM,
            B_HEADS * HEAD_DIM, B_KV_HEADS * HEAD_DIM, B_KV_HEADS * HEAD_DIM)
AB_HEADS = (A_HEADS, A_KV_HEADS, A_KV_HEADS, B_HEADS, B_KV_HEADS, B_KV_HEADS)
AB_SPLITS = (512, 640, 768, 1280, 1408)
AB_IN = 1536
AB_OUT = (A_HEADS + B_HEADS) * HEAD_DIM

C_HEADS = 8
C_V_DIM = 2 * HEAD_DIM
C_QK_WIDTH = C_HEADS * 2 * HEAD_DIM
C_IN = 2 * C_QK_WIDTH + C_HEADS * C_V_DIM
C_OUT = C_HEADS * C_V_DIM

N_EXPERTS = 64
TOP_K = 8
N_GROUPS = 8
TOPK_GROUPS = 4
EXPERT_HIDDEN = 256
SHARED_HIDDEN = 256
ROUTED_SCALE = 2.5
EBLK = 128

N_EVEN = (DEPTH + 1) // 2
N_ODD = DEPTH // 2

kernel_name = "hybrid_dit_gqa_window_diffattn_moe"


def rms_norm(x, g):
    xf = x.astype(jnp.float32)
    y = xf * lax.rsqrt(jnp.mean(xf * xf, axis=-1, keepdims=True) + EPS)
    return y.astype(x.dtype) * g


def modulate(h, shift, scale):
    return h * (1 + scale) + shift


def axial_rope(rows, dtype):
    r = jnp.repeat(jnp.arange(rows, dtype=jnp.float32), GRID_W)
    col = jnp.tile(jnp.arange(GRID_W, dtype=jnp.float32), rows)
    n_pair = HEAD_DIM // 4
    inv = ROPE_THETA ** (-jnp.arange(n_pair, dtype=jnp.float32) / n_pair)
    ang = jnp.concatenate([r[:, None] * inv, col[:, None] * inv], axis=-1)
    return jnp.cos(ang).astype(dtype), jnp.sin(ang).astype(dtype)


def apply_rope(x, cos, sin):
    x1, x2 = x[..., 0::2], x[..., 1::2]
    c = cos[None, :, None, :]
    s = sin[None, :, None, :]
    return jnp.stack([x1 * c - x2 * s, x1 * s + x2 * c], axis=-1).reshape(x.shape)


def sweep_query_blocks(fn, q):
    b, n = q.shape[:2]
    nb = n // QBLK
    qb = q.reshape((b, nb, QBLK) + q.shape[2:]).swapaxes(0, 1)
    o = lax.map(fn, qb)
    return o.swapaxes(0, 1).reshape((b, n) + o.shape[3:])


def gqa_attend(q, k, v, sink=None, mask=None):
    b, nq, hq, d = q.shape
    hkv = k.shape[2]
    qg = q.reshape(b, nq, hkv, hq // hkv, d)
    s = jnp.einsum('bqhgd,bkhd->bhgqk', qg, k, preferred_element_type=jnp.float32) * (d ** -0.5)
    if mask is not None:
        s = jnp.where(mask, s, NEG_INF)
    if sink is not None:
        sk = jnp.broadcast_to(sink.astype(jnp.float32).reshape(hkv, hq // hkv)[None, :, :, None, None],
                              s.shape[:-1] + (1,))
        p = jax.nn.softmax(jnp.concatenate([s, sk], axis=-1), axis=-1)[..., :-1]
    else:
        p = jax.nn.softmax(s, axis=-1)
    o = jnp.einsum('bhgqk,bkhd->bqhgd', p.astype(v.dtype), v)
    return o.reshape(b, nq, hq * d)


def gqa_window_blocks(q, k, v, k_ctx, v_ctx, sink):
    b, n, hq, d = q.shape
    nb = n // QBLK
    lc = k_ctx.shape[1]
    pad = ((0, 0), (QBLK, QBLK), (0, 0), (0, 0))
    kp = jnp.pad(k, pad)
    vp = jnp.pad(v, pad)

    def one(i):
        q0 = i * QBLK
        qblk = lax.dynamic_slice_in_dim(q, q0, QBLK, axis=1)
        kb = lax.dynamic_slice_in_dim(kp, q0, 3 * QBLK, axis=1)
        vb = lax.dynamic_slice_in_dim(vp, q0, 3 * QBLK, axis=1)
        qpos = q0 + jnp.arange(QBLK)
        kpos = q0 - QBLK + jnp.arange(3 * QBLK)
        ok = (jnp.abs(qpos[:, None] - kpos[None, :]) <= B_WINDOW) & (kpos[None, :] >= 0) & (kpos[None, :] < n)
        mask = jnp.concatenate([jnp.ones((QBLK, lc), bool), ok], axis=1)
        return gqa_attend(qblk, jnp.concatenate([k_ctx, kb], axis=1),
                          jnp.concatenate([v_ctx, vb], axis=1), sink=sink, mask=mask)

    o = lax.map(one, jnp.arange(nb))
    return o.swapaxes(0, 1).reshape(b, n, hq * d)


def diff_attend(q, k, v, lam):
    d = q.shape[-1]
    s = jnp.einsum('bqhmd,bkhmd->bhmqk', q, k, preferred_element_type=jnp.float32) * (d ** -0.5)
    p = jax.nn.softmax(s, axis=-1)
    a = p[:, :, 0] - lam * p[:, :, 1]
    return jnp.einsum('bhqk,bkhe->bqhe', a.astype(v.dtype), v)


def mixer_ab(h_lat, h_ctx, w_in, w_out, q_norm, k_norm, sink, cos, sin, want_ctx):
    def project(h):
        b, n, _ = h.shape
        parts = jnp.split(h @ w_in, list(AB_SPLITS), axis=-1)
        qa, ka, va, qb, kb, vb = [p.reshape(b, n, hh, HEAD_DIM) for p, hh in zip(parts, AB_HEADS)]
        return rms_norm(qa, q_norm), rms_norm(ka, k_norm), va, qb, kb, vb

    qa, ka, va, qb, kb, vb = project(h_lat)
    qa, ka = apply_rope(qa, cos, sin), apply_rope(ka, cos, sin)
    qb, kb = apply_rope(qb, cos, sin), apply_rope(kb, cos, sin)
    qa_c, ka_c, va_c, qb_c, kb_c, vb_c = project(h_ctx)

    ka_all = jnp.concatenate([ka_c, ka], axis=1)
    va_all = jnp.concatenate([va_c, va], axis=1)
    oa = sweep_query_blocks(lambda qq: gqa_attend(qq, ka_all, va_all), qa)
    ob = gqa_window_blocks(qb, kb, vb, kb_c, vb_c, sink)
    o_lat = jnp.concatenate([oa, ob], axis=-1) @ w_out
    if want_ctx:
        oa_c = gqa_attend(qa_c, ka_c, va_c)
        ob_c = gqa_attend(qb_c, kb_c, vb_c, sink=sink)
        o_ctx = jnp.concatenate([oa_c, ob_c], axis=-1) @ w_out
    else:
        o_ctx = None
    return o_lat, o_ctx


def mixer_c(h_lat, h_ctx, w_in, w_out, lam_p, subln_g, cos, sin, lambda_init, want_ctx):
    def project(h):
        b, n, _ = h.shape
        q, k, v = jnp.split(h @ w_in, [C_QK_WIDTH, 2 * C_QK_WIDTH], axis=-1)
        return (q.reshape(b, n, C_HEADS, 2, HEAD_DIM), k.reshape(b, n, C_HEADS, 2, HEAD_DIM),
                v.reshape(b, n, C_HEADS, C_V_DIM))

    def rope2(t):
        b, n = t.shape[:2]
        return apply_rope(t.reshape(b, n, C_HEADS * 2, HEAD_DIM), cos, sin).reshape(t.shape)

    def finish(o):
        b, n = o.shape[:2]
        return (rms_norm(o, subln_g) * (1 - lambda_init)).reshape(b, n, C_OUT) @ w_out

    q, k, v = project(h_lat)
    q, k = rope2(q), rope2(k)
    qc, kc, vc = project(h_ctx)
    lp = lam_p.astype(jnp.float32)
    lam = jnp.exp(jnp.sum(lp[0] * lp[1])) - jnp.exp(jnp.sum(lp[2] * lp[3])) + lambda_init
    k_all = jnp.concatenate([kc, k], axis=1)
    v_all = jnp.concatenate([vc, v], axis=1)
    o_lat = finish(sweep_query_blocks(lambda qq: diff_attend(qq, k_all, v_all, lam), q))
    o_ctx = finish(diff_attend(qc, kc, vc, lam)) if want_ctx else None
    return o_lat, o_ctx


def moe(xt, router_w, router_bias, w1, w3, w2, sw1, sw3, sw2):
    t, d = xt.shape
    scores = jax.nn.sigmoid(xt.astype(jnp.float32) @ router_w.astype(jnp.float32))
    sel = scores + router_bias.astype(jnp.float32)
    group_score = lax.top_k(sel.reshape(t, N_GROUPS, N_EXPERTS // N_GROUPS), 2)[0].sum(-1)
    _, gidx = lax.top_k(group_score, TOPK_GROUPS)
    gmask = jax.nn.one_hot(gidx, N_GROUPS, dtype=jnp.float32).sum(1) > 0
    emask = jnp.repeat(gmask, N_EXPERTS // N_GROUPS, axis=1)
    _, idx = lax.top_k(jnp.where(emask, sel, NEG_INF), TOP_K)
    wts = jnp.take_along_axis(scores, idx, axis=1)
    wts = wts / jnp.sum(wts, axis=-1, keepdims=True) * ROUTED_SCALE

    flat_e = idx.reshape(-1)
    order = jnp.argsort(flat_e)
    sorted_e = flat_e[order]
    sorted_tok = order // TOP_K
    sorted_w = wts.reshape(-1)[order]
    counts = jnp.bincount(flat_e, length=N_EXPERTS)
    padded = (counts + EBLK - 1) // EBLK * EBLK
    pad_end = jnp.cumsum(padded)
    pad_start = pad_end - padded
    start = jnp.cumsum(counts) - counts
    dest = pad_start[sorted_e] + jnp.arange(t * TOP_K) - start[sorted_e]
    n_blocks = -(-(t * TOP_K) // EBLK) + N_EXPERTS
    n_slots = n_blocks * EBLK
    slot_tok = jnp.full((n_slots,), t, jnp.int32).at[dest].set(sorted_tok.astype(jnp.int32))
    slot_w = jnp.zeros((n_slots,), xt.dtype).at[dest].set(sorted_w.astype(xt.dtype))
    block_e = jnp.minimum(jnp.searchsorted(pad_end, jnp.arange(n_blocks) * EBLK, side='right'),
                          N_EXPERTS - 1)
    xpad = jnp.concatenate([xt, jnp.zeros((1, d), xt.dtype)], axis=0)

    def expert_block(args):
        tok, e = args
        xb = xpad[tok]
        h = jax.nn.silu(xb @ w1[e]) * (xb @ w3[e])
        return h @ w2[e]

    yb = lax.map(expert_block, (slot_tok.reshape(n_blocks, EBLK), block_e))
    routed = jnp.zeros((t + 1, d), xt.dtype).at[slot_tok].add(yb.reshape(n_slots, d) * slot_w[:, None])[:t]
    shared = (jax.nn.silu(xt @ sw1) * (xt @ sw3)) @ sw2
    return routed + shared


def setup_inputs(seed: int = 0) -> dict:
    key = jax.random.key(seed)
    ks = jax.random.split(key, 26)
    f32 = jnp.float32

    def nrm(k, shape, fan_in, gain=1.0):
        return jax.random.normal(k, shape, f32) * (gain * fan_in ** -0.5)

    def gain(k, shape):
        return 1.0 + 0.05 * jax.random.normal(k, shape, f32)

    return {
        "x": jax.random.normal(ks[0], (BATCH, SEQ, D_MODEL), f32),
        "c": jax.random.normal(ks[1], (BATCH, D_MODEL), f32),
        "ctx": jax.random.normal(ks[2], (BATCH, CTX_LEN, D_MODEL), f32),
        "c_ctx": jax.random.normal(ks[3], (D_MODEL,), f32),
        "ada_w": nrm(ks[4], (DEPTH, D_MODEL, 6 * D_MODEL), D_MODEL, 0.5),
        "ada_b": 0.02 * jax.random.normal(ks[5], (DEPTH, 6 * D_MODEL), f32),
        "norm_g": gain(ks[6], (DEPTH, 4, D_MODEL)),
        "ab_w_in": nrm(ks[7], (N_EVEN, D_MODEL, AB_IN), D_MODEL),
        "ab_w_out": nrm(ks[8], (N_EVEN, AB_OUT, D_MODEL), AB_OUT),
        "a_q_norm": gain(ks[9], (N_EVEN, HEAD_DIM)),
        "a_k_norm": gain(ks[10], (N_EVEN, HEAD_DIM)),
        "b_sink": 0.5 * jax.random.normal(ks[11], (N_EVEN, B_HEADS), f32),
        "c_w_in": nrm(ks[12], (N_ODD, D_MODEL, C_IN), D_MODEL),
        "c_w_out": nrm(ks[13], (N_ODD, C_OUT, D_MODEL), C_OUT),
        "c_lambda": 0.1 * jax.random.normal(ks[14], (N_ODD, 4, HEAD_DIM), f32),
        "c_subln_g": gain(ks[15], (N_ODD, C_V_DIM)),
        "router_w": nrm(ks[16], (DEPTH, D_MODEL, N_EXPERTS), D_MODEL),
        "router_bias": 0.01 * jax.random.normal(ks[17], (DEPTH, N_EXPERTS), f32),
        "exp_w1": nrm(ks[18], (DEPTH, N_EXPERTS, D_MODEL, EXPERT_HIDDEN), D_MODEL),
        "exp_w3": nrm(ks[19], (DEPTH, N_EXPERTS, D_MODEL, EXPERT_HIDDEN), D_MODEL),
        "exp_w2": nrm(ks[20], (DEPTH, N_EXPERTS, EXPERT_HIDDEN, D_MODEL), EXPERT_HIDDEN),
        "sh_w1": nrm(ks[21], (DEPTH, D_MODEL, SHARED_HIDDEN), D_MODEL),
        "sh_w3": nrm(ks[22], (DEPTH, D_MODEL, SHARED_HIDDEN), D_MODEL),
        "sh_w2": nrm(ks[23], (DEPTH, SHARED_HIDDEN, D_MODEL), SHARED_HIDDEN),
    }


def reference(x, c, ctx, c_ctx, ada_w, ada_b, norm_g, ab_w_in, ab_w_out, a_q_norm, a_k_norm, b_sink,
              c_w_in, c_w_out, c_lambda, c_subln_g, router_w, router_bias, exp_w1, exp_w3, exp_w2,
              sh_w1, sh_w3, sh_w2):
    b, n, d = x.shape
    rows = n // GRID_W
    cos, sin = axial_rope(rows, x.dtype)
    sc = jax.nn.silu(c)
    scc = jax.nn.silu(c_ctx)
    x_lat, x_ctx = x, ctx
    lc = ctx.shape[1]

    for l in range(DEPTH):
        last = l == DEPTH - 1
        mod = sc @ ada_w[l] + ada_b[l]
        mod_c = scc @ ada_w[l] + ada_b[l]
        sh1, s1, g1, sh2, s2, g2 = jnp.split(mod[:, None, :], 6, axis=-1)
        csh1, cs1, cg1, csh2, cs2, cg2 = jnp.split(mod_c, 6, axis=-1)

        h_lat = modulate(rms_norm(x_lat, norm_g[l, 0]), sh1, s1)
        h_ctx = modulate(rms_norm(x_ctx, norm_g[l, 0]), csh1, cs1)
        j = l // 2
        if l % 2 == 0:
            o_lat, o_ctx = mixer_ab(h_lat, h_ctx, ab_w_in[j], ab_w_out[j], a_q_norm[j], a_k_norm[j],
                                    b_sink[j], cos, sin, not last)
        else:
            lambda_init = 0.8 - 0.6 * math.exp(-0.3 * l)
            o_lat, o_ctx = mixer_c(h_lat, h_ctx, c_w_in[j], c_w_out[j], c_lambda[j], c_subln_g[j],
                                   cos, sin, lambda_init, not last)
        x_lat = x_lat + g1 * rms_norm(o_lat, norm_g[l, 1])
        if not last:
            x_ctx = x_ctx + cg1 * rms_norm(o_ctx, norm_g[l, 1])

        h_lat = modulate(rms_norm(x_lat, norm_g[l, 2]), sh2, s2).reshape(b * n, d)
        if not last:
            h_ctx = modulate(rms_norm(x_ctx, norm_g[l, 2]), csh2, cs2).reshape(b * lc, d)
            y = moe(jnp.concatenate([h_ctx, h_lat], axis=0), router_w[l], router_bias[l], exp_w1[l],
                    exp_w3[l], exp_w2[l], sh_w1[l], sh_w3[l], sh_w2[l])
            y_ctx = y[:b * lc].reshape(b, lc, d)
            y_lat = y[b * lc:].reshape(b, n, d)
            x_ctx = x_ctx + cg2 * rms_norm(y_ctx, norm_g[l, 3])
        else:
            y_lat = moe(h_lat, router_w[l], router_bias[l], exp_w1[l], exp_w3[l], exp_w2[l],
                        sh_w1[l], sh_w3[l], sh_w2[l]).reshape(b, n, d)
        x_lat = x_lat + g2 * rms_norm(y_lat, norm_g[l, 3])

    return x_lat
```

```python
import functools
import math

import numpy as np
import jax
import jax.numpy as jnp
from jax import lax
from jax.experimental import pallas as pl
from jax.experimental.pallas import tpu as pltpu

F32 = jnp.float32
BF16 = jnp.bfloat16
U32 = jnp.uint32
I32 = jnp.int32
HIGHEST = lax.Precision.HIGHEST

D_MODEL = 1024
HEAD_DIM = 64
LANES = 128
GRID_W = 64
ROPE_THETA = 10000.0
EPS = 1e-6
NEG_INF = -1e30
WINDOW = 128
N_EXPERTS = 64
TOP_K = 8
N_GROUPS = 8
TOPK_GROUPS = 4
GROUP_SIZE = N_EXPERTS // N_GROUPS
ROUTED_SCALE = 2.5

ROW_TILE = 256
Q_TILE = 256
KV_TILE = 256
ROUTE_TILE = 512
MOVE_TILE = 128
EXPERT_BLOCK = 256
HALF = D_MODEL // 2
VMEM_LIMIT = 48 * 1024 * 1024


def _cparams(sem):
    return pltpu.CompilerParams(dimension_semantics=sem, vmem_limit_bytes=VMEM_LIMIT)


def _rms(x, g):
    ms = jnp.mean(x * x, axis=-1, keepdims=True)
    return x * lax.rsqrt(ms + EPS) * g


def _pack_halves(h):
    lo = lax.bitcast_convert_type(h[:, :HALF].astype(BF16).astype(F32), U32) >> 16
    hi = lax.bitcast_convert_type(h[:, HALF:].astype(BF16).astype(F32), U32) & jnp.uint32(0xFFFF0000)
    return hi | lo


def _unpack_halves(u):
    lo = lax.bitcast_convert_type(u << 16, F32)
    hi = lax.bitcast_convert_type(u & jnp.uint32(0xFFFF0000), F32)
    return lo, hi


def _mod_kernel(c_ref, w_ref, b_ref, o_ref):
    c = c_ref[...]
    sc = c * jax.nn.sigmoid(c)
    o_ref[0] = jnp.dot(sc, w_ref[0], precision=HIGHEST, preferred_element_type=F32) + b_ref[0]


def _modulation(cond, ada_w, ada_b):
    depth, d, n = ada_w.shape
    nt = 1536
    return pl.pallas_call(
        _mod_kernel,
        out_shape=jax.ShapeDtypeStruct((depth, 8, n), F32),
        grid=(depth, n // nt),
        in_specs=[pl.BlockSpec((8, d), lambda l, j: (0, 0)),
                  pl.BlockSpec((1, d, nt), lambda l, j: (l, 0, j)),
                  pl.BlockSpec((1, 1, nt), lambda l, j: (l, 0, j))],
        out_specs=pl.BlockSpec((1, 8, nt), lambda l, j: (l, 0, j)),
        compiler_params=_cparams(("arbitrary", "arbitrary")),
        name="ada_mod",
    )(cond, ada_w, ada_b.reshape(depth, 1, n))


def _proj_kernel(x_ref, sh_ref, sc_ref, g_ref, w_ref, cos_ref, sin_ref, hg_ref, gm_ref, o_ref, *, blocks):
    h = _rms(x_ref[...], g_ref[...]) * (1.0 + sc_ref[...]) + sh_ref[...]
    y = jnp.dot(h.astype(BF16), w_ref[...], preferred_element_type=F32)
    lane = lax.broadcasted_iota(I32, (x_ref.shape[0], LANES), 1)
    first_half = (lane % HEAD_DIM) < (HEAD_DIM // 2)
    for jb, (norm, rope, scale) in enumerate(blocks):
        cols = slice(jb * LANES, (jb + 1) * LANES)
        yb = y[:, cols]
        if norm:
            ms = jnp.dot(yb * yb, gm_ref[...], precision=HIGHEST, preferred_element_type=F32)
            yb = yb * lax.rsqrt(ms + EPS) * hg_ref[:, cols]
        if rope:
            swapped = jnp.where(first_half, pltpu.roll(yb, LANES - HEAD_DIM // 2, 1),
                                pltpu.roll(yb, HEAD_DIM // 2, 1))
            yb = yb * cos_ref[...] + swapped * sin_ref[...]
        if scale != 1.0:
            yb = yb * scale
        o_ref[:, cols] = yb.astype(BF16)


def _project(x, modr, mod_base, row_of_tile, g, w, cos_t, sin_t, head_gain, blocks, tiles_per_batch):
    t, d = x.shape
    n = w.shape[1]
    group_mean = jnp.asarray(np.kron(np.eye(LANES // HEAD_DIM), np.full((HEAD_DIM, HEAD_DIM), 1.0 / HEAD_DIM)), F32)
    return pl.pallas_call(
        functools.partial(_proj_kernel, blocks=blocks),
        out_shape=jax.ShapeDtypeStruct((t, n), BF16),
        grid=(t // ROW_TILE,),
        in_specs=[pl.BlockSpec((ROW_TILE, d), lambda i: (i, 0)),
                  pl.BlockSpec((None, 1, d), lambda i: (mod_base + row_of_tile(i) * 6 + 0, 0, 0)),
                  pl.BlockSpec((None, 1, d), lambda i: (mod_base + row_of_tile(i) * 6 + 1, 0, 0)),
                  pl.BlockSpec((1, d), lambda i: (0, 0)),
                  pl.BlockSpec((d, n), lambda i: (0, 0)),
                  pl.BlockSpec((ROW_TILE, LANES), lambda i: (i % tiles_per_batch, 0)),
                  pl.BlockSpec((ROW_TILE, LANES), lambda i: (i % tiles_per_batch, 0)),
                  pl.BlockSpec((1, n), lambda i: (0, 0)),
                  pl.BlockSpec((LANES, LANES), lambda i: (0, 0))],
        out_specs=pl.BlockSpec((ROW_TILE, n), lambda i: (i, 0)),
        compiler_params=_cparams(("arbitrary",)),
        name="prenorm_proj",
    )(x, modr, modr, g, w, cos_t, sin_t, head_gain, group_mean)


def _attn_kernel(*refs, mode, n_kv, seq, ctx_len, ctx_queries, lambda_init):
    if mode == "window":
        q_ref, k_ref, v_ref, sink_ref, o_ref, m_ref, l_ref, acc_ref = refs
    elif mode == "diff":
        q_ref, k_ref, v_ref, lam_ref, sg_ref, o_ref, m_ref, l_ref, acc_ref = refs
    else:
        q_ref, k_ref, v_ref, o_ref, m_ref, l_ref, acc_ref = refs
    tq = q_ref.shape[0]
    j = pl.program_id(1)
    qi = pl.program_id(2)
    is_ctx_q = (qi == 0) if ctx_queries else False

    q = q_ref[...]
    lane = lax.broadcasted_iota(I32, (tq, LANES), 1)
    low = lane < HEAD_DIM
    zero = jnp.zeros_like(q)
    q2 = jnp.concatenate([jnp.where(low, q, zero), jnp.where(low, zero, q)], axis=0)

    if mode == "window":
        s0 = sink_ref[pl.ds(2 * j, 1), :]
        s1 = sink_ref[pl.ds(2 * j + 1, 1), :]
        row = lax.broadcasted_iota(I32, (2 * tq, LANES), 0)
        sink = jnp.where(row < tq, s0, s1)
        m_ref[...] = sink
    else:
        m_ref[...] = jnp.full(m_ref.shape, NEG_INF, F32)
    l_ref[...] = jnp.zeros(l_ref.shape, F32)
    acc_ref[...] = jnp.zeros(acc_ref.shape, F32)

    def chunk(start, valid):
        k = k_ref[pl.ds(start, KV_TILE), :]
        v = v_ref[pl.ds(start, KV_TILE), :]
        s = lax.dot_general(q2, k, (((1,), (1,)), ((), ())), preferred_element_type=F32)
        if valid is not None:
            s = jnp.where(valid, s, NEG_INF)
        m_prev = m_ref[...]
        m_new = jnp.maximum(m_prev, jnp.max(s, axis=1, keepdims=True))
        alpha = jnp.exp(m_prev - m_new)
        p = jnp.exp(s - jnp.concatenate([m_new] * (KV_TILE // LANES), axis=1))
        part = p[:, :LANES]
        for c in range(1, KV_TILE // LANES):
            part = part + p[:, c * LANES:(c + 1) * LANES]
        l_ref[...] = alpha * l_ref[...] + part
        acc_ref[...] = alpha * acc_ref[...] + jnp.dot(p.astype(BF16), v, preferred_element_type=F32)
        m_ref[...] = m_new

    if mode == "window":
        chunk(0, None)

        @pl.when(jnp.logical_not(is_ctx_q))
        def _():
            q0 = (qi - (1 if ctx_queries else 0)) * tq
            kstart = jnp.clip(q0 - WINDOW, 0, seq - 2 * KV_TILE)
            r = lax.broadcasted_iota(I32, (2 * tq, KV_TILE), 0)
            qpos = q0 + jnp.where(r >= tq, r - tq, r)
            col = lax.broadcasted_iota(I32, (2 * tq, KV_TILE), 1)
            for w in range(2):
                kpos = kstart + w * KV_TILE + col
                chunk(pl.multiple_of(ctx_len + kstart + w * KV_TILE, WINDOW), jnp.abs(qpos - kpos) <= WINDOW)
    else:
        n = jnp.where(is_ctx_q, ctx_len // KV_TILE, n_kv) if ctx_queries else n_kv

        def body(c, carry):
            chunk(pl.multiple_of(c * KV_TILE, KV_TILE), None)
            return carry

        lax.fori_loop(0, n, body, 0)

    l = jnp.sum(l_ref[...], axis=1, keepdims=True)
    if mode == "window":
        l = l + jnp.exp(sink - m_ref[...])[:, :1]
    o2 = acc_ref[...] / l
    if mode == "diff":
        lp = lam_ref[...]
        lam = (jnp.exp(jnp.sum(lp[0:1] * lp[1:2], axis=1, keepdims=True))
               - jnp.exp(jnp.sum(lp[2:3] * lp[3:4], axis=1, keepdims=True)) + lambda_init)
        o = o2[:tq] - lam * o2[tq:]
        o = _rms(o, sg_ref[...]) * (1.0 - lambda_init)
    else:
        o = jnp.where(low, o2[:tq], o2[tq:])
    o_ref[...] = o.astype(BF16)


def _attention(p, extra, *, mode, batch, seq, ctx_len, n_qblocks, q_col0, k_col, v_col, ctx_queries,
               lambda_init=0.0):
    n_p = ctx_len + seq
    tiles = n_p // Q_TILE
    qt = tiles if ctx_queries else seq // Q_TILE
    q_off = 0 if ctx_queries else ctx_len // Q_TILE
    in_specs = [pl.BlockSpec((Q_TILE, LANES), lambda b, j, qi: (b * tiles + q_off + qi, q_col0 + j)),
                pl.BlockSpec((n_p, LANES), lambda b, j, qi: (b, k_col(j))),
                pl.BlockSpec((n_p, LANES), lambda b, j, qi: (b, v_col(j)))]
    args = [p, p, p]
    for e in extra:
        in_specs.append(pl.BlockSpec(e.shape, lambda b, j, qi: (0, 0)))
        args.append(e)
    kern = functools.partial(_attn_kernel, mode=mode, n_kv=n_p // KV_TILE, seq=seq, ctx_len=ctx_len,
                             ctx_queries=ctx_queries, lambda_init=lambda_init)
    return pl.pallas_call(
        kern,
        out_shape=jax.ShapeDtypeStruct((batch * qt * Q_TILE, n_qblocks * LANES), BF16),
        grid=(batch, n_qblocks, qt),
        in_specs=in_specs,
        out_specs=pl.BlockSpec((Q_TILE, LANES), lambda b, j, qi: (b * qt + qi, j)),
        scratch_shapes=[pltpu.VMEM((2 * Q_TILE, LANES), F32)] * 3,
        compiler_params=_cparams(("arbitrary", "arbitrary", "arbitrary")),
        name="attn_" + mode,
    )(*args)


def _out_kernel(*refs, n_o):
    o_refs = refs[:n_o]
    w_refs = refs[n_o:2 * n_o]
    x_ref, g1_ref, gate_ref, g2_ref, sh_ref, sc_ref, rw_ref, xn_ref, hp_ref, lg_ref = refs[2 * n_o:]
    y = jnp.dot(o_refs[0][...], w_refs[0][...], preferred_element_type=F32)
    for a in range(1, n_o):
        y = y + jnp.dot(o_refs[a][...], w_refs[a][...], preferred_element_type=F32)
    xn = x_ref[...] + gate_ref[...] * _rms(y, g1_ref[...])
    xn_ref[...] = xn
    h = _rms(xn, g2_ref[...]) * (1.0 + sc_ref[...]) + sh_ref[...]
    hp_ref[...] = _pack_halves(h)
    lg_ref[...] = lax.dot_general(rw_ref[...], h, (((1,), (1,)), ((), ())), precision=HIGHEST,
                                  preferred_element_type=F32)


def _out_project(os_, ws, x, x_tile, modr, mod_base, row_of_tile, g1, g2, rw_t, n_tiles):
    d = x.shape[1]
    n_o = len(os_)
    tm = n_tiles * ROW_TILE
    mspec = lambda which: pl.BlockSpec((None, 1, d), lambda i: (mod_base + row_of_tile(i) * 6 + which, 0, 0))
    in_specs = ([pl.BlockSpec((ROW_TILE, o.shape[1]), lambda i: (i, 0)) for o in os_]
                + [pl.BlockSpec(w.shape, lambda i: (0, 0)) for w in ws]
                + [pl.BlockSpec((ROW_TILE, d), lambda i: (x_tile(i), 0)),
                   pl.BlockSpec((1, d), lambda i: (0, 0)), mspec(2),
                   pl.BlockSpec((1, d), lambda i: (0, 0)), mspec(3), mspec(4),
                   pl.BlockSpec(rw_t.shape, lambda i: (0, 0))])
    return pl.pallas_call(
        functools.partial(_out_kernel, n_o=n_o),
        out_shape=(jax.ShapeDtypeStruct((tm, d), F32), jax.ShapeDtypeStruct((tm, HALF), U32),
                   jax.ShapeDtypeStruct((N_EXPERTS, tm), F32)),
        grid=(n_tiles,),
        in_specs=in_specs,
        out_specs=(pl.BlockSpec((ROW_TILE, d), lambda i: (i, 0)),
                   pl.BlockSpec((ROW_TILE, HALF), lambda i: (i, 0)),
                   pl.BlockSpec((N_EXPERTS, ROW_TILE), lambda i: (0, i))),
        compiler_params=_cparams(("arbitrary",)),
        name="out_proj",
    )(*os_, *ws, x, g1, modr, g2, modr, modr, rw_t)


def _route_kernel(lg_ref, bias_ref, tri_ref, idx_ref, w_ref, pos_ref, cnt_ref, carry_ref):
    i = pl.program_id(0)

    @pl.when(i == 0)
    def _():
        carry_ref[...] = jnp.zeros(carry_ref.shape, F32)

    tr = lg_ref.shape[1]
    score = jax.nn.sigmoid(lg_ref[...])
    sel = score + bias_ref[...][:, :1]
    sel_j = [sel[j * GROUP_SIZE:(j + 1) * GROUP_SIZE] for j in range(GROUP_SIZE)]
    sc_j = [score[j * GROUP_SIZE:(j + 1) * GROUP_SIZE] for j in range(GROUP_SIZE)]
    gi = lax.broadcasted_iota(I32, (N_GROUPS, tr), 0)

    m1 = sel_j[0]
    m2 = jnp.full_like(m1, -jnp.inf)
    for j in range(1, GROUP_SIZE):
        m2 = jnp.maximum(m2, jnp.minimum(m1, sel_j[j]))
        m1 = jnp.maximum(m1, sel_j[j])
    gs = m1 + m2

    grank = jnp.zeros((N_GROUPS, tr), I32)
    for gp in range(N_GROUPS):
        rowv = gs[gp:gp + 1, :]
        grank = grank + jnp.where(gi > gp, jnp.where(rowv >= gs, 1, 0), jnp.where(rowv > gs, 1, 0))
    gmask = grank < TOPK_GROUPS
    val_j = [jnp.where(gmask, sel_j[j], NEG_INF) for j in range(GROUP_SIZE)]

    rank_j = [jnp.zeros((N_GROUPS, tr), I32) for _ in range(GROUP_SIZE)]
    for gp in range(N_GROUPS):
        after = gi > gp
        not_before = gi >= gp
        for jp in range(GROUP_SIZE):
            rowv = val_j[jp][gp:gp + 1, :]
            for j in range(GROUP_SIZE):
                tie = after if jp >= j else not_before
                rank_j[j] = rank_j[j] + jnp.where(tie, jnp.where(rowv >= val_j[j], 1, 0),
                                                  jnp.where(rowv > val_j[j], 1, 0))

    chosen = jnp.concatenate([jnp.where(rank_j[j] < TOP_K, 1.0, 0.0) for j in range(GROUP_SIZE)], axis=0)
    before = jnp.dot(chosen.astype(BF16), tri_ref[...], preferred_element_type=F32) + carry_ref[...][:, :1]
    carry_ref[...] = carry_ref[...] + jnp.sum(chosen, axis=1, keepdims=True)
    cnt_ref[...] = carry_ref[...].astype(I32)

    idx_rows, w_rows, pos_rows = [], [], []
    for k in range(TOP_K):
        e_acc = jnp.zeros((N_GROUPS, tr), I32)
        w_acc = jnp.zeros((N_GROUPS, tr), F32)
        p_acc = jnp.zeros((N_GROUPS, tr), F32)
        for j in range(GROUP_SIZE):
            hit = rank_j[j] == k
            e_acc = e_acc + jnp.where(hit, gi * GROUP_SIZE + j, 0)
            w_acc = w_acc + jnp.where(hit, sc_j[j], 0.0)
            p_acc = p_acc + jnp.where(hit, before[j * GROUP_SIZE:(j + 1) * GROUP_SIZE], 0.0)
        idx_rows.append(jnp.sum(e_acc, axis=0, keepdims=True))
        w_rows.append(jnp.sum(w_acc, axis=0, keepdims=True))
        pos_rows.append(jnp.sum(p_acc, axis=0, keepdims=True))
    w_all = jnp.concatenate(w_rows, axis=0)
    idx_ref[...] = jnp.concatenate(idx_rows, axis=0)
    w_ref[...] = w_all / jnp.sum(w_all, axis=0, keepdims=True) * ROUTED_SCALE
    pos_ref[...] = jnp.concatenate(pos_rows, axis=0).astype(I32)


def _route(lg_t, bias_rep):
    tm = lg_t.shape[1]
    tri = jnp.asarray(np.triu(np.ones((ROUTE_TILE, ROUTE_TILE), np.float32), 1), BF16)
    tok = lambda i: (0, i)
    return pl.pallas_call(
        _route_kernel,
        out_shape=(jax.ShapeDtypeStruct((TOP_K, tm), I32), jax.ShapeDtypeStruct((TOP_K, tm), F32),
                   jax.ShapeDtypeStruct((TOP_K, tm), I32), jax.ShapeDtypeStruct((N_EXPERTS, LANES), I32)),
        grid=(tm // ROUTE_TILE,),
        in_specs=[pl.BlockSpec((N_EXPERTS, ROUTE_TILE), tok),
                  pl.BlockSpec((N_EXPERTS, LANES), lambda i: (0, 0)),
                  pl.BlockSpec((ROUTE_TILE, ROUTE_TILE), lambda i: (0, 0))],
        out_specs=(pl.BlockSpec((TOP_K, ROUTE_TILE), tok), pl.BlockSpec((TOP_K, ROUTE_TILE), tok),
                   pl.BlockSpec((TOP_K, ROUTE_TILE), tok), pl.BlockSpec((N_EXPERTS, LANES), lambda i: (0, 0))),
        scratch_shapes=[pltpu.VMEM((N_EXPERTS, LANES), F32)],
        compiler_params=_cparams(("arbitrary",)),
        name="route",
    )(lg_t, bias_rep, tri)


def _row_copy(src, s, dst, d, sem):
    return pltpu.make_async_copy(src.at[pl.ds(s, 1), :], dst.at[pl.ds(d, 1), :], sem)


def _dispatch_kernel(zstart_ref, zlen_ref, dest_hbm, hp_ref, xs_hbm, dest_smem, zrow_ref, sem, dsem):
    i = pl.program_id(0)
    n_idx = MOVE_TILE * TOP_K
    cp = pltpu.make_async_copy(dest_hbm.at[pl.ds(i * n_idx, n_idx)], dest_smem, dsem)
    cp.start()
    cp.wait()

    def issue(t, carry):
        for k in range(TOP_K):
            _row_copy(hp_ref, t, xs_hbm, dest_smem[t * TOP_K + k], sem).start()
        return carry

    lax.fori_loop(0, MOVE_TILE, issue, 0)

    def drain(t, carry):
        _row_copy(hp_ref, 0, xs_hbm, 0, sem).wait()
        return carry

    lax.fori_loop(0, n_idx, drain, 0)

    @pl.when(i == 0)
    def _():
        zrow_ref[...] = jnp.zeros(zrow_ref.shape, U32)

        def per_expert(e, carry):
            z0 = zstart_ref[e]

            def fill(r, c):
                _row_copy(zrow_ref, 0, xs_hbm, z0 + r, sem).start()
                return c

            lax.fori_loop(0, zlen_ref[e], fill, 0)

            def fill_wait(r, c):
                _row_copy(zrow_ref, 0, xs_hbm, 0, sem).wait()
                return c

            lax.fori_loop(0, zlen_ref[e], fill_wait, 0)
            return carry

        lax.fori_loop(0, N_EXPERTS, per_expert, 0)


def _dispatch(zstart, zlen, dest, hp, n_slots):
    tm = hp.shape[0]
    return pl.pallas_call(
        _dispatch_kernel,
        out_shape=jax.ShapeDtypeStruct((n_slots, HALF), U32),
        grid_spec=pltpu.PrefetchScalarGridSpec(
            num_scalar_prefetch=2,
            grid=(tm // MOVE_TILE,),
            in_specs=[pl.BlockSpec(memory_space=pl.ANY),
                      pl.BlockSpec((MOVE_TILE, HALF), lambda i, zs, zl: (i, 0))],
            out_specs=pl.BlockSpec(memory_space=pl.ANY),
            scratch_shapes=[pltpu.SMEM((MOVE_TILE * TOP_K,), I32), pltpu.VMEM((8, HALF), U32),
                            pltpu.SemaphoreType.DMA, pltpu.SemaphoreType.DMA]),
        compiler_params=_cparams(("arbitrary",)),
        name="dispatch",
    )(zstart, zlen, dest, hp)


def _expert_kernel(be_ref, nu_ref, xs_ref, w1_ref, w3_ref, w2_ref, ys_ref):
    @pl.when(pl.program_id(0) < nu_ref[0])
    def _():
        lo, hi = _unpack_halves(xs_ref[...])
        lo = lo.astype(BF16)
        hi = hi.astype(BF16)
        w1 = w1_ref[0].astype(BF16)
        w3 = w3_ref[0].astype(BF16)
        h1 = (jnp.dot(lo, w1[:HALF], preferred_element_type=F32)
              + jnp.dot(hi, w1[HALF:], preferred_element_type=F32))
        h3 = (jnp.dot(lo, w3[:HALF], preferred_element_type=F32)
              + jnp.dot(hi, w3[HALF:], preferred_element_type=F32))
        g = (h1 * jax.nn.sigmoid(h1) * h3).astype(BF16)
        y = jnp.dot(g, w2_ref[0].astype(BF16), preferred_element_type=F32)
        ys_ref[...] = _pack_halves(y)


def _experts(block_e, n_used, xs, w1, w3, w2):
    n_slots = xs.shape[0]
    n_blocks = n_slots // EXPERT_BLOCK
    d, eh = w1.shape[1], w1.shape[2]
    rows = lambda i, be, nu: (jnp.minimum(i, nu[0] - 1), 0)
    return pl.pallas_call(
        _expert_kernel,
        out_shape=jax.ShapeDtypeStruct((n_slots, HALF), U32),
        grid_spec=pltpu.PrefetchScalarGridSpec(
            num_scalar_prefetch=2,
            grid=(n_blocks,),
            in_specs=[pl.BlockSpec((EXPERT_BLOCK, HALF), rows),
                      pl.BlockSpec((1, d, eh), lambda i, be, nu: (be[i], 0, 0)),
                      pl.BlockSpec((1, d, eh), lambda i, be, nu: (be[i], 0, 0)),
                      pl.BlockSpec((1, eh, d), lambda i, be, nu: (be[i], 0, 0))],
            out_specs=pl.BlockSpec((EXPERT_BLOCK, HALF), rows)),
        compiler_params=_cparams(("arbitrary",)),
        name="experts",
    )(block_e, n_used, xs, w1, w3, w2)


def _combine_kernel(dest_hbm, ys_hbm, wt_ref, hp_ref, x_ref, s1_ref, s3_ref, s2_ref, g_ref, gate_ref, o_ref,
                    dest_smem, buf_ref, sem, dsem):
    i = pl.program_id(0)
    n_idx = MOVE_TILE * TOP_K
    cp = pltpu.make_async_copy(dest_hbm.at[pl.ds(i * n_idx, n_idx)], dest_smem, dsem)
    cp.start()
    cp.wait()

    def issue(t, carry):
        for k in range(TOP_K):
            _row_copy(ys_hbm, dest_smem[t * TOP_K + k], buf_ref.at[k], t, sem).start()
        return carry

    lax.fori_loop(0, MOVE_TILE, issue, 0)

    lo, hi = _unpack_halves(hp_ref[...])
    lo = lo.astype(BF16)
    hi = hi.astype(BF16)
    h1 = (jnp.dot(lo, s1_ref[:HALF, :], preferred_element_type=F32)
          + jnp.dot(hi, s1_ref[HALF:, :], preferred_element_type=F32))
    h3 = (jnp.dot(lo, s3_ref[:HALF, :], preferred_element_type=F32)
          + jnp.dot(hi, s3_ref[HALF:, :], preferred_element_type=F32))
    y = jnp.dot((h1 * jax.nn.sigmoid(h1) * h3).astype(BF16), s2_ref[...], preferred_element_type=F32)

    def drain(t, carry):
        _row_copy(ys_hbm, 0, buf_ref.at[0], 0, sem).wait()
        return carry

    lax.fori_loop(0, n_idx, drain, 0)

    wt = wt_ref[...]
    r_lo = jnp.zeros((MOVE_TILE, HALF), F32)
    r_hi = jnp.zeros((MOVE_TILE, HALF), F32)
    for k in range(TOP_K):
        a, b = _unpack_halves(buf_ref[k])
        wk = wt[:, k:k + 1]
        r_lo = r_lo + wk * a
        r_hi = r_hi + wk * b
    y = y + jnp.concatenate([r_lo, r_hi], axis=1)
    o_ref[...] = x_ref[...] + gate_ref[...] * _rms(y, g_ref[...])


def _combine(dest, ys, wt, hp, x, s1, s3, s2, g, modr, mod_base, row_of_tile):
    tm, d = x.shape
    full = lambda a: pl.BlockSpec(a.shape, lambda i: (0, 0))
    return pl.pallas_call(
        _combine_kernel,
        out_shape=jax.ShapeDtypeStruct((tm, d), F32),
        grid=(tm // MOVE_TILE,),
        in_specs=[pl.BlockSpec(memory_space=pl.ANY), pl.BlockSpec(memory_space=pl.ANY),
                  pl.BlockSpec((MOVE_TILE, TOP_K), lambda i: (i, 0)),
                  pl.BlockSpec((MOVE_TILE, HALF), lambda i: (i, 0)),
                  pl.BlockSpec((MOVE_TILE, d), lambda i: (i, 0)),
                  full(s1), full(s3), full(s2), full(g),
                  pl.BlockSpec((None, 1, d), lambda i: (mod_base + row_of_tile(i) * 6 + 5, 0, 0))],
        out_specs=pl.BlockSpec((MOVE_TILE, d), lambda i: (i, 0)),
        scratch_shapes=[pltpu.SMEM((MOVE_TILE * TOP_K,), I32), pltpu.VMEM((TOP_K, MOVE_TILE, HALF), U32),
                        pltpu.SemaphoreType.DMA, pltpu.SemaphoreType.DMA],
        compiler_params=_cparams(("arbitrary",)),
        name="combine",
    )(dest, ys, wt, hp, x, s1, s3, s2, g, modr)


def _moe(xn, hp, lg_t, bias, w1, w3, w2, s1, s3, s2, g, modr, mod_base, row_of_tile):
    tm = xn.shape[0]
    perm = np.array([(r % GROUP_SIZE) * GROUP_SIZE + r // GROUP_SIZE for r in range(N_EXPERTS)])
    bias_rep = jnp.broadcast_to(bias.astype(F32)[perm][:, None], (N_EXPERTS, LANES))
    idx, w, pos, cnt = _route(lg_t, bias_rep)

    counts = cnt[:, 0].reshape(GROUP_SIZE, N_GROUPS).T.reshape(N_EXPERTS)
    padded = (counts + EXPERT_BLOCK - 1) // EXPERT_BLOCK * EXPERT_BLOCK
    pad_end = jnp.cumsum(padded)
    pad_start = pad_end - padded
    n_blocks = tm * TOP_K // EXPERT_BLOCK + N_EXPERTS
    n_slots = n_blocks * EXPERT_BLOCK
    dest = (pad_start[idx] + pos).T.reshape(-1).astype(I32)
    block_e = jnp.minimum(jnp.searchsorted(pad_end, jnp.arange(n_blocks, dtype=I32) * EXPERT_BLOCK, side="right"),
                          N_EXPERTS - 1).astype(I32)
    n_used = (pad_end[-1:] // EXPERT_BLOCK).astype(I32)

    xs = _dispatch((pad_start + counts).astype(I32), (padded - counts).astype(I32), dest, hp, n_slots)
    ys = _experts(block_e, n_used, xs, w1, w3, w2)
    return _combine(dest, ys, w.T, hp, xn, s1.astype(BF16), s3.astype(BF16), s2.astype(BF16), g, modr, mod_base,
                    row_of_tile)


_DEINTERLEAVE = np.concatenate([np.arange(0, HEAD_DIM, 2), np.arange(1, HEAD_DIM, 2)])


def _rope_tables(seq, ctx_len):
    t = np.arange(seq)
    n_pair = HEAD_DIM // 4
    inv = jnp.asarray(ROPE_THETA, F32) ** (-jnp.arange(n_pair, dtype=F32) / n_pair)
    r = jnp.asarray(t // GRID_W, F32)
    c = jnp.asarray(t % GRID_W, F32)
    ang = jnp.concatenate([r[:, None] * inv, c[:, None] * inv], axis=-1)
    cos, sin = jnp.cos(ang), jnp.sin(ang)
    cos_t = jnp.tile(jnp.concatenate([cos, cos], axis=-1), (1, LANES // HEAD_DIM))
    sin_t = jnp.tile(jnp.concatenate([-sin, sin], axis=-1), (1, LANES // HEAD_DIM))
    cos_t = jnp.concatenate([jnp.ones((ctx_len, LANES), F32), cos_t], axis=0)
    sin_t = jnp.concatenate([jnp.zeros((ctx_len, LANES), F32), sin_t], axis=0)
    return cos_t, sin_t


def _ab_layout():
    cols, blocks, gain_kind = [], [], []
    for base_q, base_k, base_v, normed in ((0, 512, 640, True), (768, 1280, 1408, False)):
        for jb in range(4):
            cols += [base_q + h * HEAD_DIM + _DEINTERLEAVE for h in (2 * jb, 2 * jb + 1)]
            blocks.append((normed, True, HEAD_DIM ** -0.5))
            gain_kind.append("q" if normed else None)
        for kvh in range(2):
            cols += [base_k + kvh * HEAD_DIM + _DEINTERLEAVE] * 2
            blocks.append((normed, True, 1.0))
            gain_kind.append("k" if normed else None)
        for kvh in range(2):
            cols += [base_v + kvh * HEAD_DIM + np.arange(HEAD_DIM)] * 2
            blocks.append((False, False, 1.0))
            gain_kind.append(None)
    return np.concatenate(cols), tuple(blocks), gain_kind


def _c_layout():
    cols, blocks = [], []
    for base, rope, scale in ((0, True, HEAD_DIM ** -0.5), (1024, True, 1.0)):
        for h in range(8):
            cols += [base + (2 * h + m) * HEAD_DIM + _DEINTERLEAVE for m in range(2)]
            blocks.append((False, rope, scale))
    for h in range(8):
        cols.append(2048 + h * LANES + np.arange(LANES))
        blocks.append((False, False, 1.0))
    return np.concatenate(cols), tuple(blocks)


def kernel(x, c, ctx, c_ctx, ada_w, ada_b, norm_g, ab_w_in, ab_w_out, a_q_norm, a_k_norm, b_sink,
           c_w_in, c_w_out, c_lambda, c_subln_g, router_w, router_bias, exp_w1, exp_w3, exp_w2,
           sh_w1, sh_w3, sh_w2):
    batch, seq, d = x.shape
    ctx_len = ctx.shape[1]
    depth = ada_w.shape[0]
    assert d == D_MODEL and ctx_len == ROW_TILE and depth == 2
    assert seq % Q_TILE == 0 and seq >= 2 * KV_TILE and batch <= 4
    n_p = ctx_len + seq
    tiles = n_p // ROW_TILE
    lat_tiles = seq // ROW_TILE
    t_all = batch * n_p

    cond = jnp.zeros((8, d), F32).at[:batch].set(c).at[4].set(c_ctx)
    modr = _modulation(cond, ada_w, ada_b).reshape(depth * 8 * 6, 1, d)
    row_all = lambda i: jnp.where(i % tiles == 0, 4, i // tiles)
    cos_t, sin_t = _rope_tables(seq, ctx_len)
    rperm = np.array([(r % GROUP_SIZE) * GROUP_SIZE + r // GROUP_SIZE for r in range(N_EXPERTS)])
    xt = jnp.concatenate([ctx, x], axis=1).reshape(t_all, d)

    cols, blocks, gain_kind = _ab_layout()
    w0 = ab_w_in[0][:, cols].astype(BF16)
    gq = jnp.tile(a_q_norm[0][_DEINTERLEAVE], 2)
    gk = jnp.tile(a_k_norm[0][_DEINTERLEAVE], 2)
    ones = jnp.ones((LANES,), F32)
    head_gain = jnp.concatenate([{"q": gq, "k": gk, None: ones}[kind] for kind in gain_kind])[None, :]
    p0 = _project(xt, modr, 0, row_all, norm_g[0, 0][None, :], w0, cos_t, sin_t, head_gain, blocks, tiles)
    common = dict(batch=batch, seq=seq, ctx_len=ctx_len, n_qblocks=4, ctx_queries=True)
    oa = _attention(p0, [], mode="dense", q_col0=0, k_col=lambda j: 4 + j // 2, v_col=lambda j: 6 + j // 2,
                    **common)
    sink = jnp.broadcast_to(b_sink[0].astype(F32)[:, None], (8, LANES))
    ob = _attention(p0, [sink], mode="window", q_col0=8, k_col=lambda j: 12 + j // 2,
                    v_col=lambda j: 14 + j // 2, **common)
    w_out = ab_w_out[0].astype(BF16)
    x1, hp, lg = _out_project([oa, ob], [w_out[:512], w_out[512:]], xt, lambda i: i, modr, 0, row_all,
                              norm_g[0, 1][None, :], norm_g[0, 2][None, :], router_w[0].T[rperm],
                              t_all // ROW_TILE)
    x2 = _moe(x1, hp, lg, router_bias[0], exp_w1[0], exp_w3[0], exp_w2[0], sh_w1[0], sh_w3[0], sh_w2[0],
              norm_g[0, 3][None, :], modr, 0, lambda i: row_all(i // (ROW_TILE // MOVE_TILE)))

    base1 = 8 * 6
    lambda_init = 0.8 - 0.6 * math.exp(-0.3 * 1)
    cols1, blocks1 = _c_layout()
    w1p = c_w_in[0][:, cols1].astype(BF16)
    p1 = _project(x2, modr, base1, row_all, norm_g[1, 0][None, :], w1p, cos_t, sin_t,
                  jnp.ones((1, w1p.shape[1]), F32), blocks1, tiles)
    lam = jnp.zeros((8, LANES), F32).at[:4, :HEAD_DIM].set(c_lambda[0].astype(F32))
    oc = _attention(p1, [lam, c_subln_g[0][None, :]], mode="diff", batch=batch, seq=seq, ctx_len=ctx_len,
                    n_qblocks=8, q_col0=0, k_col=lambda j: 8 + j, v_col=lambda j: 16 + j, ctx_queries=False,
                    lambda_init=lambda_init)
    row_lat = lambda i: i // lat_tiles
    x3, hp1, lg1 = _out_project([oc], [c_w_out[0].astype(BF16)], x2,
                                lambda i: (i // lat_tiles) * tiles + 1 + i % lat_tiles, modr, base1, row_lat,
                                norm_g[1, 1][None, :], norm_g[1, 2][None, :], router_w[1].T[rperm],
                                batch * lat_tiles)
    out = _moe(x3, hp1, lg1, router_bias[1], exp_w1[1], exp_w3[1], exp_w2[1], sh_w1[1], sh_w3[1], sh_w2[1],
               norm_g[1, 3][None, :], modr, base1, lambda i: row_lat(i // (ROW_TILE // MOVE_TILE)))
    return out.reshape(batch, seq, d)
```

```python
import functools
import math

import numpy as np
import jax
import jax.numpy as jnp
from jax import lax
from jax.experimental import pallas as pl
from jax.experimental.pallas import tpu as pltpu

F32 = jnp.float32
BF16 = jnp.bfloat16
U32 = jnp.uint32
I32 = jnp.int32
HIGHEST = lax.Precision.HIGHEST

D_MODEL = 1024
HEAD_DIM = 64
LANES = 128
GRID_W = 64
ROPE_THETA = 10000.0
EPS = 1e-6
NEG_INF = -1e30
WINDOW = 128
N_EXPERTS = 64
TOP_K = 8
N_GROUPS = 8
TOPK_GROUPS = 4
GROUP_SIZE = N_EXPERTS // N_GROUPS
ROUTED_SCALE = 2.5
LOG2E = 1.4426950408889634
Q_SCALE = HEAD_DIM ** -0.5 * LOG2E

ROW_TILE = 256
Q_TILE = 256
KV_TILE = 256
ROUTE_TILE = 512
MOVE_TILE = 128
EXPERT_BLOCK = 256
HALF = D_MODEL // 2
VMEM_LIMIT = 48 * 1024 * 1024


def _cparams(sem):
    return pltpu.CompilerParams(dimension_semantics=sem, vmem_limit_bytes=VMEM_LIMIT)


def _rms(x, g):
    ms = jnp.mean(x * x, axis=-1, keepdims=True)
    return x * lax.rsqrt(ms + EPS) * g


def _pack_halves(h):
    lo = lax.bitcast_convert_type(h[:, :HALF].astype(BF16).astype(F32), U32) >> 16
    hi = lax.bitcast_convert_type(h[:, HALF:].astype(BF16).astype(F32), U32) & jnp.uint32(0xFFFF0000)
    return hi | lo


def _unpack_halves(u):
    lo = lax.bitcast_convert_type(u << 16, F32)
    hi = lax.bitcast_convert_type(u & jnp.uint32(0xFFFF0000), F32)
    return lo, hi


def _mod_kernel(c_ref, w_ref, b_ref, o_ref):
    c = c_ref[...]
    sc = c * jax.nn.sigmoid(c)
    o_ref[0] = jnp.dot(sc, w_ref[0], precision=HIGHEST, preferred_element_type=F32) + b_ref[0]


def _modulation(cond, ada_w, ada_b):
    depth, d, n = ada_w.shape
    nt = 1536
    return pl.pallas_call(
        _mod_kernel,
        out_shape=jax.ShapeDtypeStruct((depth, 8, n), F32),
        grid=(depth, n // nt),
        in_specs=[pl.BlockSpec((8, d), lambda l, j: (0, 0)),
                  pl.BlockSpec((1, d, nt), lambda l, j: (l, 0, j)),
                  pl.BlockSpec((1, 1, nt), lambda l, j: (l, 0, j))],
        out_specs=pl.BlockSpec((1, 8, nt), lambda l, j: (l, 0, j)),
        compiler_params=_cparams(("arbitrary", "arbitrary")),
        name="ada_mod",
    )(cond, ada_w, ada_b.reshape(depth, 1, n))


def _proj_kernel(x_ref, sh_ref, sc_ref, g_ref, w_ref, cos_ref, sin_ref, hg_ref, gm_ref, o_ref, *, blocks):
    h = _rms(x_ref[...], g_ref[...]) * (1.0 + sc_ref[...]) + sh_ref[...]
    y = jnp.dot(h.astype(BF16), w_ref[...], preferred_element_type=F32)
    lane = lax.broadcasted_iota(I32, (x_ref.shape[0], LANES), 1)
    first_half = (lane % HEAD_DIM) < (HEAD_DIM // 2)
    for jb, (norm, rope, scale) in enumerate(blocks):
        cols = slice(jb * LANES, (jb + 1) * LANES)
        yb = y[:, cols]
        if norm:
            ms = jnp.dot(yb * yb, gm_ref[...], precision=HIGHEST, preferred_element_type=F32)
            yb = yb * lax.rsqrt(ms + EPS) * hg_ref[:, cols]
        if rope:
            swapped = jnp.where(first_half, pltpu.roll(yb, LANES - HEAD_DIM // 2, 1),
                                pltpu.roll(yb, HEAD_DIM // 2, 1))
            yb = yb * cos_ref[...] + swapped * sin_ref[...]
        if scale != 1.0:
            yb = yb * scale
        o_ref[:, cols] = yb.astype(BF16)


def _project(x, modr, mod_base, row_of_tile, g, w, cos_t, sin_t, head_gain, blocks, tiles_per_batch):
    t, d = x.shape
    n = w.shape[1]
    group_mean = jnp.asarray(np.kron(np.eye(LANES // HEAD_DIM), np.full((HEAD_DIM, HEAD_DIM), 1.0 / HEAD_DIM)), F32)
    return pl.pallas_call(
        functools.partial(_proj_kernel, blocks=blocks),
        out_shape=jax.ShapeDtypeStruct((t, n), BF16),
        grid=(t // ROW_TILE,),
        in_specs=[pl.BlockSpec((ROW_TILE, d), lambda i: (i, 0)),
                  pl.BlockSpec((None, 1, d), lambda i: (mod_base + row_of_tile(i) * 6 + 0, 0, 0)),
                  pl.BlockSpec((None, 1, d), lambda i: (mod_base + row_of_tile(i) * 6 + 1, 0, 0)),
                  pl.BlockSpec((1, d), lambda i: (0, 0)),
                  pl.BlockSpec((d, n), lambda i: (0, 0)),
                  pl.BlockSpec((ROW_TILE, LANES), lambda i: (i % tiles_per_batch, 0)),
                  pl.BlockSpec((ROW_TILE, LANES), lambda i: (i % tiles_per_batch, 0)),
                  pl.BlockSpec((1, n), lambda i: (0, 0)),
                  pl.BlockSpec((LANES, LANES), lambda i: (0, 0))],
        out_specs=pl.BlockSpec((ROW_TILE, n), lambda i: (i, 0)),
        compiler_params=_cparams(("arbitrary",)),
        name="prenorm_proj",
    )(x, modr, modr, g, w, cos_t, sin_t, head_gain, group_mean)


def _attn_kernel(*refs, mode, n_kv, seq, ctx_len, ctx_queries, lambda_init):
    if mode == "window":
        q_ref, k_ref, v_ref, sink_ref, o_ref, m_ref, l_ref, acc_ref = refs
    elif mode == "diff":
        q_ref, k_ref, v_ref, lam_ref, sg_ref, o_ref, m_ref, l_ref, acc_ref = refs
    else:
        q_ref, k_ref, v_ref, o_ref, m_ref, l_ref, acc_ref = refs
    tq = q_ref.shape[0]
    j = pl.program_id(1)
    qi = pl.program_id(2)
    is_ctx_q = (qi == 0) if ctx_queries else False

    q = q_ref[...]
    lane = lax.broadcasted_iota(I32, (tq, LANES), 1)
    low = lane < HEAD_DIM
    zero = jnp.zeros_like(q)
    q2 = jnp.concatenate([jnp.where(low, q, zero), jnp.where(low, zero, q)], axis=0)

    if mode == "window":
        s0 = sink_ref[pl.ds(2 * j, 1), :]
        s1 = sink_ref[pl.ds(2 * j + 1, 1), :]
        row = lax.broadcasted_iota(I32, (2 * tq, LANES), 0)
        sink = jnp.where(row < tq, s0, s1) * LOG2E
        m0 = sink
    else:
        m0 = jnp.full((2 * tq, LANES), NEG_INF, F32)
    state = (m0, jnp.zeros((2 * tq, LANES), F32), jnp.zeros((2 * tq, LANES), F32))

    def chunk(state, start, valid):
        m_prev, l_prev, acc_prev = state
        k = k_ref[pl.ds(start, KV_TILE), :]
        v = v_ref[pl.ds(start, KV_TILE), :]
        s = lax.dot_general(q2, k, (((1,), (1,)), ((), ())), preferred_element_type=F32)
        if valid is not None:
            s = jnp.where(valid, s, NEG_INF)
        m_new = jnp.maximum(m_prev, jnp.max(s, axis=1, keepdims=True))
        alpha = jnp.exp2(m_prev - m_new)
        p = jnp.exp2(s - jnp.concatenate([m_new] * (KV_TILE // LANES), axis=1))
        part = p[:, :LANES]
        for c in range(1, KV_TILE // LANES):
            part = part + p[:, c * LANES:(c + 1) * LANES]
        return (m_new, alpha * l_prev + part,
                alpha * acc_prev + jnp.dot(p.astype(BF16), v, preferred_element_type=F32))

    def save(state):
        m_ref[...], l_ref[...], acc_ref[...] = state

    state = chunk(state, 0, None)
    if mode == "window":
        save(state)

        @pl.when(jnp.logical_not(is_ctx_q))
        def _():
            st = (m_ref[...], l_ref[...], acc_ref[...])
            q0 = (qi - (1 if ctx_queries else 0)) * tq
            kstart = jnp.clip(q0 - WINDOW, 0, seq - 2 * KV_TILE)
            r = lax.broadcasted_iota(I32, (2 * tq, KV_TILE), 0)
            qpos = q0 + jnp.where(r >= tq, r - tq, r)
            col = lax.broadcasted_iota(I32, (2 * tq, KV_TILE), 1)
            for w in range(2):
                kpos = kstart + w * KV_TILE + col
                st = chunk(st, pl.multiple_of(ctx_len + kstart + w * KV_TILE, WINDOW),
                           jnp.abs(qpos - kpos) <= WINDOW)
            save(st)
    elif ctx_queries:
        save(state)

        @pl.when(jnp.logical_not(is_ctx_q))
        def _():
            st = (m_ref[...], l_ref[...], acc_ref[...])
            for c in range(ctx_len // KV_TILE, n_kv):
                st = chunk(st, c * KV_TILE, None)
            save(st)
    else:
        for c in range(ctx_len // KV_TILE, n_kv):
            state = chunk(state, c * KV_TILE, None)
        save(state)

    l = jnp.sum(l_ref[...], axis=1, keepdims=True)
    if mode == "window":
        l = l + jnp.exp2(sink - m_ref[...])[:, :1]
    o2 = acc_ref[...] / l
    if mode == "diff":
        lp = lam_ref[...]
        lam = (jnp.exp(jnp.sum(lp[0:1] * lp[1:2], axis=1, keepdims=True))
               - jnp.exp(jnp.sum(lp[2:3] * lp[3:4], axis=1, keepdims=True)) + lambda_init)
        o = o2[:tq] - lam * o2[tq:]
        o = _rms(o, sg_ref[...]) * (1.0 - lambda_init)
    else:
        o = jnp.where(low, o2[:tq], o2[tq:])
    o_ref[...] = o.astype(BF16)


def _attention(p, extra, *, mode, batch, seq, ctx_len, n_qblocks, q_col0, k_col, v_col, ctx_queries,
               lambda_init=0.0):
    n_p = ctx_len + seq
    tiles = n_p // Q_TILE
    qt = tiles if ctx_queries else seq // Q_TILE
    q_off = 0 if ctx_queries else ctx_len // Q_TILE
    in_specs = [pl.BlockSpec((Q_TILE, LANES), lambda b, j, qi: (b * tiles + q_off + qi, q_col0 + j)),
                pl.BlockSpec((n_p, LANES), lambda b, j, qi: (b, k_col(j))),
                pl.BlockSpec((n_p, LANES), lambda b, j, qi: (b, v_col(j)))]
    args = [p, p, p]
    for e in extra:
        in_specs.append(pl.BlockSpec(e.shape, lambda b, j, qi: (0, 0)))
        args.append(e)
    kern = functools.partial(_attn_kernel, mode=mode, n_kv=n_p // KV_TILE, seq=seq, ctx_len=ctx_len,
                             ctx_queries=ctx_queries, lambda_init=lambda_init)
    return pl.pallas_call(
        kern,
        out_shape=jax.ShapeDtypeStruct((batch * qt * Q_TILE, n_qblocks * LANES), BF16),
        grid=(batch, n_qblocks, qt),
        in_specs=in_specs,
        out_specs=pl.BlockSpec((Q_TILE, LANES), lambda b, j, qi: (b * qt + qi, j)),
        scratch_shapes=[pltpu.VMEM((2 * Q_TILE, LANES), F32)] * 3,
        compiler_params=_cparams(("arbitrary", "arbitrary", "arbitrary")),
        name="attn_" + mode,
    )(*args)


def _out_kernel(*refs, n_o):
    o_refs = refs[:n_o]
    w_refs = refs[n_o:2 * n_o]
    x_ref, g1_ref, gate_ref, g2_ref, sh_ref, sc_ref, rw_ref, xn_ref, hp_ref, lg_ref = refs[2 * n_o:]
    y = jnp.dot(o_refs[0][...], w_refs[0][...], preferred_element_type=F32)
    for a in range(1, n_o):
        y = y + jnp.dot(o_refs[a][...], w_refs[a][...], preferred_element_type=F32)
    xn = x_ref[...] + gate_ref[...] * _rms(y, g1_ref[...])
    xn_ref[...] = xn
    h = _rms(xn, g2_ref[...]) * (1.0 + sc_ref[...]) + sh_ref[...]
    hp_ref[...] = _pack_halves(h)
    lg_ref[...] = lax.dot_general(rw_ref[...], h, (((1,), (1,)), ((), ())), precision=HIGHEST,
                                  preferred_element_type=F32)


def _out_project(os_, ws, x, x_tile, modr, mod_base, row_of_tile, g1, g2, rw_t, n_tiles):
    d = x.shape[1]
    n_o = len(os_)
    tm = n_tiles * ROW_TILE
    mspec = lambda which: pl.BlockSpec((None, 1, d), lambda i: (mod_base + row_of_tile(i) * 6 + which, 0, 0))
    in_specs = ([pl.BlockSpec((ROW_TILE, o.shape[1]), lambda i: (i, 0)) for o in os_]
                + [pl.BlockSpec(w.shape, lambda i: (0, 0)) for w in ws]
                + [pl.BlockSpec((ROW_TILE, d), lambda i: (x_tile(i), 0)),
                   pl.BlockSpec((1, d), lambda i: (0, 0)), mspec(2),
                   pl.BlockSpec((1, d), lambda i: (0, 0)), mspec(3), mspec(4),
                   pl.BlockSpec(rw_t.shape, lambda i: (0, 0))])
    return pl.pallas_call(
        functools.partial(_out_kernel, n_o=n_o),
        out_shape=(jax.ShapeDtypeStruct((tm, d), F32), jax.ShapeDtypeStruct((tm, HALF), U32),
                   jax.ShapeDtypeStruct((N_EXPERTS, tm), F32)),
        grid=(n_tiles,),
        in_specs=in_specs,
        out_specs=(pl.BlockSpec((ROW_TILE, d), lambda i: (i, 0)),
                   pl.BlockSpec((ROW_TILE, HALF), lambda i: (i, 0)),
                   pl.BlockSpec((N_EXPERTS, ROW_TILE), lambda i: (0, i))),
        compiler_params=_cparams(("arbitrary",)),
        name="out_proj",
    )(*os_, *ws, x, g1, modr, g2, modr, modr, rw_t)


def _route_kernel(lg_ref, bias_ref, tri_ref, idx_ref, w_ref, pos_ref, cnt_ref, carry_ref):
    i = pl.program_id(0)

    @pl.when(i == 0)
    def _():
        carry_ref[...] = jnp.zeros(carry_ref.shape, F32)

    tr = lg_ref.shape[1]
    score = jax.nn.sigmoid(lg_ref[...])
    sel = score + bias_ref[...][:, :1]
    sel_j = [sel[j * GROUP_SIZE:(j + 1) * GROUP_SIZE] for j in range(GROUP_SIZE)]
    sc_j = [score[j * GROUP_SIZE:(j + 1) * GROUP_SIZE] for j in range(GROUP_SIZE)]
    gi = lax.broadcasted_iota(I32, (N_GROUPS, tr), 0)

    m1 = sel_j[0]
    m2 = jnp.full_like(m1, -jnp.inf)
    for j in range(1, GROUP_SIZE):
        m2 = jnp.maximum(m2, jnp.minimum(m1, sel_j[j]))
        m1 = jnp.maximum(m1, sel_j[j])
    gs = m1 + m2

    grank = jnp.zeros((N_GROUPS, tr), I32)
    for gp in range(N_GROUPS):
        rowv = gs[gp:gp + 1, :]
        grank = grank + jnp.where(gi > gp, jnp.where(rowv >= gs, 1, 0), jnp.where(rowv > gs, 1, 0))
    gmask = grank < TOPK_GROUPS
    val_j = [jnp.where(gmask, sel_j[j], NEG_INF) for j in range(GROUP_SIZE)]

    rank_j = [jnp.zeros((N_GROUPS, tr), I32) for _ in range(GROUP_SIZE)]
    for gp in range(N_GROUPS):
        after = gi > gp
        not_before = gi >= gp
        for jp in range(GROUP_SIZE):
            rowv = val_j[jp][gp:gp + 1, :]
            for j in range(GROUP_SIZE):
                tie = after if jp >= j else not_before
                rank_j[j] = rank_j[j] + jnp.where(tie, jnp.where(rowv >= val_j[j], 1, 0),
                                                  jnp.where(rowv > val_j[j], 1, 0))

    chosen = jnp.concatenate([jnp.where(rank_j[j] < TOP_K, 1.0, 0.0) for j in range(GROUP_SIZE)], axis=0)
    before = jnp.dot(chosen.astype(BF16), tri_ref[...], preferred_element_type=F32) + carry_ref[...][:, :1]
    carry_ref[...] = carry_ref[...] + jnp.sum(chosen, axis=1, keepdims=True)
    cnt_ref[...] = carry_ref[...].astype(I32)

    idx_rows, w_rows, pos_rows = [], [], []
    for k in range(TOP_K):
        e_acc = jnp.zeros((N_GROUPS, tr), I32)
        w_acc = jnp.zeros((N_GROUPS, tr), F32)
        p_acc = jnp.zeros((N_GROUPS, tr), F32)
        for j in range(GROUP_SIZE):
            hit = rank_j[j] == k
            e_acc = e_acc + jnp.where(hit, gi * GROUP_SIZE + j, 0)
            w_acc = w_acc + jnp.where(hit, sc_j[j], 0.0)
            p_acc = p_acc + jnp.where(hit, before[j * GROUP_SIZE:(j + 1) * GROUP_SIZE], 0.0)
        idx_rows.append(jnp.sum(e_acc, axis=0, keepdims=True))
        w_rows.append(jnp.sum(w_acc, axis=0, keepdims=True))
        pos_rows.append(jnp.sum(p_acc, axis=0, keepdims=True))
    w_all = jnp.concatenate(w_rows, axis=0)
    idx_ref[...] = jnp.concatenate(idx_rows, axis=0)
    w_ref[...] = w_all / jnp.sum(w_all, axis=0, keepdims=True) * ROUTED_SCALE
    pos_ref[...] = jnp.concatenate(pos_rows, axis=0).astype(I32)


def _route(lg_t, bias_rep):
    tm = lg_t.shape[1]
    tri = jnp.asarray(np.triu(np.ones((ROUTE_TILE, ROUTE_TILE), np.float32), 1), BF16)
    tok = lambda i: (0, i)
    return pl.pallas_call(
        _route_kernel,
        out_shape=(jax.ShapeDtypeStruct((TOP_K, tm), I32), jax.ShapeDtypeStruct((TOP_K, tm), F32),
                   jax.ShapeDtypeStruct((TOP_K, tm), I32), jax.ShapeDtypeStruct((N_EXPERTS, LANES), I32)),
        grid=(tm // ROUTE_TILE,),
        in_specs=[pl.BlockSpec((N_EXPERTS, ROUTE_TILE), tok),
                  pl.BlockSpec((N_EXPERTS, LANES), lambda i: (0, 0)),
                  pl.BlockSpec((ROUTE_TILE, ROUTE_TILE), lambda i: (0, 0))],
        out_specs=(pl.BlockSpec((TOP_K, ROUTE_TILE), tok), pl.BlockSpec((TOP_K, ROUTE_TILE), tok),
                   pl.BlockSpec((TOP_K, ROUTE_TILE), tok), pl.BlockSpec((N_EXPERTS, LANES), lambda i: (0, 0))),
        scratch_shapes=[pltpu.VMEM((N_EXPERTS, LANES), F32)],
        compiler_params=_cparams(("arbitrary",)),
        name="route",
    )(lg_t, bias_rep, tri)


def _row_copy(src, s, dst, d, sem):
    return pltpu.make_async_copy(src.at[pl.ds(s, 1), :], dst.at[pl.ds(d, 1), :], sem)


def _dispatch_kernel(zstart_ref, zlen_ref, dest_hbm, hp_ref, xs_hbm, dest_smem, zrow_ref, sem, dsem):
    i = pl.program_id(0)
    n_idx = MOVE_TILE * TOP_K
    cp = pltpu.make_async_copy(dest_hbm.at[pl.ds(i * n_idx, n_idx)], dest_smem, dsem)
    cp.start()
    cp.wait()

    def issue(t, carry):
        for k in range(TOP_K):
            _row_copy(hp_ref, t, xs_hbm, dest_smem[t * TOP_K + k], sem).start()
        return carry

    lax.fori_loop(0, MOVE_TILE, issue, 0)

    def drain(t, carry):
        _row_copy(hp_ref, 0, xs_hbm, 0, sem).wait()
        return carry

    lax.fori_loop(0, n_idx, drain, 0)

    @pl.when(i == 0)
    def _():
        zrow_ref[...] = jnp.zeros(zrow_ref.shape, U32)

        def per_expert(e, carry):
            z0 = zstart_ref[e]

            def fill(r, c):
                _row_copy(zrow_ref, 0, xs_hbm, z0 + r, sem).start()
                return c

            lax.fori_loop(0, zlen_ref[e], fill, 0)

            def fill_wait(r, c):
                _row_copy(zrow_ref, 0, xs_hbm, 0, sem).wait()
                return c

            lax.fori_loop(0, zlen_ref[e], fill_wait, 0)
            return carry

        lax.fori_loop(0, N_EXPERTS, per_expert, 0)


def _dispatch(zstart, zlen, dest, hp, n_slots):
    tm = hp.shape[0]
    return pl.pallas_call(
        _dispatch_kernel,
        out_shape=jax.ShapeDtypeStruct((n_slots, HALF), U32),
        grid_spec=pltpu.PrefetchScalarGridSpec(
            num_scalar_prefetch=2,
            grid=(tm // MOVE_TILE,),
            in_specs=[pl.BlockSpec(memory_space=pl.ANY),
                      pl.BlockSpec((MOVE_TILE, HALF), lambda i, zs, zl: (i, 0))],
            out_specs=pl.BlockSpec(memory_space=pl.ANY),
            scratch_shapes=[pltpu.SMEM((MOVE_TILE * TOP_K,), I32), pltpu.VMEM((8, HALF), U32),
                            pltpu.SemaphoreType.DMA, pltpu.SemaphoreType.DMA]),
        compiler_params=_cparams(("arbitrary",)),
        name="dispatch",
    )(zstart, zlen, dest, hp)


def _expert_kernel(be_ref, nu_ref, xs_ref, w1_ref, w3_ref, w2_ref, ys_ref, w1b_ref, w3b_ref, w2b_ref):
    i = pl.program_id(0)

    @pl.when(jnp.logical_or(i == 0, be_ref[i] != be_ref[jnp.maximum(i - 1, 0)]))
    def _():
        w1b_ref[...] = w1_ref[0].astype(BF16)
        w3b_ref[...] = w3_ref[0].astype(BF16)
        w2b_ref[...] = w2_ref[0].astype(BF16)

    @pl.when(i < nu_ref[0])
    def _():
        lo, hi = _unpack_halves(xs_ref[...])
        lo = lo.astype(BF16)
        hi = hi.astype(BF16)
        h1 = (jnp.dot(lo, w1b_ref[:HALF, :], preferred_element_type=F32)
              + jnp.dot(hi, w1b_ref[HALF:, :], preferred_element_type=F32))
        h3 = (jnp.dot(lo, w3b_ref[:HALF, :], preferred_element_type=F32)
              + jnp.dot(hi, w3b_ref[HALF:, :], preferred_element_type=F32))
        g = (h1 * jax.nn.sigmoid(h1) * h3).astype(BF16)
        y = jnp.dot(g, w2b_ref[...], preferred_element_type=F32)
        ys_ref[...] = _pack_halves(y)


def _experts(block_e, n_used, xs, w1, w3, w2):
    n_slots = xs.shape[0]
    n_blocks = n_slots // EXPERT_BLOCK
    d, eh = w1.shape[1], w1.shape[2]
    rows = lambda i, be, nu: (jnp.minimum(i, nu[0] - 1), 0)
    return pl.pallas_call(
        _expert_kernel,
        out_shape=jax.ShapeDtypeStruct((n_slots, HALF), U32),
        grid_spec=pltpu.PrefetchScalarGridSpec(
            num_scalar_prefetch=2,
            grid=(n_blocks,),
            in_specs=[pl.BlockSpec((EXPERT_BLOCK, HALF), rows),
                      pl.BlockSpec((1, d, eh), lambda i, be, nu: (be[i], 0, 0)),
                      pl.BlockSpec((1, d, eh), lambda i, be, nu: (be[i], 0, 0)),
                      pl.BlockSpec((1, eh, d), lambda i, be, nu: (be[i], 0, 0))],
            out_specs=pl.BlockSpec((EXPERT_BLOCK, HALF), rows),
            scratch_shapes=[pltpu.VMEM((d, eh), BF16), pltpu.VMEM((d, eh), BF16), pltpu.VMEM((eh, d), BF16)]),
        compiler_params=_cparams(("arbitrary",)),
        name="experts",
    )(block_e, n_used, xs, w1, w3, w2)


def _combine_kernel(dest_hbm, ys_hbm, wt_ref, hp_ref, x_ref, s1_ref, s3_ref, s2_ref, g_ref, gate_ref, o_ref,
                    dest_smem, buf_ref, sem, dsem):
    i = pl.program_id(0)
    n_idx = MOVE_TILE * TOP_K
    cp = pltpu.make_async_copy(dest_hbm.at[pl.ds(i * n_idx, n_idx)], dest_smem, dsem)
    cp.start()
    cp.wait()

    def issue(t, carry):
        for k in range(TOP_K):
            _row_copy(ys_hbm, dest_smem[t * TOP_K + k], buf_ref.at[k], t, sem).start()
        return carry

    lax.fori_loop(0, MOVE_TILE, issue, 0)

    lo, hi = _unpack_halves(hp_ref[...])
    lo = lo.astype(BF16)
    hi = hi.astype(BF16)
    h1 = (jnp.dot(lo, s1_ref[:HALF, :], preferred_element_type=F32)
          + jnp.dot(hi, s1_ref[HALF:, :], preferred_element_type=F32))
    h3 = (jnp.dot(lo, s3_ref[:HALF, :], preferred_element_type=F32)
          + jnp.dot(hi, s3_ref[HALF:, :], preferred_element_type=F32))
    y = jnp.dot((h1 * jax.nn.sigmoid(h1) * h3).astype(BF16), s2_ref[...], preferred_element_type=F32)

    def drain(t, carry):
        _row_copy(ys_hbm, 0, buf_ref.at[0], 0, sem).wait()
        return carry

    lax.fori_loop(0, n_idx, drain, 0)

    wt = wt_ref[...]
    r_lo = jnp.zeros((MOVE_TILE, HALF), F32)
    r_hi = jnp.zeros((MOVE_TILE, HALF), F32)
    for k in range(TOP_K):
        a, b = _unpack_halves(buf_ref[k])
        wk = wt[:, k:k + 1]
        r_lo = r_lo + wk * a
        r_hi = r_hi + wk * b
    y = y + jnp.concatenate([r_lo, r_hi], axis=1)
    o_ref[...] = x_ref[...] + gate_ref[...] * _rms(y, g_ref[...])


def _combine(dest, ys, wt, hp, x, s1, s3, s2, g, modr, mod_base, row_of_tile):
    tm, d = x.shape
    full = lambda a: pl.BlockSpec(a.shape, lambda i: (0, 0))
    return pl.pallas_call(
        _combine_kernel,
        out_shape=jax.ShapeDtypeStruct((tm, d), F32),
        grid=(tm // MOVE_TILE,),
        in_specs=[pl.BlockSpec(memory_space=pl.ANY), pl.BlockSpec(memory_space=pl.ANY),
                  pl.BlockSpec((MOVE_TILE, TOP_K), lambda i: (i, 0)),
                  pl.BlockSpec((MOVE_TILE, HALF), lambda i: (i, 0)),
                  pl.BlockSpec((MOVE_TILE, d), lambda i: (i, 0)),
                  full(s1), full(s3), full(s2), full(g),
                  pl.BlockSpec((None, 1, d), lambda i: (mod_base + row_of_tile(i) * 6 + 5, 0, 0))],
        out_specs=pl.BlockSpec((MOVE_TILE, d), lambda i: (i, 0)),
        scratch_shapes=[pltpu.SMEM((MOVE_TILE * TOP_K,), I32), pltpu.VMEM((TOP_K, MOVE_TILE, HALF), U32),
                        pltpu.SemaphoreType.DMA, pltpu.SemaphoreType.DMA],
        compiler_params=_cparams(("arbitrary",)),
        name="combine",
    )(dest, ys, wt, hp, x, s1, s3, s2, g, modr)


def _moe(xn, hp, lg_t, bias, w1, w3, w2, s1, s3, s2, g, modr, mod_base, row_of_tile):
    tm = xn.shape[0]
    perm = np.array([(r % GROUP_SIZE) * GROUP_SIZE + r // GROUP_SIZE for r in range(N_EXPERTS)])
    bias_rep = jnp.broadcast_to(bias.astype(F32)[perm][:, None], (N_EXPERTS, LANES))
    idx, w, pos, cnt = _route(lg_t, bias_rep)

    counts = cnt[:, 0].reshape(GROUP_SIZE, N_GROUPS).T.reshape(N_EXPERTS)
    padded = (counts + EXPERT_BLOCK - 1) // EXPERT_BLOCK * EXPERT_BLOCK
    pad_end = jnp.cumsum(padded)
    pad_start = pad_end - padded
    n_blocks = tm * TOP_K // EXPERT_BLOCK + N_EXPERTS
    n_slots = n_blocks * EXPERT_BLOCK
    eids = jnp.arange(N_EXPERTS, dtype=I32)[:, None, None]
    dest = (pos + jnp.sum(jnp.where(idx[None] == eids, pad_start[:, None, None], 0), axis=0)).T.reshape(-1)
    dest = dest.astype(I32)
    starts = jnp.arange(n_blocks, dtype=I32) * EXPERT_BLOCK
    block_e = jnp.minimum(jnp.sum((pad_end[None, :] <= starts[:, None]).astype(I32), axis=1), N_EXPERTS - 1)
    n_used = (pad_end[-1:] // EXPERT_BLOCK).astype(I32)

    xs = _dispatch((pad_start + counts).astype(I32), (padded - counts).astype(I32), dest, hp, n_slots)
    ys = _experts(block_e, n_used, xs, w1, w3, w2)
    return _combine(dest, ys, w.T, hp, xn, s1.astype(BF16), s3.astype(BF16), s2.astype(BF16), g, modr, mod_base,
                    row_of_tile)


_DEINTERLEAVE = np.concatenate([np.arange(0, HEAD_DIM, 2), np.arange(1, HEAD_DIM, 2)])


def _rope_tables(seq, ctx_len):
    t = np.arange(seq)
    n_pair = HEAD_DIM // 4
    inv = jnp.asarray(ROPE_THETA, F32) ** (-jnp.arange(n_pair, dtype=F32) / n_pair)
    r = jnp.asarray(t // GRID_W, F32)
    c = jnp.asarray(t % GRID_W, F32)
    ang = jnp.concatenate([r[:, None] * inv, c[:, None] * inv], axis=-1)
    cos, sin = jnp.cos(ang), jnp.sin(ang)
    cos_t = jnp.tile(jnp.concatenate([cos, cos], axis=-1), (1, LANES // HEAD_DIM))
    sin_t = jnp.tile(jnp.concatenate([-sin, sin], axis=-1), (1, LANES // HEAD_DIM))
    cos_t = jnp.concatenate([jnp.ones((ctx_len, LANES), F32), cos_t], axis=0)
    sin_t = jnp.concatenate([jnp.zeros((ctx_len, LANES), F32), sin_t], axis=0)
    return cos_t, sin_t


def _ab_layout():
    cols, blocks, gain_kind = [], [], []
    for base_q, base_k, base_v, normed in ((0, 512, 640, True), (768, 1280, 1408, False)):
        for jb in range(4):
            cols += [base_q + h * HEAD_DIM + _DEINTERLEAVE for h in (2 * jb, 2 * jb + 1)]
            blocks.append((normed, True, Q_SCALE))
            gain_kind.append("q" if normed else None)
        for kvh in range(2):
            cols += [base_k + kvh * HEAD_DIM + _DEINTERLEAVE] * 2
            blocks.append((normed, True, 1.0))
            gain_kind.append("k" if normed else None)
        for kvh in range(2):
            cols += [base_v + kvh * HEAD_DIM + np.arange(HEAD_DIM)] * 2
            blocks.append((False, False, 1.0))
            gain_kind.append(None)
    return np.concatenate(cols), tuple(blocks), gain_kind


def _c_layout():
    cols, blocks = [], []
    for base, rope, scale in ((0, True, Q_SCALE), (1024, True, 1.0)):
        for h in range(8):
            cols += [base + (2 * h + m) * HEAD_DIM + _DEINTERLEAVE for m in range(2)]
            blocks.append((False, rope, scale))
    for h in range(8):
        cols.append(2048 + h * LANES + np.arange(LANES))
        blocks.append((False, False, 1.0))
    return np.concatenate(cols), tuple(blocks)


def kernel(x, c, ctx, c_ctx, ada_w, ada_b, norm_g, ab_w_in, ab_w_out, a_q_norm, a_k_norm, b_sink,
           c_w_in, c_w_out, c_lambda, c_subln_g, router_w, router_bias, exp_w1, exp_w3, exp_w2,
           sh_w1, sh_w3, sh_w2):
    batch, seq, d = x.shape
    ctx_len = ctx.shape[1]
    depth = ada_w.shape[0]
    assert d == D_MODEL and ctx_len == ROW_TILE and depth == 2
    assert seq % Q_TILE == 0 and seq >= 2 * KV_TILE and batch <= 4
    n_p = ctx_len + seq
    tiles = n_p // ROW_TILE
    lat_tiles = seq // ROW_TILE
    t_all = batch * n_p

    cond = jnp.zeros((8, d), F32).at[:batch].set(c).at[4].set(c_ctx)
    modr = _modulation(cond, ada_w, ada_b).reshape(depth * 8 * 6, 1, d)
    row_all = lambda i: jnp.where(i % tiles == 0, 4, i // tiles)
    cos_t, sin_t = _rope_tables(seq, ctx_len)
    rperm = np.array([(r % GROUP_SIZE) * GROUP_SIZE + r // GROUP_SIZE for r in range(N_EXPERTS)])
    xt = jnp.concatenate([ctx, x], axis=1).reshape(t_all, d)

    cols, blocks, gain_kind = _ab_layout()
    w0 = ab_w_in[0][:, cols].astype(BF16)
    gq = jnp.tile(a_q_norm[0][_DEINTERLEAVE], 2)
    gk = jnp.tile(a_k_norm[0][_DEINTERLEAVE], 2)
    ones = jnp.ones((LANES,), F32)
    head_gain = jnp.concatenate([{"q": gq, "k": gk, None: ones}[kind] for kind in gain_kind])[None, :]
    p0 = _project(xt, modr, 0, row_all, norm_g[0, 0][None, :], w0, cos_t, sin_t, head_gain, blocks, tiles)
    common = dict(batch=batch, seq=seq, ctx_len=ctx_len, n_qblocks=4, ctx_queries=True)
    oa = _attention(p0, [], mode="dense", q_col0=0, k_col=lambda j: 4 + j // 2, v_col=lambda j: 6 + j // 2,
                    **common)
    sink = jnp.broadcast_to(b_sink[0].astype(F32)[:, None], (8, LANES))
    ob = _attention(p0, [sink], mode="window", q_col0=8, k_col=lambda j: 12 + j // 2,
                    v_col=lambda j: 14 + j // 2, **common)
    w_out = ab_w_out[0].astype(BF16)
    x1, hp, lg = _out_project([oa, ob], [w_out[:512], w_out[512:]], xt, lambda i: i, modr, 0, row_all,
                              norm_g[0, 1][None, :], norm_g[0, 2][None, :], router_w[0].T[rperm],
                              t_all // ROW_TILE)
    x2 = _moe(x1, hp, lg, router_bias[0], exp_w1[0], exp_w3[0], exp_w2[0], sh_w1[0], sh_w3[0], sh_w2[0],
              norm_g[0, 3][None, :], modr, 0, lambda i: row_all(i // (ROW_TILE // MOVE_TILE)))

    base1 = 8 * 6
    lambda_init = 0.8 - 0.6 * math.exp(-0.3 * 1)
    cols1, blocks1 = _c_layout()
    w1p = c_w_in[0][:, cols1].astype(BF16)
    p1 = _project(x2, modr, base1, row_all, norm_g[1, 0][None, :], w1p, cos_t, sin_t,
                  jnp.ones((1, w1p.shape[1]), F32), blocks1, tiles)
    lam = jnp.zeros((8, LANES), F32).at[:4, :HEAD_DIM].set(c_lambda[0].astype(F32))
    oc = _attention(p1, [lam, c_subln_g[0][None, :]], mode="diff", batch=batch, seq=seq, ctx_len=ctx_len,
                    n_qblocks=8, q_col0=0, k_col=lambda j: 8 + j, v_col=lambda j: 16 + j, ctx_queries=False,
                    lambda_init=lambda_init)
    row_lat = lambda i: i // lat_tiles
    x3, hp1, lg1 = _out_project([oc], [c_w_out[0].astype(BF16)], x2,
                                lambda i: (i // lat_tiles) * tiles + 1 + i % lat_tiles, modr, base1, row_lat,
                                norm_g[1, 1][None, :], norm_g[1, 2][None, :], router_w[1].T[rperm],
                                batch * lat_tiles)
    out = _moe(x3, hp1, lg1, router_bias[1], exp_w1[1], exp_w3[1], exp_w2[1], sh_w1[1], sh_w3[1], sh_w2[1],
               norm_g[1, 3][None, :], modr, base1, lambda i: row_lat(i // (ROW_TILE // MOVE_TILE)))
    return out.reshape(batch, seq, d)
```

```python
import functools
import math

import numpy as np
import jax
import jax.numpy as jnp
from jax import lax
from jax.experimental import pallas as pl
from jax.experimental.pallas import tpu as pltpu

F32 = jnp.float32
BF16 = jnp.bfloat16
U32 = jnp.uint32
I32 = jnp.int32
HIGHEST = lax.Precision.HIGHEST

D_MODEL = 1024
HEAD_DIM = 64
LANES = 128
GRID_W = 64
ROPE_THETA = 10000.0
EPS = 1e-6
NEG_INF = -1e30
WINDOW = 128
N_EXPERTS = 64
TOP_K = 8
N_GROUPS = 8
TOPK_GROUPS = 4
GROUP_SIZE = N_EXPERTS // N_GROUPS
ROUTED_SCALE = 2.5
LOG2E = 1.4426950408889634
Q_SCALE = HEAD_DIM ** -0.5 * LOG2E

ROW_TILE = 256
Q_TILE = 256
KV_TILE = 256
MOVE_TILE = 256
EXPERT_BLOCK = 256
SEG_ALIGN = 8
RUN_PIECES = (256, 128, 64, 32, 16, 8)
STAGE_ROWS = 2560
STAGE_CHUNK = 512
HALF = D_MODEL // 2
VMEM_LIMIT = 48 * 1024 * 1024


def _cparams(sem):
    return pltpu.CompilerParams(dimension_semantics=sem, vmem_limit_bytes=VMEM_LIMIT)


def _rms(x, g):
    ms = jnp.mean(x * x, axis=-1, keepdims=True)
    return x * lax.rsqrt(ms + EPS) * g


def _pack_halves(h):
    lo = lax.bitcast_convert_type(h[:, :HALF].astype(BF16).astype(F32), U32) >> 16
    hi = lax.bitcast_convert_type(h[:, HALF:].astype(BF16).astype(F32), U32) & jnp.uint32(0xFFFF0000)
    return hi | lo


def _unpack_halves(u):
    lo = lax.bitcast_convert_type(u << 16, F32)
    hi = lax.bitcast_convert_type(u & jnp.uint32(0xFFFF0000), F32)
    return lo, hi


def _mod_kernel(c_ref, w_ref, b_ref, o_ref):
    c = c_ref[...]
    sc = c * jax.nn.sigmoid(c)
    o_ref[0] = jnp.dot(sc, w_ref[0], precision=HIGHEST, preferred_element_type=F32) + b_ref[0]


def _modulation(cond, ada_w, ada_b):
    depth, d, n = ada_w.shape
    nt = 1536
    return pl.pallas_call(
        _mod_kernel,
        out_shape=jax.ShapeDtypeStruct((depth, 8, n), F32),
        grid=(depth, n // nt),
        in_specs=[pl.BlockSpec((8, d), lambda l, j: (0, 0)),
                  pl.BlockSpec((1, d, nt), lambda l, j: (l, 0, j)),
                  pl.BlockSpec((1, 1, nt), lambda l, j: (l, 0, j))],
        out_specs=pl.BlockSpec((1, 8, nt), lambda l, j: (l, 0, j)),
        compiler_params=_cparams(("arbitrary", "arbitrary")),
        name="ada_mod",
    )(cond, ada_w, ada_b.reshape(depth, 1, n))


def _proj_kernel(x_ref, sh_ref, sc_ref, g_ref, w_ref, cos_ref, sin_ref, hg_ref, gm_ref, o_ref, *, blocks):
    h = _rms(x_ref[...], g_ref[...]) * (1.0 + sc_ref[...]) + sh_ref[...]
    y = jnp.dot(h.astype(BF16), w_ref[...], preferred_element_type=F32)
    lane = lax.broadcasted_iota(I32, (x_ref.shape[0], LANES), 1)
    first_half = (lane % HEAD_DIM) < (HEAD_DIM // 2)
    for jb, (norm, rope, scale) in enumerate(blocks):
        cols = slice(jb * LANES, (jb + 1) * LANES)
        yb = y[:, cols]
        if norm:
            ms = jnp.dot(yb * yb, gm_ref[...], precision=HIGHEST, preferred_element_type=F32)
            yb = yb * lax.rsqrt(ms + EPS) * hg_ref[:, cols]
        if rope:
            swapped = jnp.where(first_half, pltpu.roll(yb, LANES - HEAD_DIM // 2, 1),
                                pltpu.roll(yb, HEAD_DIM // 2, 1))
            yb = yb * cos_ref[...] + swapped * sin_ref[...]
        if scale != 1.0:
            yb = yb * scale
        o_ref[:, cols] = yb.astype(BF16)


def _project(x, modr, mod_base, row_of_tile, g, w, cos_t, sin_t, head_gain, blocks, tiles_per_batch):
    t, d = x.shape
    n = w.shape[1]
    group_mean = jnp.asarray(np.kron(np.eye(LANES // HEAD_DIM), np.full((HEAD_DIM, HEAD_DIM), 1.0 / HEAD_DIM)), F32)
    return pl.pallas_call(
        functools.partial(_proj_kernel, blocks=blocks),
        out_shape=jax.ShapeDtypeStruct((t, n), BF16),
        grid=(t // ROW_TILE,),
        in_specs=[pl.BlockSpec((ROW_TILE, d), lambda i: (i, 0)),
                  pl.BlockSpec((None, 1, d), lambda i: (mod_base + row_of_tile(i) * 6 + 0, 0, 0)),
                  pl.BlockSpec((None, 1, d), lambda i: (mod_base + row_of_tile(i) * 6 + 1, 0, 0)),
                  pl.BlockSpec((1, d), lambda i: (0, 0)),
                  pl.BlockSpec((d, n), lambda i: (0, 0)),
                  pl.BlockSpec((ROW_TILE, LANES), lambda i: (i % tiles_per_batch, 0)),
                  pl.BlockSpec((ROW_TILE, LANES), lambda i: (i % tiles_per_batch, 0)),
                  pl.BlockSpec((1, n), lambda i: (0, 0)),
                  pl.BlockSpec((LANES, LANES), lambda i: (0, 0))],
        out_specs=pl.BlockSpec((ROW_TILE, n), lambda i: (i, 0)),
        compiler_params=_cparams(("arbitrary",)),
        name="prenorm_proj",
    )(x, modr, modr, g, w, cos_t, sin_t, head_gain, group_mean)


def _attn_kernel(*refs, mode, n_kv, seq, ctx_len, ctx_queries, lambda_init):
    if mode == "window":
        q_ref, k_ref, v_ref, sink_ref, o_ref, m_ref, l_ref, acc_ref = refs
    elif mode == "diff":
        q_ref, k_ref, v_ref, lam_ref, sg_ref, o_ref, m_ref, l_ref, acc_ref = refs
    else:
        q_ref, k_ref, v_ref, o_ref, m_ref, l_ref, acc_ref = refs
    tq = q_ref.shape[0]
    j = pl.program_id(1)
    qi = pl.program_id(2)
    is_ctx_q = (qi == 0) if ctx_queries else False

    q = q_ref[...]
    lane = lax.broadcasted_iota(I32, (tq, LANES), 1)
    low = lane < HEAD_DIM
    zero = jnp.zeros_like(q)
    q2 = jnp.concatenate([jnp.where(low, q, zero), jnp.where(low, zero, q)], axis=0)

    if mode == "window":
        s0 = sink_ref[pl.ds(2 * j, 1), :]
        s1 = sink_ref[pl.ds(2 * j + 1, 1), :]
        row = lax.broadcasted_iota(I32, (2 * tq, LANES), 0)
        sink = jnp.where(row < tq, s0, s1) * LOG2E
        m0 = sink
    else:
        m0 = jnp.full((2 * tq, LANES), NEG_INF, F32)
    state = (m0, jnp.zeros((2 * tq, LANES), F32), jnp.zeros((2 * tq, LANES), F32))

    def chunk(state, start, valid):
        m_prev, l_prev, acc_prev = state
        k = k_ref[pl.ds(start, KV_TILE), :]
        v = v_ref[pl.ds(start, KV_TILE), :]
        s = lax.dot_general(q2, k, (((1,), (1,)), ((), ())), preferred_element_type=F32)
        if valid is not None:
            s = jnp.where(valid, s, NEG_INF)
        m_new = jnp.maximum(m_prev, jnp.max(s, axis=1, keepdims=True))
        alpha = jnp.exp2(m_prev - m_new)
        p = jnp.exp2(s - jnp.concatenate([m_new] * (KV_TILE // LANES), axis=1))
        part = p[:, :LANES]
        for c in range(1, KV_TILE // LANES):
            part = part + p[:, c * LANES:(c + 1) * LANES]
        return (m_new, alpha * l_prev + part,
                alpha * acc_prev + jnp.dot(p.astype(BF16), v, preferred_element_type=F32))

    def save(state):
        m_ref[...], l_ref[...], acc_ref[...] = state

    state = chunk(state, 0, None)
    if mode == "window":
        save(state)

        @pl.when(jnp.logical_not(is_ctx_q))
        def _():
            st = (m_ref[...], l_ref[...], acc_ref[...])
            q0 = (qi - (1 if ctx_queries else 0)) * tq
            kstart = jnp.clip(q0 - WINDOW, 0, seq - 2 * KV_TILE)
            r = lax.broadcasted_iota(I32, (2 * tq, KV_TILE), 0)
            qpos = q0 + jnp.where(r >= tq, r - tq, r)
            col = lax.broadcasted_iota(I32, (2 * tq, KV_TILE), 1)
            for w in range(2):
                kpos = kstart + w * KV_TILE + col
                st = chunk(st, pl.multiple_of(ctx_len + kstart + w * KV_TILE, WINDOW),
                           jnp.abs(qpos - kpos) <= WINDOW)
            save(st)
    elif ctx_queries:
        save(state)

        @pl.when(jnp.logical_not(is_ctx_q))
        def _():
            st = (m_ref[...], l_ref[...], acc_ref[...])
            for c in range(ctx_len // KV_TILE, n_kv):
                st = chunk(st, c * KV_TILE, None)
            save(st)
    else:
        for c in range(ctx_len // KV_TILE, n_kv):
            state = chunk(state, c * KV_TILE, None)
        save(state)

    l = jnp.sum(l_ref[...], axis=1, keepdims=True)
    if mode == "window":
        l = l + jnp.exp2(sink - m_ref[...])[:, :1]
    o2 = acc_ref[...] / l
    if mode == "diff":
        lp = lam_ref[...]
        lam = (jnp.exp(jnp.sum(lp[0:1] * lp[1:2], axis=1, keepdims=True))
               - jnp.exp(jnp.sum(lp[2:3] * lp[3:4], axis=1, keepdims=True)) + lambda_init)
        o = o2[:tq] - lam * o2[tq:]
        o = _rms(o, sg_ref[...]) * (1.0 - lambda_init)
    else:
        o = jnp.where(low, o2[:tq], o2[tq:])
    o_ref[...] = o.astype(BF16)


def _attention(p, extra, *, mode, batch, seq, ctx_len, n_qblocks, q_col0, k_col, v_col, ctx_queries,
               lambda_init=0.0):
    n_p = ctx_len + seq
    tiles = n_p // Q_TILE
    qt = tiles if ctx_queries else seq // Q_TILE
    q_off = 0 if ctx_queries else ctx_len // Q_TILE
    in_specs = [pl.BlockSpec((Q_TILE, LANES), lambda b, j, qi: (b * tiles + q_off + qi, q_col0 + j)),
                pl.BlockSpec((n_p, LANES), lambda b, j, qi: (b, k_col(j))),
                pl.BlockSpec((n_p, LANES), lambda b, j, qi: (b, v_col(j)))]
    args = [p, p, p]
    for e in extra:
        in_specs.append(pl.BlockSpec(e.shape, lambda b, j, qi: (0, 0)))
        args.append(e)
    kern = functools.partial(_attn_kernel, mode=mode, n_kv=n_p // KV_TILE, seq=seq, ctx_len=ctx_len,
                             ctx_queries=ctx_queries, lambda_init=lambda_init)
    return pl.pallas_call(
        kern,
        out_shape=jax.ShapeDtypeStruct((batch * qt * Q_TILE, n_qblocks * LANES), BF16),
        grid=(batch, n_qblocks, qt),
        in_specs=in_specs,
        out_specs=pl.BlockSpec((Q_TILE, LANES), lambda b, j, qi: (b * qt + qi, j)),
        scratch_shapes=[pltpu.VMEM((2 * Q_TILE, LANES), F32)] * 3,
        compiler_params=_cparams(("arbitrary", "arbitrary", "arbitrary")),
        name="attn_" + mode,
    )(*args)


def _out_kernel(*refs, n_o):
    o_refs = refs[:n_o]
    w_refs = refs[n_o:2 * n_o]
    x_ref, g1_ref, gate_ref, g2_ref, sh_ref, sc_ref, rw_ref, xn_ref, hp_ref, lg_ref = refs[2 * n_o:]
    y = jnp.dot(o_refs[0][...], w_refs[0][...], preferred_element_type=F32)
    for a in range(1, n_o):
        y = y + jnp.dot(o_refs[a][...], w_refs[a][...], preferred_element_type=F32)
    xn = x_ref[...] + gate_ref[...] * _rms(y, g1_ref[...])
    xn_ref[...] = xn
    h = _rms(xn, g2_ref[...]) * (1.0 + sc_ref[...]) + sh_ref[...]
    hp_ref[...] = _pack_halves(h)
    lg_ref[...] = lax.dot_general(rw_ref[...], h, (((1,), (1,)), ((), ())), precision=HIGHEST,
                                  preferred_element_type=F32)


def _out_project(os_, ws, x, x_tile, modr, mod_base, row_of_tile, g1, g2, rw_t, n_tiles):
    d = x.shape[1]
    n_o = len(os_)
    tm = n_tiles * ROW_TILE
    mspec = lambda which: pl.BlockSpec((None, 1, d), lambda i: (mod_base + row_of_tile(i) * 6 + which, 0, 0))
    in_specs = ([pl.BlockSpec((ROW_TILE, o.shape[1]), lambda i: (i, 0)) for o in os_]
                + [pl.BlockSpec(w.shape, lambda i: (0, 0)) for w in ws]
                + [pl.BlockSpec((ROW_TILE, d), lambda i: (x_tile(i), 0)),
                   pl.BlockSpec((1, d), lambda i: (0, 0)), mspec(2),
                   pl.BlockSpec((1, d), lambda i: (0, 0)), mspec(3), mspec(4),
                   pl.BlockSpec(rw_t.shape, lambda i: (0, 0))])
    return pl.pallas_call(
        functools.partial(_out_kernel, n_o=n_o),
        out_shape=(jax.ShapeDtypeStruct((tm, d), F32), jax.ShapeDtypeStruct((tm, HALF), U32),
                   jax.ShapeDtypeStruct((N_EXPERTS, tm), F32)),
        grid=(n_tiles,),
        in_specs=in_specs,
        out_specs=(pl.BlockSpec((ROW_TILE, d), lambda i: (i, 0)),
                   pl.BlockSpec((ROW_TILE, HALF), lambda i: (i, 0)),
                   pl.BlockSpec((N_EXPERTS, ROW_TILE), lambda i: (0, i))),
        compiler_params=_cparams(("arbitrary",)),
        name="out_proj",
    )(*os_, *ws, x, g1, modr, g2, modr, modr, rw_t)


def _route_kernel(lg_ref, bias_ref, tri_ref, ltri_ref, ls_ref, w_ref, segn_ref, segc_ref, cnt_ref, carry_ref):
    i = pl.program_id(0)

    @pl.when(i == 0)
    def _():
        carry_ref[...] = jnp.zeros(carry_ref.shape, F32)

    tr = lg_ref.shape[1]
    score = jax.nn.sigmoid(lg_ref[...])
    sel = score + bias_ref[...][:, :1]
    sel_j = [sel[j * GROUP_SIZE:(j + 1) * GROUP_SIZE] for j in range(GROUP_SIZE)]
    sc_j = [score[j * GROUP_SIZE:(j + 1) * GROUP_SIZE] for j in range(GROUP_SIZE)]
    gi = lax.broadcasted_iota(I32, (N_GROUPS, tr), 0)

    m1 = sel_j[0]
    m2 = jnp.full_like(m1, -jnp.inf)
    for j in range(1, GROUP_SIZE):
        m2 = jnp.maximum(m2, jnp.minimum(m1, sel_j[j]))
        m1 = jnp.maximum(m1, sel_j[j])
    gs = m1 + m2

    grank = jnp.zeros((N_GROUPS, tr), I32)
    for gp in range(N_GROUPS):
        rowv = gs[gp:gp + 1, :]
        grank = grank + jnp.where(gi > gp, jnp.where(rowv >= gs, 1, 0), jnp.where(rowv > gs, 1, 0))
    gmask = grank < TOPK_GROUPS
    val_j = [jnp.where(gmask, sel_j[j], NEG_INF) for j in range(GROUP_SIZE)]

    rank_j = [jnp.zeros((N_GROUPS, tr), I32) for _ in range(GROUP_SIZE)]
    for gp in range(N_GROUPS):
        after = gi > gp
        not_before = gi >= gp
        for jp in range(GROUP_SIZE):
            rowv = val_j[jp][gp:gp + 1, :]
            for j in range(GROUP_SIZE):
                tie = after if jp >= j else not_before
                rank_j[j] = rank_j[j] + jnp.where(tie, jnp.where(rowv >= val_j[j], 1, 0),
                                                  jnp.where(rowv > val_j[j], 1, 0))

    chosen = jnp.concatenate([jnp.where(rank_j[j] < TOP_K, 1.0, 0.0) for j in range(GROUP_SIZE)], axis=0)
    n_run = jnp.floor((jnp.sum(chosen, axis=1, keepdims=True) + (SEG_ALIGN - 1.0)) * (1.0 / SEG_ALIGN)) * SEG_ALIGN
    n_run = jnp.broadcast_to(n_run, (N_EXPERTS, LANES))
    run_start = jnp.dot(ltri_ref[...], n_run.astype(BF16), preferred_element_type=F32)
    local = jnp.dot(chosen.astype(BF16), tri_ref[...], preferred_element_type=F32) + run_start[:, :1]
    segn_ref[0] = n_run.astype(I32)
    segc_ref[0] = carry_ref[...].astype(I32)
    carry_ref[...] = carry_ref[...] + n_run
    cnt_ref[...] = carry_ref[...].astype(I32)

    w_rows, ls_rows = [], []
    for k in range(TOP_K):
        w_acc = jnp.zeros((N_GROUPS, tr), F32)
        p_acc = jnp.zeros((N_GROUPS, tr), F32)
        for j in range(GROUP_SIZE):
            hit = rank_j[j] == k
            w_acc = w_acc + jnp.where(hit, sc_j[j], 0.0)
            p_acc = p_acc + jnp.where(hit, local[j * GROUP_SIZE:(j + 1) * GROUP_SIZE], 0.0)
        w_rows.append(jnp.sum(w_acc, axis=0, keepdims=True))
        ls_rows.append(jnp.sum(p_acc, axis=0, keepdims=True))
    w_all = jnp.concatenate(w_rows, axis=0)
    w_ref[...] = w_all / jnp.sum(w_all, axis=0, keepdims=True) * ROUTED_SCALE
    ls_ref[...] = jnp.concatenate(ls_rows, axis=0).astype(I32)


def _route(lg_t, bias_rep):
    tm = lg_t.shape[1]
    n_tiles = tm // MOVE_TILE
    tri = jnp.asarray(np.triu(np.ones((MOVE_TILE, MOVE_TILE), np.float32), 1), BF16)
    ltri = jnp.asarray(np.tril(np.ones((N_EXPERTS, N_EXPERTS), np.float32), -1), BF16)
    tok = lambda i: (0, i)
    per_tile = pl.BlockSpec((1, N_EXPERTS, LANES), lambda i: (i, 0, 0))
    return pl.pallas_call(
        _route_kernel,
        out_shape=(jax.ShapeDtypeStruct((TOP_K, tm), I32), jax.ShapeDtypeStruct((TOP_K, tm), F32),
                   jax.ShapeDtypeStruct((n_tiles, N_EXPERTS, LANES), I32),
                   jax.ShapeDtypeStruct((n_tiles, N_EXPERTS, LANES), I32),
                   jax.ShapeDtypeStruct((N_EXPERTS, LANES), I32)),
        grid=(n_tiles,),
        in_specs=[pl.BlockSpec((N_EXPERTS, MOVE_TILE), tok),
                  pl.BlockSpec((N_EXPERTS, LANES), lambda i: (0, 0)),
                  pl.BlockSpec((MOVE_TILE, MOVE_TILE), lambda i: (0, 0)),
                  pl.BlockSpec((N_EXPERTS, N_EXPERTS), lambda i: (0, 0))],
        out_specs=(pl.BlockSpec((TOP_K, MOVE_TILE), tok), pl.BlockSpec((TOP_K, MOVE_TILE), tok),
                   per_tile, per_tile, pl.BlockSpec((N_EXPERTS, LANES), lambda i: (0, 0))),
        scratch_shapes=[pltpu.VMEM((N_EXPERTS, LANES), F32)],
        compiler_params=_cparams(("arbitrary",)),
        name="route",
    )(lg_t, bias_rep, tri, ltri)


def _run_copies(i, segn_ref, segd_ref, make, act):
    def per_run(r, src):
        n = segn_ref[i * N_EXPERTS + r]
        dst = segd_ref[i * N_EXPERTS + r]
        for size in RUN_PIECES:
            above = (n // (2 * size)) * (2 * size)

            @pl.when((n & size) != 0)
            def _():
                act(make(pl.multiple_of(src + above, SEG_ALIGN), pl.multiple_of(dst + above, SEG_ALIGN), size))

        return src + n

    lax.fori_loop(0, N_EXPERTS, per_run, 0)


def _fill_copies(zstart_ref, zlen_ref, make, act):
    def per_expert(r, carry):
        n = zlen_ref[r]
        dst = zstart_ref[r]
        for size in RUN_PIECES[1:]:
            above = (n // (2 * size)) * (2 * size)

            @pl.when((n & size) != 0)
            def _():
                act(make(0, pl.multiple_of(dst + above, SEG_ALIGN), size))

        return carry

    lax.fori_loop(0, N_EXPERTS, per_expert, 0)


def _dispatch_kernel(segn_ref, segd_ref, zstart_ref, zlen_ref, ls_ref, hp_ref, xs_hbm, stage_ref, zero_ref, sem):
    i = pl.program_id(0)
    lo, hi = _unpack_halves(hp_ref[...])
    lo = lo.astype(BF16)
    hi = hi.astype(BF16)
    ls = ls_ref[...]
    for c in range(STAGE_ROWS // STAGE_CHUNK):
        slot = lax.broadcasted_iota(I32, (STAGE_CHUNK, MOVE_TILE), 0) + c * STAGE_CHUNK
        onehot = jnp.zeros((STAGE_CHUNK, MOVE_TILE), F32)
        for k in range(TOP_K):
            onehot = jnp.where(ls[k:k + 1, :] == slot, 1.0, onehot)
        onehot = onehot.astype(BF16)
        a = lax.bitcast_convert_type(jnp.dot(onehot, lo, preferred_element_type=F32), U32) >> 16
        b = lax.bitcast_convert_type(jnp.dot(onehot, hi, preferred_element_type=F32), U32)
        stage_ref[c * STAGE_CHUNK:(c + 1) * STAGE_CHUNK, :] = b | a

    def to_slots(src, dst, size):
        return pltpu.make_async_copy(stage_ref.at[pl.ds(src, size), :], xs_hbm.at[pl.ds(dst, size), :], sem)

    _run_copies(i, segn_ref, segd_ref, to_slots, lambda cp: cp.start())
    _run_copies(i, segn_ref, segd_ref, to_slots, lambda cp: cp.wait())

    @pl.when(i == 0)
    def _():
        zero_ref[...] = jnp.zeros(zero_ref.shape, U32)

        def zeros_to_slots(src, dst, size):
            return pltpu.make_async_copy(zero_ref.at[pl.ds(0, size), :], xs_hbm.at[pl.ds(dst, size), :], sem)

        _fill_copies(zstart_ref, zlen_ref, zeros_to_slots, lambda cp: cp.start())
        _fill_copies(zstart_ref, zlen_ref, zeros_to_slots, lambda cp: cp.wait())


def _dispatch(segn, segd, zstart, zlen, ls, hp, n_slots):
    tm = hp.shape[0]
    tile = lambda i, *_: (i, 0)
    return pl.pallas_call(
        _dispatch_kernel,
        out_shape=jax.ShapeDtypeStruct((n_slots, HALF), U32),
        grid_spec=pltpu.PrefetchScalarGridSpec(
            num_scalar_prefetch=4,
            grid=(tm // MOVE_TILE,),
            in_specs=[pl.BlockSpec((TOP_K, MOVE_TILE), lambda i, *_: (0, i)),
                      pl.BlockSpec((MOVE_TILE, HALF), tile)],
            out_specs=pl.BlockSpec(memory_space=pl.ANY),
            scratch_shapes=[pltpu.VMEM((STAGE_ROWS, HALF), U32), pltpu.VMEM((EXPERT_BLOCK // 2, HALF), U32),
                            pltpu.SemaphoreType.DMA]),
        compiler_params=_cparams(("arbitrary",)),
        name="dispatch",
    )(segn, segd, zstart, zlen, ls, hp)


def _expert_kernel(be_ref, nu_ref, xs_ref, w1_ref, w3_ref, w2_ref, ys_ref, w1b_ref, w3b_ref, w2b_ref):
    i = pl.program_id(0)

    @pl.when(jnp.logical_or(i == 0, be_ref[i] != be_ref[jnp.maximum(i - 1, 0)]))
    def _():
        w1b_ref[...] = w1_ref[0].astype(BF16)
        w3b_ref[...] = w3_ref[0].astype(BF16)
        w2b_ref[...] = w2_ref[0].astype(BF16)

    @pl.when(i < nu_ref[0])
    def _():
        lo, hi = _unpack_halves(xs_ref[...])
        lo = lo.astype(BF16)
        hi = hi.astype(BF16)
        h1 = (jnp.dot(lo, w1b_ref[:HALF, :], preferred_element_type=F32)
              + jnp.dot(hi, w1b_ref[HALF:, :], preferred_element_type=F32))
        h3 = (jnp.dot(lo, w3b_ref[:HALF, :], preferred_element_type=F32)
              + jnp.dot(hi, w3b_ref[HALF:, :], preferred_element_type=F32))
        g = (h1 * jax.nn.sigmoid(h1) * h3).astype(BF16)
        y = jnp.dot(g, w2b_ref[...], preferred_element_type=F32)
        ys_ref[...] = _pack_halves(y)


def _experts(block_e, n_used, xs, w1, w3, w2):
    n_slots = xs.shape[0]
    n_blocks = n_slots // EXPERT_BLOCK
    d, eh = w1.shape[1], w1.shape[2]
    rows = lambda i, be, nu: (jnp.minimum(i, nu[0] - 1), 0)
    return pl.pallas_call(
        _expert_kernel,
        out_shape=jax.ShapeDtypeStruct((n_slots, HALF), U32),
        grid_spec=pltpu.PrefetchScalarGridSpec(
            num_scalar_prefetch=2,
            grid=(n_blocks,),
            in_specs=[pl.BlockSpec((EXPERT_BLOCK, HALF), rows),
                      pl.BlockSpec((1, d, eh), lambda i, be, nu: (be[i], 0, 0)),
                      pl.BlockSpec((1, d, eh), lambda i, be, nu: (be[i], 0, 0)),
                      pl.BlockSpec((1, eh, d), lambda i, be, nu: (be[i], 0, 0))],
            out_specs=pl.BlockSpec((EXPERT_BLOCK, HALF), rows),
            scratch_shapes=[pltpu.VMEM((d, eh), BF16), pltpu.VMEM((d, eh), BF16), pltpu.VMEM((eh, d), BF16)]),
        compiler_params=_cparams(("arbitrary",)),
        name="experts",
    )(block_e, n_used, xs, w1, w3, w2)


def _combine_kernel(segn_ref, segd_ref, ys_hbm, lst_ref, wt_ref, hp_ref, x_ref, s1_ref, s3_ref, s2_ref, g_ref,
                    gate_ref, o_ref, stage_ref, sem):
    i = pl.program_id(0)

    @pl.when(i == 0)
    def _():
        stage_ref[...] = jnp.zeros(stage_ref.shape, U32)

    def from_slots(src, dst, size):
        return pltpu.make_async_copy(ys_hbm.at[pl.ds(dst, size), :], stage_ref.at[pl.ds(src, size), :], sem)

    _run_copies(i, segn_ref, segd_ref, from_slots, lambda cp: cp.start())

    lo, hi = _unpack_halves(hp_ref[...])
    lo = lo.astype(BF16)
    hi = hi.astype(BF16)
    h1 = (jnp.dot(lo, s1_ref[:HALF, :], preferred_element_type=F32)
          + jnp.dot(hi, s1_ref[HALF:, :], preferred_element_type=F32))
    h3 = (jnp.dot(lo, s3_ref[:HALF, :], preferred_element_type=F32)
          + jnp.dot(hi, s3_ref[HALF:, :], preferred_element_type=F32))
    y = jnp.dot((h1 * jax.nn.sigmoid(h1) * h3).astype(BF16), s2_ref[...], preferred_element_type=F32)

    _run_copies(i, segn_ref, segd_ref, from_slots, lambda cp: cp.wait())

    lst = lst_ref[...]
    wt = wt_ref[...]
    r_lo = jnp.zeros((MOVE_TILE, HALF), F32)
    r_hi = jnp.zeros((MOVE_TILE, HALF), F32)
    for c in range(STAGE_ROWS // STAGE_CHUNK):
        slot = lax.broadcasted_iota(I32, (MOVE_TILE, STAGE_CHUNK), 1) + c * STAGE_CHUNK
        wsel = jnp.zeros((MOVE_TILE, STAGE_CHUNK), F32)
        for k in range(TOP_K):
            wsel = jnp.where(lst[:, k:k + 1] == slot, wt[:, k:k + 1], wsel)
        wsel = wsel.astype(BF16)
        a, b = _unpack_halves(stage_ref[c * STAGE_CHUNK:(c + 1) * STAGE_CHUNK, :])
        r_lo = r_lo + jnp.dot(wsel, a.astype(BF16), preferred_element_type=F32)
        r_hi = r_hi + jnp.dot(wsel, b.astype(BF16), preferred_element_type=F32)
    y = y + jnp.concatenate([r_lo, r_hi], axis=1)
    o_ref[...] = x_ref[...] + gate_ref[...] * _rms(y, g_ref[...])


def _combine(segn, segd, ys, lst, wt, hp, x, s1, s3, s2, g, modr, mod_base, row_of_tile):
    tm, d = x.shape
    full = lambda a: pl.BlockSpec(a.shape, lambda i, *_: (0, 0))
    tile = lambda i, *_: (i, 0)
    return pl.pallas_call(
        _combine_kernel,
        out_shape=jax.ShapeDtypeStruct((tm, d), F32),
        grid_spec=pltpu.PrefetchScalarGridSpec(
            num_scalar_prefetch=2,
            grid=(tm // MOVE_TILE,),
            in_specs=[pl.BlockSpec(memory_space=pl.ANY),
                      pl.BlockSpec((MOVE_TILE, TOP_K), tile), pl.BlockSpec((MOVE_TILE, TOP_K), tile),
                      pl.BlockSpec((MOVE_TILE, HALF), tile), pl.BlockSpec((MOVE_TILE, d), tile),
                      full(s1), full(s3), full(s2), full(g),
                      pl.BlockSpec((None, 1, d), lambda i, *_: (mod_base + row_of_tile(i) * 6 + 5, 0, 0))],
            out_specs=pl.BlockSpec((MOVE_TILE, d), tile),
            scratch_shapes=[pltpu.VMEM((STAGE_ROWS, HALF), U32), pltpu.SemaphoreType.DMA]),
        compiler_params=_cparams(("arbitrary",)),
        name="combine",
    )(segn, segd, ys, lst, wt, hp, x, s1, s3, s2, g, modr)


def _moe(xn, hp, lg_t, bias, w1, w3, w2, s1, s3, s2, g, modr, mod_base, row_of_tile):
    tm = xn.shape[0]
    n_tiles = tm // MOVE_TILE
    perm = np.array([(r % GROUP_SIZE) * GROUP_SIZE + r // GROUP_SIZE for r in range(N_EXPERTS)])
    bias_rep = jnp.broadcast_to(bias.astype(F32)[perm][:, None], (N_EXPERTS, LANES))
    ls, w, segn, segc, cnt = _route(lg_t, bias_rep)

    total = cnt[:, 0]
    padded = (total + EXPERT_BLOCK - 1) // EXPERT_BLOCK * EXPERT_BLOCK
    pad_end = jnp.cumsum(padded)
    pad_start = pad_end - padded
    n_slots = tm * TOP_K + (SEG_ALIGN - 1) * N_EXPERTS * n_tiles + N_EXPERTS * EXPERT_BLOCK
    n_blocks = -(-n_slots // EXPERT_BLOCK)
    starts = jnp.arange(n_blocks, dtype=I32) * EXPERT_BLOCK
    region = jnp.minimum(jnp.sum((pad_end[None, :] <= starts[:, None]).astype(I32), axis=1), N_EXPERTS - 1)
    block_e = jnp.asarray(perm, I32)[region]
    n_used = (pad_end[-1:] // EXPERT_BLOCK).astype(I32)
    segn = segn[:, :, 0].reshape(-1)
    segd = (segc[:, :, 0] + pad_start[None, :]).reshape(-1).astype(I32)

    xs = _dispatch(segn, segd, (pad_start + total).astype(I32), (padded - total).astype(I32), ls, hp,
                   n_blocks * EXPERT_BLOCK)
    ys = _experts(block_e, n_used, xs, w1, w3, w2)
    return _combine(segn, segd, ys, ls.T, w.T, hp, xn, s1.astype(BF16), s3.astype(BF16), s2.astype(BF16), g, modr,
                    mod_base, row_of_tile)


_DEINTERLEAVE = np.concatenate([np.arange(0, HEAD_DIM, 2), np.arange(1, HEAD_DIM, 2)])


def _rope_tables(seq, ctx_len):
    t = np.arange(seq)
    n_pair = HEAD_DIM // 4
    inv = jnp.asarray(ROPE_THETA, F32) ** (-jnp.arange(n_pair, dtype=F32) / n_pair)
    r = jnp.asarray(t // GRID_W, F32)
    c = jnp.asarray(t % GRID_W, F32)
    ang = jnp.concatenate([r[:, None] * inv, c[:, None] * inv], axis=-1)
    cos, sin = jnp.cos(ang), jnp.sin(ang)
    cos_t = jnp.tile(jnp.concatenate([cos, cos], axis=-1), (1, LANES // HEAD_DIM))
    sin_t = jnp.tile(jnp.concatenate([-sin, sin], axis=-1), (1, LANES // HEAD_DIM))
    cos_t = jnp.concatenate([jnp.ones((ctx_len, LANES), F32), cos_t], axis=0)
    sin_t = jnp.concatenate([jnp.zeros((ctx_len, LANES), F32), sin_t], axis=0)
    return cos_t, sin_t


def _ab_layout():
    cols, blocks, gain_kind = [], [], []
    for base_q, base_k, base_v, normed in ((0, 512, 640, True), (768, 1280, 1408, False)):
        for jb in range(4):
            cols += [base_q + h * HEAD_DIM + _DEINTERLEAVE for h in (2 * jb, 2 * jb + 1)]
            blocks.append((normed, True, Q_SCALE))
            gain_kind.append("q" if normed else None)
        for kvh in range(2):
            cols += [base_k + kvh * HEAD_DIM + _DEINTERLEAVE] * 2
            blocks.append((normed, True, 1.0))
            gain_kind.append("k" if normed else None)
        for kvh in range(2):
            cols += [base_v + kvh * HEAD_DIM + np.arange(HEAD_DIM)] * 2
            blocks.append((False, False, 1.0))
            gain_kind.append(None)
    return np.concatenate(cols), tuple(blocks), gain_kind


def _c_layout():
    cols, blocks = [], []
    for base, rope, scale in ((0, True, Q_SCALE), (1024, True, 1.0)):
        for h in range(8):
            cols += [base + (2 * h + m) * HEAD_DIM + _DEINTERLEAVE for m in range(2)]
            blocks.append((False, rope, scale))
    for h in range(8):
        cols.append(2048 + h * LANES + np.arange(LANES))
        blocks.append((False, False, 1.0))
    return np.concatenate(cols), tuple(blocks)


def kernel(x, c, ctx, c_ctx, ada_w, ada_b, norm_g, ab_w_in, ab_w_out, a_q_norm, a_k_norm, b_sink,
           c_w_in, c_w_out, c_lambda, c_subln_g, router_w, router_bias, exp_w1, exp_w3, exp_w2,
           sh_w1, sh_w3, sh_w2):
    batch, seq, d = x.shape
    ctx_len = ctx.shape[1]
    depth = ada_w.shape[0]
    assert d == D_MODEL and ctx_len == ROW_TILE and depth == 2
    assert seq % Q_TILE == 0 and seq >= 2 * KV_TILE and batch <= 4
    n_p = ctx_len + seq
    tiles = n_p // ROW_TILE
    lat_tiles = seq // ROW_TILE
    t_all = batch * n_p

    cond = jnp.zeros((8, d), F32).at[:batch].set(c).at[4].set(c_ctx)
    modr = _modulation(cond, ada_w, ada_b).reshape(depth * 8 * 6, 1, d)
    row_all = lambda i: jnp.where(i % tiles == 0, 4, i // tiles)
    cos_t, sin_t = _rope_tables(seq, ctx_len)
    rperm = np.array([(r % GROUP_SIZE) * GROUP_SIZE + r // GROUP_SIZE for r in range(N_EXPERTS)])
    xt = jnp.concatenate([ctx, x], axis=1).reshape(t_all, d)

    cols, blocks, gain_kind = _ab_layout()
    w0 = ab_w_in[0][:, cols].astype(BF16)
    gq = jnp.tile(a_q_norm[0][_DEINTERLEAVE], 2)
    gk = jnp.tile(a_k_norm[0][_DEINTERLEAVE], 2)
    ones = jnp.ones((LANES,), F32)
    head_gain = jnp.concatenate([{"q": gq, "k": gk, None: ones}[kind] for kind in gain_kind])[None, :]
    p0 = _project(xt, modr, 0, row_all, norm_g[0, 0][None, :], w0, cos_t, sin_t, head_gain, blocks, tiles)
    common = dict(batch=batch, seq=seq, ctx_len=ctx_len, n_qblocks=4, ctx_queries=True)
    oa = _attention(p0, [], mode="dense", q_col0=0, k_col=lambda j: 4 + j // 2, v_col=lambda j: 6 + j // 2,
                    **common)
    sink = jnp.broadcast_to(b_sink[0].astype(F32)[:, None], (8, LANES))
    ob = _attention(p0, [sink], mode="window", q_col0=8, k_col=lambda j: 12 + j // 2,
                    v_col=lambda j: 14 + j // 2, **common)
    w_out = ab_w_out[0].astype(BF16)
    x1, hp, lg = _out_project([oa, ob], [w_out[:512], w_out[512:]], xt, lambda i: i, modr, 0, row_all,
                              norm_g[0, 1][None, :], norm_g[0, 2][None, :], router_w[0].T[rperm],
                              t_all // ROW_TILE)
    x2 = _moe(x1, hp, lg, router_bias[0], exp_w1[0], exp_w3[0], exp_w2[0], sh_w1[0], sh_w3[0], sh_w2[0],
              norm_g[0, 3][None, :], modr, 0, lambda i: row_all(i // (ROW_TILE // MOVE_TILE)))

    base1 = 8 * 6
    lambda_init = 0.8 - 0.6 * math.exp(-0.3 * 1)
    cols1, blocks1 = _c_layout()
    w1p = c_w_in[0][:, cols1].astype(BF16)
    p1 = _project(x2, modr, base1, row_all, norm_g[1, 0][None, :], w1p, cos_t, sin_t,
                  jnp.ones((1, w1p.shape[1]), F32), blocks1, tiles)
    lam = jnp.zeros((8, LANES), F32).at[:4, :HEAD_DIM].set(c_lambda[0].astype(F32))
    oc = _attention(p1, [lam, c_subln_g[0][None, :]], mode="diff", batch=batch, seq=seq, ctx_len=ctx_len,
                    n_qblocks=8, q_col0=0, k_col=lambda j: 8 + j, v_col=lambda j: 16 + j, ctx_queries=False,
                    lambda_init=lambda_init)
    row_lat = lambda i: i // lat_tiles
    x3, hp1, lg1 = _out_project([oc], [c_w_out[0].astype(BF16)], x2,
                                lambda i: (i // lat_tiles) * tiles + 1 + i % lat_tiles, modr, base1, row_lat,
                                norm_g[1, 1][None, :], norm_g[1, 2][None, :], router_w[1].T[rperm],
                                batch * lat_tiles)
    out = _moe(x3, hp1, lg1, router_bias[1], exp_w1[1], exp_w3[1], exp_w2[1], sh_w1[1], sh_w3[1], sh_w2[1],
               norm_g[1, 3][None, :], modr, base1, lambda i: row_lat(i // (ROW_TILE // MOVE_TILE)))
    return out.reshape(batch, seq, d)
```

```python
import functools
import math

import numpy as np
import jax
import jax.numpy as jnp
from jax import lax
from jax.experimental import pallas as pl
from jax.experimental.pallas import tpu as pltpu

F32 = jnp.float32
BF16 = jnp.bfloat16
U32 = jnp.uint32
I32 = jnp.int32
HIGHEST = lax.Precision.HIGHEST

D_MODEL = 1024
HEAD_DIM = 64
LANES = 128
GRID_W = 64
ROPE_THETA = 10000.0
EPS = 1e-6
NEG_INF = -1e30
WINDOW = 128
N_EXPERTS = 64
TOP_K = 8
N_GROUPS = 8
TOPK_GROUPS = 4
GROUP_SIZE = N_EXPERTS // N_GROUPS
ROUTED_SCALE = 2.5
LOG2E = 1.4426950408889634
Q_SCALE = HEAD_DIM ** -0.5 * LOG2E

ROW_TILE = 256
Q_TILE = 256
KV_TILE = 256
LAT_KV_TILE = 256
MOVE_TILE = 256
EXPERT_BLOCK = 512
EXPERT_SLAB = 256
SEG_ALIGN = 8
RUN_PIECES = (256, 128, 64, 32, 16, 8)
WAIT_PIECES = (2048, 1024, 512) + RUN_PIECES
STAGE_ROWS = 2560
STAGE_CHUNK = 512
HALF = D_MODEL // 2
VMEM_LIMIT = 48 * 1024 * 1024


def _cparams(sem):
    return pltpu.CompilerParams(dimension_semantics=sem, vmem_limit_bytes=VMEM_LIMIT)


def _rms(x, g):
    ms = jnp.mean(x * x, axis=-1, keepdims=True)
    return x * lax.rsqrt(ms + EPS) * g


def _pack_halves(h):
    lo = lax.bitcast_convert_type(h[:, :HALF].astype(BF16).astype(F32), U32) >> 16
    hi = lax.bitcast_convert_type(h[:, HALF:].astype(BF16).astype(F32), U32) & jnp.uint32(0xFFFF0000)
    return hi | lo


def _unpack_halves(u):
    lo = lax.bitcast_convert_type(u << 16, F32)
    hi = lax.bitcast_convert_type(u & jnp.uint32(0xFFFF0000), F32)
    return lo, hi


def _mod_kernel(c_ref, w_ref, b_ref, o_ref):
    c = c_ref[...]
    sc = c * jax.nn.sigmoid(c)
    o_ref[0] = jnp.dot(sc, w_ref[0], precision=HIGHEST, preferred_element_type=F32) + b_ref[0]


def _modulation(cond, ada_w, ada_b):
    depth, d, n = ada_w.shape
    nt = 1536
    return pl.pallas_call(
        _mod_kernel,
        out_shape=jax.ShapeDtypeStruct((depth, 8, n), F32),
        grid=(depth, n // nt),
        in_specs=[pl.BlockSpec((8, d), lambda l, j: (0, 0)),
                  pl.BlockSpec((1, d, nt), lambda l, j: (l, 0, j)),
                  pl.BlockSpec((1, 1, nt), lambda l, j: (l, 0, j))],
        out_specs=pl.BlockSpec((1, 8, nt), lambda l, j: (l, 0, j)),
        compiler_params=_cparams(("arbitrary", "arbitrary")),
        name="ada_mod",
    )(cond, ada_w, ada_b.reshape(depth, 1, n))


def _proj_kernel(x_ref, sh_ref, sc_ref, g_ref, w_ref, cos_ref, sin_ref, hg_ref, gm_ref, o_ref, *, blocks):
    h = _rms(x_ref[...], g_ref[...]) * (1.0 + sc_ref[...]) + sh_ref[...]
    y = jnp.dot(h.astype(BF16), w_ref[...], preferred_element_type=F32)
    lane = lax.broadcasted_iota(I32, (x_ref.shape[0], LANES), 1)
    first_half = (lane % HEAD_DIM) < (HEAD_DIM // 2)
    for jb, (norm, rope, scale) in enumerate(blocks):
        cols = slice(jb * LANES, (jb + 1) * LANES)
        yb = y[:, cols]
        if norm:
            ms = jnp.dot(yb * yb, gm_ref[...], precision=HIGHEST, preferred_element_type=F32)
            yb = yb * lax.rsqrt(ms + EPS) * hg_ref[:, cols]
        if rope:
            swapped = jnp.where(first_half, pltpu.roll(yb, LANES - HEAD_DIM // 2, 1),
                                pltpu.roll(yb, HEAD_DIM // 2, 1))
            yb = yb * cos_ref[...] + swapped * sin_ref[...]
        if scale != 1.0:
            yb = yb * scale
        o_ref[:, cols] = yb.astype(BF16)


def _project(x, modr, mod_base, row_of_tile, g, w, cos_t, sin_t, head_gain, blocks, tiles_per_batch):
    t, d = x.shape
    n = w.shape[1]
    group_mean = jnp.asarray(np.kron(np.eye(LANES // HEAD_DIM), np.full((HEAD_DIM, HEAD_DIM), 1.0 / HEAD_DIM)), F32)
    return pl.pallas_call(
        functools.partial(_proj_kernel, blocks=blocks),
        out_shape=jax.ShapeDtypeStruct((t, n), BF16),
        grid=(t // ROW_TILE,),
        in_specs=[pl.BlockSpec((ROW_TILE, d), lambda i: (i, 0)),
                  pl.BlockSpec((None, 1, d), lambda i: (mod_base + row_of_tile(i) * 6 + 0, 0, 0)),
                  pl.BlockSpec((None, 1, d), lambda i: (mod_base + row_of_tile(i) * 6 + 1, 0, 0)),
                  pl.BlockSpec((1, d), lambda i: (0, 0)),
                  pl.BlockSpec((d, n), lambda i: (0, 0)),
                  pl.BlockSpec((ROW_TILE, LANES), lambda i: (i % tiles_per_batch, 0)),
                  pl.BlockSpec((ROW_TILE, LANES), lambda i: (i % tiles_per_batch, 0)),
                  pl.BlockSpec((1, n), lambda i: (0, 0)),
                  pl.BlockSpec((LANES, LANES), lambda i: (0, 0))],
        out_specs=pl.BlockSpec((ROW_TILE, n), lambda i: (i, 0)),
        compiler_params=_cparams(("arbitrary",)),
        name="prenorm_proj",
    )(x, modr, modr, g, w, cos_t, sin_t, head_gain, group_mean)


def _attn_kernel(*refs, mode, n_kv, seq, ctx_len, ctx_queries, lambda_init):
    if mode == "window":
        q_ref, k_ref, v_ref, sink_ref, o_ref, m_ref, l_ref, acc_ref = refs
    elif mode == "diff":
        q_ref, k_ref, v_ref, lam_ref, sg_ref, o_ref, m_ref, l_ref, acc_ref = refs
    else:
        q_ref, k_ref, v_ref, o_ref, m_ref, l_ref, acc_ref = refs
    tq = q_ref.shape[0]
    j = pl.program_id(1)
    qi = pl.program_id(2)
    is_ctx_q = (qi == 0) if ctx_queries else False

    q = q_ref[...]
    lane = lax.broadcasted_iota(I32, (tq, LANES), 1)
    low = lane < HEAD_DIM
    zero = jnp.zeros_like(q)
    q2 = jnp.concatenate([jnp.where(low, q, zero), jnp.where(low, zero, q)], axis=0)

    if mode == "window":
        s0 = sink_ref[pl.ds(2 * j, 1), :]
        s1 = sink_ref[pl.ds(2 * j + 1, 1), :]
        row = lax.broadcasted_iota(I32, (2 * tq, LANES), 0)
        sink = jnp.where(row < tq, s0, s1) * LOG2E
        m0 = sink
    else:
        m0 = jnp.full((2 * tq, LANES), NEG_INF, F32)
    state = (m0, jnp.zeros((2 * tq, LANES), F32), jnp.zeros((2 * tq, LANES), F32))

    def chunk(state, start, valid, size=KV_TILE):
        m_prev, l_prev, acc_prev = state
        k = k_ref[pl.ds(start, size), :]
        v = v_ref[pl.ds(start, size), :]
        s = lax.dot_general(q2, k, (((1,), (1,)), ((), ())), preferred_element_type=F32)
        if valid is not None:
            s = jnp.where(valid, s, NEG_INF)
        m_new = jnp.maximum(m_prev, jnp.max(s, axis=1, keepdims=True))
        alpha = jnp.exp2(m_prev - m_new)
        p = jnp.exp2(s - jnp.concatenate([m_new] * (size // LANES), axis=1))
        part = p[:, :LANES]
        for c in range(1, size // LANES):
            part = part + p[:, c * LANES:(c + 1) * LANES]
        return (m_new, alpha * l_prev + part,
                alpha * acc_prev + jnp.dot(p.astype(BF16), v, preferred_element_type=F32))

    def save(state):
        m_ref[...], l_ref[...], acc_ref[...] = state

    state = chunk(state, 0, None)
    if mode == "window":
        save(state)

        @pl.when(jnp.logical_not(is_ctx_q))
        def _():
            st = (m_ref[...], l_ref[...], acc_ref[...])
            q0 = (qi - (1 if ctx_queries else 0)) * tq
            kstart = jnp.clip(q0 - WINDOW, 0, seq - 2 * KV_TILE)
            r = lax.broadcasted_iota(I32, (2 * tq, KV_TILE), 0)
            qpos = q0 + jnp.where(r >= tq, r - tq, r)
            col = lax.broadcasted_iota(I32, (2 * tq, KV_TILE), 1)
            for w in range(2):
                kpos = kstart + w * KV_TILE + col
                st = chunk(st, pl.multiple_of(ctx_len + kstart + w * KV_TILE, WINDOW),
                           jnp.abs(qpos - kpos) <= WINDOW)
            save(st)
    elif ctx_queries:
        save(state)

        @pl.when(jnp.logical_not(is_ctx_q))
        def _():
            st = (m_ref[...], l_ref[...], acc_ref[...])
            for c in range(seq // LAT_KV_TILE):
                st = chunk(st, ctx_len + c * LAT_KV_TILE, None, LAT_KV_TILE)
            save(st)
    else:
        for c in range(seq // LAT_KV_TILE):
            state = chunk(state, ctx_len + c * LAT_KV_TILE, None, LAT_KV_TILE)
        save(state)

    l = jnp.sum(l_ref[...], axis=1, keepdims=True)
    if mode == "window":
        l = l + jnp.exp2(sink - m_ref[...])[:, :1]
    o2 = acc_ref[...] / l
    if mode == "diff":
        lp = lam_ref[...]
        lam = (jnp.exp(jnp.sum(lp[0:1] * lp[1:2], axis=1, keepdims=True))
               - jnp.exp(jnp.sum(lp[2:3] * lp[3:4], axis=1, keepdims=True)) + lambda_init)
        o = o2[:tq] - lam * o2[tq:]
        o = _rms(o, sg_ref[...]) * (1.0 - lambda_init)
    else:
        o = jnp.where(low, o2[:tq], o2[tq:])
    o_ref[...] = o.astype(BF16)


def _attention(p, extra, *, mode, batch, seq, ctx_len, n_qblocks, q_col0, k_col, v_col, ctx_queries,
               lambda_init=0.0):
    n_p = ctx_len + seq
    tiles = n_p // Q_TILE
    qt = tiles if ctx_queries else seq // Q_TILE
    q_off = 0 if ctx_queries else ctx_len // Q_TILE
    in_specs = [pl.BlockSpec((Q_TILE, LANES), lambda b, j, qi: (b * tiles + q_off + qi, q_col0 + j)),
                pl.BlockSpec((n_p, LANES), lambda b, j, qi: (b, k_col(j))),
                pl.BlockSpec((n_p, LANES), lambda b, j, qi: (b, v_col(j)))]
    args = [p, p, p]
    for e in extra:
        in_specs.append(pl.BlockSpec(e.shape, lambda b, j, qi: (0, 0)))
        args.append(e)
    kern = functools.partial(_attn_kernel, mode=mode, n_kv=n_p // KV_TILE, seq=seq, ctx_len=ctx_len,
                             ctx_queries=ctx_queries, lambda_init=lambda_init)
    return pl.pallas_call(
        kern,
        out_shape=jax.ShapeDtypeStruct((batch * qt * Q_TILE, n_qblocks * LANES), BF16),
        grid=(batch, n_qblocks, qt),
        in_specs=in_specs,
        out_specs=pl.BlockSpec((Q_TILE, LANES), lambda b, j, qi: (b * qt + qi, j)),
        scratch_shapes=[pltpu.VMEM((2 * Q_TILE, LANES), F32)] * 3,
        compiler_params=_cparams(("arbitrary", "arbitrary", "arbitrary")),
        name="attn_" + mode,
    )(*args)


def _out_kernel(*refs, n_o):
    o_refs = refs[:n_o]
    w_refs = refs[n_o:2 * n_o]
    x_ref, g1_ref, gate_ref, g2_ref, sh_ref, sc_ref, rw_ref, xn_ref, hp_ref, lg_ref = refs[2 * n_o:]
    slab = LANES
    for r0 in range(0, x_ref.shape[0], slab):
        rows = slice(r0, r0 + slab)
        y = jnp.dot(o_refs[0][rows, :], w_refs[0][...], preferred_element_type=F32)
        for a in range(1, n_o):
            y = y + jnp.dot(o_refs[a][rows, :], w_refs[a][...], preferred_element_type=F32)
        xn = x_ref[rows, :] + gate_ref[...] * _rms(y, g1_ref[...])
        xn_ref[rows, :] = xn
        h = _rms(xn, g2_ref[...]) * (1.0 + sc_ref[...]) + sh_ref[...]
        hp_ref[rows, :] = _pack_halves(h)
        lg_ref[:, rows] = lax.dot_general(rw_ref[...], h, (((1,), (1,)), ((), ())), precision=HIGHEST,
                                          preferred_element_type=F32)


def _out_project(os_, ws, x, x_tile, modr, mod_base, row_of_tile, g1, g2, rw_t, n_tiles):
    d = x.shape[1]
    n_o = len(os_)
    tm = n_tiles * ROW_TILE
    mspec = lambda which: pl.BlockSpec((None, 1, d), lambda i: (mod_base + row_of_tile(i) * 6 + which, 0, 0))
    in_specs = ([pl.BlockSpec((ROW_TILE, o.shape[1]), lambda i: (i, 0)) for o in os_]
                + [pl.BlockSpec(w.shape, lambda i: (0, 0)) for w in ws]
                + [pl.BlockSpec((ROW_TILE, d), lambda i: (x_tile(i), 0)),
                   pl.BlockSpec((1, d), lambda i: (0, 0)), mspec(2),
                   pl.BlockSpec((1, d), lambda i: (0, 0)), mspec(3), mspec(4),
                   pl.BlockSpec(rw_t.shape, lambda i: (0, 0))])
    return pl.pallas_call(
        functools.partial(_out_kernel, n_o=n_o),
        out_shape=(jax.ShapeDtypeStruct((tm, d), F32), jax.ShapeDtypeStruct((tm, HALF), U32),
                   jax.ShapeDtypeStruct((N_EXPERTS, tm), F32)),
        grid=(n_tiles,),
        in_specs=in_specs,
        out_specs=(pl.BlockSpec((ROW_TILE, d), lambda i: (i, 0)),
                   pl.BlockSpec((ROW_TILE, HALF), lambda i: (i, 0)),
                   pl.BlockSpec((N_EXPERTS, ROW_TILE), lambda i: (0, i))),
        compiler_params=_cparams(("arbitrary",)),
        name="out_proj",
    )(*os_, *ws, x, g1, modr, g2, modr, modr, rw_t)


def _route_kernel(lg_ref, bias_ref, tri_ref, ltri_ref, ls_ref, w_ref, segn_ref, segc_ref, cnt_ref, carry_ref):
    i = pl.program_id(0)

    @pl.when(i == 0)
    def _():
        carry_ref[...] = jnp.zeros(carry_ref.shape, F32)

    tr = lg_ref.shape[1]
    score = jax.nn.sigmoid(lg_ref[...])
    sel = score + bias_ref[...][:, :1]
    sel_j = [sel[j * GROUP_SIZE:(j + 1) * GROUP_SIZE] for j in range(GROUP_SIZE)]
    sc_j = [score[j * GROUP_SIZE:(j + 1) * GROUP_SIZE] for j in range(GROUP_SIZE)]
    gi = lax.broadcasted_iota(I32, (N_GROUPS, tr), 0)

    m1 = sel_j[0]
    m2 = jnp.full_like(m1, -jnp.inf)
    for j in range(1, GROUP_SIZE):
        m2 = jnp.maximum(m2, jnp.minimum(m1, sel_j[j]))
        m1 = jnp.maximum(m1, sel_j[j])
    gs = m1 + m2

    grank = jnp.zeros((N_GROUPS, tr), I32)
    for gp in range(N_GROUPS):
        rowv = gs[gp:gp + 1, :]
        grank = grank + jnp.where(gi > gp, jnp.where(rowv >= gs, 1, 0), jnp.where(rowv > gs, 1, 0))
    gmask = grank < TOPK_GROUPS
    val_j = [jnp.where(gmask, sel_j[j], NEG_INF) for j in range(GROUP_SIZE)]

    rank_j = [jnp.zeros((N_GROUPS, tr), I32) for _ in range(GROUP_SIZE)]
    for gp in range(N_GROUPS):
        after = gi > gp
        not_before = gi >= gp
        for jp in range(GROUP_SIZE):
            rowv = val_j[jp][gp:gp + 1, :]
            for j in range(GROUP_SIZE):
                tie = after if jp >= j else not_before
                rank_j[j] = rank_j[j] + jnp.where(tie, jnp.where(rowv >= val_j[j], 1, 0),
                                                  jnp.where(rowv > val_j[j], 1, 0))

    chosen = jnp.concatenate([jnp.where(rank_j[j] < TOP_K, 1.0, 0.0) for j in range(GROUP_SIZE)], axis=0)
    n_run = jnp.floor((jnp.sum(chosen, axis=1, keepdims=True) + (SEG_ALIGN - 1.0)) * (1.0 / SEG_ALIGN)) * SEG_ALIGN
    n_run = jnp.broadcast_to(n_run, (N_EXPERTS, LANES))
    run_start = jnp.dot(ltri_ref[...], n_run.astype(BF16), preferred_element_type=F32)
    local = jnp.dot(chosen.astype(BF16), tri_ref[...], preferred_element_type=F32) + run_start[:, :1]
    segn_ref[0] = n_run.astype(I32)
    segc_ref[0] = carry_ref[...].astype(I32)
    carry_ref[...] = carry_ref[...] + n_run
    cnt_ref[...] = carry_ref[...].astype(I32)

    w_rows, ls_rows = [], []
    for k in range(TOP_K):
        w_acc = jnp.zeros((N_GROUPS, tr), F32)
        p_acc = jnp.zeros((N_GROUPS, tr), F32)
        for j in range(GROUP_SIZE):
            hit = rank_j[j] == k
            w_acc = w_acc + jnp.where(hit, sc_j[j], 0.0)
            p_acc = p_acc + jnp.where(hit, local[j * GROUP_SIZE:(j + 1) * GROUP_SIZE], 0.0)
        w_rows.append(jnp.sum(w_acc, axis=0, keepdims=True))
        ls_rows.append(jnp.sum(p_acc, axis=0, keepdims=True))
    w_all = jnp.concatenate(w_rows, axis=0)
    w_ref[...] = w_all / jnp.sum(w_all, axis=0, keepdims=True) * ROUTED_SCALE
    ls_ref[...] = jnp.concatenate(ls_rows, axis=0).astype(I32)


def _route(lg_t, bias_rep):
    tm = lg_t.shape[1]
    n_tiles = tm // MOVE_TILE
    tri = jnp.asarray(np.triu(np.ones((MOVE_TILE, MOVE_TILE), np.float32), 1), BF16)
    ltri = jnp.asarray(np.tril(np.ones((N_EXPERTS, N_EXPERTS), np.float32), -1), BF16)
    tok = lambda i: (0, i)
    per_tile = pl.BlockSpec((1, N_EXPERTS, LANES), lambda i: (i, 0, 0))
    return pl.pallas_call(
        _route_kernel,
        out_shape=(jax.ShapeDtypeStruct((TOP_K, tm), I32), jax.ShapeDtypeStruct((TOP_K, tm), F32),
                   jax.ShapeDtypeStruct((n_tiles, N_EXPERTS, LANES), I32),
                   jax.ShapeDtypeStruct((n_tiles, N_EXPERTS, LANES), I32),
                   jax.ShapeDtypeStruct((N_EXPERTS, LANES), I32)),
        grid=(n_tiles,),
        in_specs=[pl.BlockSpec((N_EXPERTS, MOVE_TILE), tok),
                  pl.BlockSpec((N_EXPERTS, LANES), lambda i: (0, 0)),
                  pl.BlockSpec((MOVE_TILE, MOVE_TILE), lambda i: (0, 0)),
                  pl.BlockSpec((N_EXPERTS, N_EXPERTS), lambda i: (0, 0))],
        out_specs=(pl.BlockSpec((TOP_K, MOVE_TILE), tok), pl.BlockSpec((TOP_K, MOVE_TILE), tok),
                   per_tile, per_tile, pl.BlockSpec((N_EXPERTS, LANES), lambda i: (0, 0))),
        scratch_shapes=[pltpu.VMEM((N_EXPERTS, LANES), F32)],
        compiler_params=_cparams(("arbitrary",)),
        name="route",
    )(lg_t, bias_rep, tri, ltri)


def _start_pieces(n, src, dst, make, pieces):
    for size in pieces:
        above = n & (-2 * size)

        @pl.when((n & size) != 0)
        def _():
            make(pl.multiple_of(src + above, SEG_ALIGN), pl.multiple_of(dst + above, SEG_ALIGN), size).start()


def _start_run_copies(i, segn_ref, segd_ref, make):
    split = RUN_PIECES.index(64)

    def per_run(r, src):
        n = segn_ref[i * N_EXPERTS + r]
        dst = segd_ref[i * N_EXPERTS + r]

        @pl.when(n >= RUN_PIECES[split - 1])
        def _():
            _start_pieces(n, src, dst, make, RUN_PIECES[:split])

        _start_pieces(n, src, dst, make, RUN_PIECES[split:])
        return src + n

    return lax.fori_loop(0, N_EXPERTS, per_run, 0, unroll=2)


def _wait_rows(total, make):
    for size in WAIT_PIECES:
        @pl.when((total & size) != 0)
        def _():
            make(0, 0, size).wait()


def _fill_copies(zstart_ref, zlen_ref, make):
    pieces = tuple(p for p in RUN_PIECES if p < EXPERT_BLOCK)

    def per_expert(r, carry):
        n = zlen_ref[r]
        _start_pieces(n, 0, zstart_ref[r], make, pieces)
        _wait_rows(n, make)
        return carry

    lax.fori_loop(0, N_EXPERTS, per_expert, 0)


def _dispatch_kernel(segn_ref, segd_ref, zstart_ref, zlen_ref, ls_ref, hp_ref, xs_hbm, stage_ref, zero_ref, sem):
    i = pl.program_id(0)
    lo, hi = _unpack_halves(hp_ref[...])
    lo = lo.astype(BF16)
    hi = hi.astype(BF16)
    ls = ls_ref[...]
    pack_rows = 16
    ls16 = [jnp.tile(jnp.broadcast_to(ls[k:k + 1, :], (pack_rows, MOVE_TILE)).astype(jnp.int16),
                     (STAGE_CHUNK // pack_rows, 1)) for k in range(TOP_K)]
    slot0 = lax.broadcasted_iota(I32, (STAGE_CHUNK, MOVE_TILE), 0).astype(jnp.int16)
    one = jnp.ones((STAGE_CHUNK, MOVE_TILE), BF16)
    for c in range(STAGE_ROWS // STAGE_CHUNK):
        slot = slot0 + jnp.int16(c * STAGE_CHUNK)
        onehot = jnp.zeros((STAGE_CHUNK, MOVE_TILE), BF16)
        for k in range(TOP_K):
            onehot = jnp.where(ls16[k] == slot, one, onehot)
        a = lax.bitcast_convert_type(jnp.dot(onehot, lo, preferred_element_type=F32), U32) >> 16
        b = lax.bitcast_convert_type(jnp.dot(onehot, hi, preferred_element_type=F32), U32)
        stage_ref[c * STAGE_CHUNK:(c + 1) * STAGE_CHUNK, :] = b | a

    def to_slots(src, dst, size):
        return pltpu.make_async_copy(stage_ref.at[pl.ds(src, size), :], xs_hbm.at[pl.ds(dst, size), :], sem)

    _wait_rows(_start_run_copies(i, segn_ref, segd_ref, to_slots), to_slots)

    @pl.when(i == 0)
    def _():
        zero_ref[...] = jnp.zeros(zero_ref.shape, U32)

        def zeros_to_slots(src, dst, size):
            return pltpu.make_async_copy(zero_ref.at[pl.ds(0, size), :], xs_hbm.at[pl.ds(dst, size), :], sem)

        _fill_copies(zstart_ref, zlen_ref, zeros_to_slots)


def _dispatch(segn, segd, zstart, zlen, ls, hp, n_slots):
    tm = hp.shape[0]
    tile = lambda i, *_: (i, 0)
    return pl.pallas_call(
        _dispatch_kernel,
        out_shape=jax.ShapeDtypeStruct((n_slots, HALF), U32),
        grid_spec=pltpu.PrefetchScalarGridSpec(
            num_scalar_prefetch=4,
            grid=(tm // MOVE_TILE,),
            in_specs=[pl.BlockSpec((TOP_K, MOVE_TILE), lambda i, *_: (0, i)),
                      pl.BlockSpec((MOVE_TILE, HALF), tile)],
            out_specs=pl.BlockSpec(memory_space=pl.ANY),
            scratch_shapes=[pltpu.VMEM((STAGE_ROWS, HALF), U32), pltpu.VMEM((EXPERT_BLOCK // 2, HALF), U32),
                            pltpu.SemaphoreType.DMA]),
        compiler_params=_cparams(("arbitrary",)),
        name="dispatch",
    )(segn, segd, zstart, zlen, ls, hp)


def _expert_kernel(be_ref, nu_ref, xs_ref, w1_ref, w3_ref, w2_ref, ys_ref, w1b_ref, w3b_ref, w2b_ref):
    i = pl.program_id(0)

    @pl.when(jnp.logical_or(i == 0, be_ref[i] != be_ref[jnp.maximum(i - 1, 0)]))
    def _():
        w1b_ref[...] = w1_ref[0].astype(BF16)
        w3b_ref[...] = w3_ref[0].astype(BF16)
        w2b_ref[...] = w2_ref[0].astype(BF16)

    @pl.when(i < nu_ref[0])
    def _():
        for rows in (slice(r0, r0 + EXPERT_SLAB) for r0 in range(0, EXPERT_BLOCK, EXPERT_SLAB)):
            lo, hi = _unpack_halves(xs_ref[rows, :])
            lo = lo.astype(BF16)
            hi = hi.astype(BF16)
            h1 = (jnp.dot(lo, w1b_ref[:HALF, :], preferred_element_type=F32)
                  + jnp.dot(hi, w1b_ref[HALF:, :], preferred_element_type=F32))
            h3 = (jnp.dot(lo, w3b_ref[:HALF, :], preferred_element_type=F32)
                  + jnp.dot(hi, w3b_ref[HALF:, :], preferred_element_type=F32))
            g = (h1 * jax.nn.sigmoid(h1) * h3).astype(BF16)
            y = jnp.dot(g, w2b_ref[...], preferred_element_type=F32)
            ys_ref[rows, :] = _pack_halves(y)


def _experts(block_e, n_used, xs, w1, w3, w2, layer):
    n_slots = xs.shape[0]
    n_blocks = n_slots // EXPERT_BLOCK
    d, eh = w1.shape[2], w1.shape[3]
    rows = lambda i, be, nu: (jnp.minimum(i, nu[0] - 1), 0)
    return pl.pallas_call(
        _expert_kernel,
        out_shape=jax.ShapeDtypeStruct((n_slots, HALF), U32),
        grid_spec=pltpu.PrefetchScalarGridSpec(
            num_scalar_prefetch=2,
            grid=(n_blocks,),
            in_specs=[pl.BlockSpec((EXPERT_BLOCK, HALF), rows),
                      pl.BlockSpec((None, 1, d, eh), lambda i, be, nu: (layer, be[i], 0, 0)),
                      pl.BlockSpec((None, 1, d, eh), lambda i, be, nu: (layer, be[i], 0, 0)),
                      pl.BlockSpec((None, 1, eh, d), lambda i, be, nu: (layer, be[i], 0, 0))],
            out_specs=pl.BlockSpec((EXPERT_BLOCK, HALF), rows),
            scratch_shapes=[pltpu.VMEM((d, eh), BF16), pltpu.VMEM((d, eh), BF16), pltpu.VMEM((eh, d), BF16)]),
        compiler_params=_cparams(("arbitrary",)),
        name="experts",
    )(block_e, n_used, xs, w1, w3, w2)


def _combine_kernel(segn_ref, segd_ref, ys_hbm, lst_ref, wt_ref, hp_ref, x_ref, s1_ref, s3_ref, s2_ref, g_ref,
                    gate_ref, o_ref, stage_ref, sem):
    i = pl.program_id(0)

    @pl.when(i == 0)
    def _():
        stage_ref[...] = jnp.zeros(stage_ref.shape, U32)

    def from_slots(src, dst, size):
        return pltpu.make_async_copy(ys_hbm.at[pl.ds(dst, size), :], stage_ref.at[pl.ds(src, size), :], sem)

    rows_in_flight = _start_run_copies(i, segn_ref, segd_ref, from_slots)

    lo, hi = _unpack_halves(hp_ref[...])
    lo = lo.astype(BF16)
    hi = hi.astype(BF16)
    h1 = (jnp.dot(lo, s1_ref[:HALF, :], preferred_element_type=F32)
          + jnp.dot(hi, s1_ref[HALF:, :], preferred_element_type=F32))
    h3 = (jnp.dot(lo, s3_ref[:HALF, :], preferred_element_type=F32)
          + jnp.dot(hi, s3_ref[HALF:, :], preferred_element_type=F32))
    y = jnp.dot((h1 * jax.nn.sigmoid(h1) * h3).astype(BF16), s2_ref[...], preferred_element_type=F32)

    _wait_rows(rows_in_flight, from_slots)

    lst = lst_ref[...]
    wt = wt_ref[...]
    lst16 = [jnp.broadcast_to(lst[:, k:k + 1], (MOVE_TILE, STAGE_CHUNK)).astype(jnp.int16) for k in range(TOP_K)]
    wt16 = [jnp.broadcast_to(wt[:, k:k + 1], (MOVE_TILE, STAGE_CHUNK)).astype(BF16) for k in range(TOP_K)]
    slot0 = lax.broadcasted_iota(I32, (MOVE_TILE, STAGE_CHUNK), 1).astype(jnp.int16)
    r_lo = jnp.zeros((MOVE_TILE, HALF), F32)
    r_hi = jnp.zeros((MOVE_TILE, HALF), F32)
    for c in range(STAGE_ROWS // STAGE_CHUNK):
        slot = slot0 + jnp.int16(c * STAGE_CHUNK)
        wsel = jnp.zeros((MOVE_TILE, STAGE_CHUNK), BF16)
        for k in range(TOP_K):
            wsel = jnp.where(lst16[k] == slot, wt16[k], wsel)
        a, b = _unpack_halves(stage_ref[c * STAGE_CHUNK:(c + 1) * STAGE_CHUNK, :])
        r_lo = r_lo + jnp.dot(wsel, a.astype(BF16), preferred_element_type=F32)
        r_hi = r_hi + jnp.dot(wsel, b.astype(BF16), preferred_element_type=F32)
    y = y + jnp.concatenate([r_lo, r_hi], axis=1)
    o_ref[...] = x_ref[...] + gate_ref[...] * _rms(y, g_ref[...])


def _combine(segn, segd, ys, lst, wt, hp, x, s1, s3, s2, g, modr, mod_base, row_of_tile):
    tm, d = x.shape
    full = lambda a: pl.BlockSpec(a.shape, lambda i, *_: (0, 0))
    tile = lambda i, *_: (i, 0)
    return pl.pallas_call(
        _combine_kernel,
        out_shape=jax.ShapeDtypeStruct((tm, d), F32),
        grid_spec=pltpu.PrefetchScalarGridSpec(
            num_scalar_prefetch=2,
            grid=(tm // MOVE_TILE,),
            in_specs=[pl.BlockSpec(memory_space=pl.ANY),
                      pl.BlockSpec((MOVE_TILE, TOP_K), tile), pl.BlockSpec((MOVE_TILE, TOP_K), tile),
                      pl.BlockSpec((MOVE_TILE, HALF), tile), pl.BlockSpec((MOVE_TILE, d), tile),
                      full(s1), full(s3), full(s2), full(g),
                      pl.BlockSpec((None, 1, d), lambda i, *_: (mod_base + row_of_tile(i) * 6 + 5, 0, 0))],
            out_specs=pl.BlockSpec((MOVE_TILE, d), tile),
            scratch_shapes=[pltpu.VMEM((STAGE_ROWS, HALF), U32), pltpu.SemaphoreType.DMA]),
        compiler_params=_cparams(("arbitrary",)),
        name="combine",
    )(segn, segd, ys, lst, wt, hp, x, s1, s3, s2, g, modr)


def _moe(xn, hp, lg_t, bias, expert_weights, s1, s3, s2, g, modr, mod_base, row_of_tile):
    tm = xn.shape[0]
    n_tiles = tm // MOVE_TILE
    perm = np.array([(r % GROUP_SIZE) * GROUP_SIZE + r // GROUP_SIZE for r in range(N_EXPERTS)])
    bias_rep = jnp.broadcast_to(bias.astype(F32)[perm][:, None], (N_EXPERTS, LANES))
    ls, w, segn, segc, cnt = _route(lg_t, bias_rep)

    total = cnt[:, 0]
    padded = (total + EXPERT_BLOCK - 1) // EXPERT_BLOCK * EXPERT_BLOCK
    pad_end = jnp.cumsum(padded)
    pad_start = pad_end - padded
    n_slots = tm * TOP_K + (SEG_ALIGN - 1) * N_EXPERTS * n_tiles + N_EXPERTS * EXPERT_BLOCK
    n_blocks = -(-n_slots // EXPERT_BLOCK)
    starts = jnp.arange(n_blocks, dtype=I32) * EXPERT_BLOCK
    region = jnp.minimum(jnp.sum((pad_end[None, :] <= starts[:, None]).astype(I32), axis=1), N_EXPERTS - 1)
    block_e = jnp.asarray(perm, I32)[region]
    n_used = (pad_end[-1:] // EXPERT_BLOCK).astype(I32)
    segn = segn[:, :, 0].reshape(-1)
    segd = (segc[:, :, 0] + pad_start[None, :]).reshape(-1).astype(I32)

    xs = _dispatch(segn, segd, (pad_start + total).astype(I32), (padded - total).astype(I32), ls, hp,
                   n_blocks * EXPERT_BLOCK)
    ys = _experts(block_e, n_used, xs, *expert_weights)
    return _combine(segn, segd, ys, ls.T, w.T, hp, xn, s1.astype(BF16), s3.astype(BF16), s2.astype(BF16), g, modr,
                    mod_base, row_of_tile)


_DEINTERLEAVE = np.concatenate([np.arange(0, HEAD_DIM, 2), np.arange(1, HEAD_DIM, 2)])


def _rope_tables(seq, ctx_len):
    t = np.arange(seq)
    n_pair = HEAD_DIM // 4
    inv = jnp.asarray(ROPE_THETA, F32) ** (-jnp.arange(n_pair, dtype=F32) / n_pair)
    r = jnp.asarray(t // GRID_W, F32)
    c = jnp.asarray(t % GRID_W, F32)
    ang = jnp.concatenate([r[:, None] * inv, c[:, None] * inv], axis=-1)
    cos, sin = jnp.cos(ang), jnp.sin(ang)
    cos_t = jnp.tile(jnp.concatenate([cos, cos], axis=-1), (1, LANES // HEAD_DIM))
    sin_t = jnp.tile(jnp.concatenate([-sin, sin], axis=-1), (1, LANES // HEAD_DIM))
    cos_t = jnp.concatenate([jnp.ones((ctx_len, LANES), F32), cos_t], axis=0)
    sin_t = jnp.concatenate([jnp.zeros((ctx_len, LANES), F32), sin_t], axis=0)
    return cos_t, sin_t


def _ab_layout():
    cols, blocks, gain_kind = [], [], []
    for base_q, base_k, base_v, normed in ((0, 512, 640, True), (768, 1280, 1408, False)):
        for jb in range(4):
            cols += [base_q + h * HEAD_DIM + _DEINTERLEAVE for h in (2 * jb, 2 * jb + 1)]
            blocks.append((normed, True, Q_SCALE))
            gain_kind.append("q" if normed else None)
        for kvh in range(2):
            cols += [base_k + kvh * HEAD_DIM + _DEINTERLEAVE] * 2
            blocks.append((normed, True, 1.0))
            gain_kind.append("k" if normed else None)
        for kvh in range(2):
            cols += [base_v + kvh * HEAD_DIM + np.arange(HEAD_DIM)] * 2
            blocks.append((False, False, 1.0))
            gain_kind.append(None)
    return np.concatenate(cols), tuple(blocks), gain_kind


def _c_layout():
    cols, blocks = [], []
    for base, rope, scale in ((0, True, Q_SCALE), (1024, True, 1.0)):
        for h in range(8):
            cols += [base + (2 * h + m) * HEAD_DIM + _DEINTERLEAVE for m in range(2)]
            blocks.append((False, rope, scale))
    for h in range(8):
        cols.append(2048 + h * LANES + np.arange(LANES))
        blocks.append((False, False, 1.0))
    return np.concatenate(cols), tuple(blocks)


def kernel(x, c, ctx, c_ctx, ada_w, ada_b, norm_g, ab_w_in, ab_w_out, a_q_norm, a_k_norm, b_sink,
           c_w_in, c_w_out, c_lambda, c_subln_g, router_w, router_bias, exp_w1, exp_w3, exp_w2,
           sh_w1, sh_w3, sh_w2):
    batch, seq, d = x.shape
    ctx_len = ctx.shape[1]
    depth = ada_w.shape[0]
    assert d == D_MODEL and ctx_len == ROW_TILE and depth == 2
    assert seq % Q_TILE == 0 and seq >= 2 * KV_TILE and batch <= 4
    n_p = ctx_len + seq
    tiles = n_p // ROW_TILE
    lat_tiles = seq // ROW_TILE
    t_all = batch * n_p

    cond = jnp.zeros((8, d), F32).at[:batch].set(c).at[4].set(c_ctx)
    modr = _modulation(cond, ada_w, ada_b).reshape(depth * 8 * 6, 1, d)
    row_all = lambda i: jnp.where(i % tiles == 0, 4, i // tiles)
    cos_t, sin_t = _rope_tables(seq, ctx_len)
    rperm = np.array([(r % GROUP_SIZE) * GROUP_SIZE + r // GROUP_SIZE for r in range(N_EXPERTS)])
    xt = jnp.concatenate([ctx, x], axis=1).reshape(t_all, d)

    cols, blocks, gain_kind = _ab_layout()
    w0 = ab_w_in[0][:, cols].astype(BF16)
    gq = jnp.tile(a_q_norm[0][_DEINTERLEAVE], 2)
    gk = jnp.tile(a_k_norm[0][_DEINTERLEAVE], 2)
    ones = jnp.ones((LANES,), F32)
    head_gain = jnp.concatenate([{"q": gq, "k": gk, None: ones}[kind] for kind in gain_kind])[None, :]
    p0 = _project(xt, modr, 0, row_all, norm_g[0, 0][None, :], w0, cos_t, sin_t, head_gain, blocks, tiles)
    common = dict(batch=batch, seq=seq, ctx_len=ctx_len, n_qblocks=4, ctx_queries=True)
    oa = _attention(p0, [], mode="dense", q_col0=0, k_col=lambda j: 4 + j // 2, v_col=lambda j: 6 + j // 2,
                    **common)
    sink = jnp.broadcast_to(b_sink[0].astype(F32)[:, None], (8, LANES))
    ob = _attention(p0, [sink], mode="window", q_col0=8, k_col=lambda j: 12 + j // 2,
                    v_col=lambda j: 14 + j // 2, **common)
    w_out = ab_w_out[0].astype(BF16)
    x1, hp, lg = _out_project([oa, ob], [w_out[:512], w_out[512:]], xt, lambda i: i, modr, 0, row_all,
                              norm_g[0, 1][None, :], norm_g[0, 2][None, :], router_w[0].T[rperm],
                              t_all // ROW_TILE)
    x2 = _moe(x1, hp, lg, router_bias[0], (exp_w1, exp_w3, exp_w2, 0), sh_w1[0], sh_w3[0], sh_w2[0],
              norm_g[0, 3][None, :], modr, 0, lambda i: row_all(i // (ROW_TILE // MOVE_TILE)))

    base1 = 8 * 6
    lambda_init = 0.8 - 0.6 * math.exp(-0.3 * 1)
    cols1, blocks1 = _c_layout()
    w1p = c_w_in[0][:, cols1].astype(BF16)
    p1 = _project(x2, modr, base1, row_all, norm_g[1, 0][None, :], w1p, cos_t, sin_t,
                  jnp.ones((1, w1p.shape[1]), F32), blocks1, tiles)
    lam = jnp.zeros((8, LANES), F32).at[:4, :HEAD_DIM].set(c_lambda[0].astype(F32))
    oc = _attention(p1, [lam, c_subln_g[0][None, :]], mode="diff", batch=batch, seq=seq, ctx_len=ctx_len,
                    n_qblocks=8, q_col0=0, k_col=lambda j: 8 + j, v_col=lambda j: 16 + j, ctx_queries=False,
                    lambda_init=lambda_init)
    row_lat = lambda i: i // lat_tiles
    x3, hp1, lg1 = _out_project([oc], [c_w_out[0].astype(BF16)], x2,
                                lambda i: (i // lat_tiles) * tiles + 1 + i % lat_tiles, modr, base1, row_lat,
                                norm_g[1, 1][None, :], norm_g[1, 2][None, :], router_w[1].T[rperm],
                                batch * lat_tiles)
    out = _moe(x3, hp1, lg1, router_bias[1], (exp_w1, exp_w3, exp_w2, 1), sh_w1[1], sh_w3[1], sh_w2[1],
               norm_g[1, 3][None, :], modr, base1, lambda i: row_lat(i // (ROW_TILE // MOVE_TILE)))
    return out.reshape(batch, seq, d)
```

```python
import functools
import math

import numpy as np
import jax
import jax.numpy as jnp
from jax import lax
from jax.experimental import pallas as pl
from jax.experimental.pallas import tpu as pltpu

F32 = jnp.float32
BF16 = jnp.bfloat16
U32 = jnp.uint32
I32 = jnp.int32
HIGHEST = lax.Precision.HIGHEST

D_MODEL = 1024
HEAD_DIM = 64
LANES = 128
SUBLANES = 8
GRID_W = 64
ROPE_THETA = 10000.0
EPS = 1e-6
NEG_INF = -1e30
WINDOW = 128
N_EXPERTS = 64
TOP_K = 8
N_GROUPS = 8
TOPK_GROUPS = 4
GROUP_SIZE = N_EXPERTS // N_GROUPS
ROUTED_SCALE = 2.5
LOG2E = 1.4426950408889634
Q_SCALE = HEAD_DIM ** -0.5 * LOG2E

ROW_TILE = 256
Q_TILE = 256
KV_TILE = 256
LAT_KV_TILE = 256
MOVE_TILE = 256
EXPERT_BLOCK = 512
EXPERT_SLAB = 256
SEG_ALIGN = 8
RUN_PIECES = (256, 128, 64, 32, 16, 8)
WAIT_PIECES = (2048, 1024, 512) + RUN_PIECES
STAGE_ROWS = 2560
STAGE_CHUNK = 512
HALF = D_MODEL // 2
VMEM_LIMIT = 48 * 1024 * 1024


def _cparams(sem):
    return pltpu.CompilerParams(dimension_semantics=sem, vmem_limit_bytes=VMEM_LIMIT)


def _rms(x, g):
    ms = jnp.mean(x * x, axis=-1, keepdims=True)
    return x * lax.rsqrt(ms + EPS) * g


def _pack_halves(h):
    lo = lax.bitcast_convert_type(h[:, :HALF].astype(BF16).astype(F32), U32) >> 16
    hi = lax.bitcast_convert_type(h[:, HALF:].astype(BF16).astype(F32), U32) & jnp.uint32(0xFFFF0000)
    return hi | lo


def _unpack_halves(u):
    lo = lax.bitcast_convert_type(u << 16, F32)
    hi = lax.bitcast_convert_type(u & jnp.uint32(0xFFFF0000), F32)
    return lo, hi


def _mod_kernel(c_ref, w_ref, b_ref, o_ref):
    c = c_ref[...]
    sc = c * jax.nn.sigmoid(c)
    o_ref[0] = jnp.dot(sc, w_ref[0], precision=HIGHEST, preferred_element_type=F32) + b_ref[0]


def _modulation(cond, ada_w, ada_b):
    depth, d, n = ada_w.shape
    nt = 1536
    return pl.pallas_call(
        _mod_kernel,
        out_shape=jax.ShapeDtypeStruct((depth, 8, n), F32),
        grid=(depth, n // nt),
        in_specs=[pl.BlockSpec((8, d), lambda l, j: (0, 0)),
                  pl.BlockSpec((1, d, nt), lambda l, j: (l, 0, j)),
                  pl.BlockSpec((1, 1, nt), lambda l, j: (l, 0, j))],
        out_specs=pl.BlockSpec((1, 8, nt), lambda l, j: (l, 0, j)),
        compiler_params=_cparams(("arbitrary", "arbitrary")),
        name="ada_mod",
    )(cond, ada_w, ada_b.reshape(depth, 1, n))


def _proj_kernel(x_ref, sh_ref, sc_ref, g_ref, w_ref, cos_ref, sin_ref, hg_ref, gm_ref, o_ref, *, blocks):
    h = _rms(x_ref[...], g_ref[...]) * (1.0 + sc_ref[...]) + sh_ref[...]
    y = jnp.dot(h.astype(BF16), w_ref[...], preferred_element_type=F32)
    lane = lax.broadcasted_iota(I32, (x_ref.shape[0], LANES), 1)
    first_half = (lane % HEAD_DIM) < (HEAD_DIM // 2)
    for jb, (norm, rope, scale) in enumerate(blocks):
        cols = slice(jb * LANES, (jb + 1) * LANES)
        yb = y[:, cols]
        if norm:
            ms = jnp.dot(yb * yb, gm_ref[...], precision=HIGHEST, preferred_element_type=F32)
            yb = yb * lax.rsqrt(ms + EPS) * hg_ref[:, cols]
        if rope:
            swapped = jnp.where(first_half, pltpu.roll(yb, LANES - HEAD_DIM // 2, 1),
                                pltpu.roll(yb, HEAD_DIM // 2, 1))
            yb = yb * cos_ref[...] + swapped * sin_ref[...]
        if scale != 1.0:
            yb = yb * scale
        o_ref[:, cols] = yb.astype(BF16)


def _project(x, modr, mod_base, row_of_tile, g, w, cos_t, sin_t, head_gain, blocks, tiles_per_batch):
    t, d = x.shape
    n = w.shape[1]
    group_mean = jnp.asarray(np.kron(np.eye(LANES // HEAD_DIM), np.full((HEAD_DIM, HEAD_DIM), 1.0 / HEAD_DIM)), F32)
    return pl.pallas_call(
        functools.partial(_proj_kernel, blocks=blocks),
        out_shape=jax.ShapeDtypeStruct((t, n), BF16),
        grid=(t // ROW_TILE,),
        in_specs=[pl.BlockSpec((ROW_TILE, d), lambda i: (i, 0)),
                  pl.BlockSpec((None, 1, d), lambda i: (mod_base + row_of_tile(i) * 6 + 0, 0, 0)),
                  pl.BlockSpec((None, 1, d), lambda i: (mod_base + row_of_tile(i) * 6 + 1, 0, 0)),
                  pl.BlockSpec((1, d), lambda i: (0, 0)),
                  pl.BlockSpec((d, n), lambda i: (0, 0)),
                  pl.BlockSpec((ROW_TILE, LANES), lambda i: (i % tiles_per_batch, 0)),
                  pl.BlockSpec((ROW_TILE, LANES), lambda i: (i % tiles_per_batch, 0)),
                  pl.BlockSpec((1, n), lambda i: (0, 0)),
                  pl.BlockSpec((LANES, LANES), lambda i: (0, 0))],
        out_specs=pl.BlockSpec((ROW_TILE, n), lambda i: (i, 0)),
        compiler_params=_cparams(("arbitrary",)),
        name="prenorm_proj",
    )(x, modr, modr, g, w, cos_t, sin_t, head_gain, group_mean)


def _attn_kernel(*refs, mode, n_kv, seq, ctx_len, ctx_queries, lambda_init):
    if mode == "window":
        q_ref, k_ref, v_ref, sink_ref, o_ref, m_ref, l_ref, acc_ref = refs
    elif mode == "diff":
        q_ref, k_ref, v_ref, lam_ref, sg_ref, o_ref, m_ref, l_ref, acc_ref = refs
    else:
        q_ref, k_ref, v_ref, o_ref, m_ref, l_ref, acc_ref = refs
    tq = q_ref.shape[0]
    j = pl.program_id(1)
    qi = pl.program_id(2)
    is_ctx_q = (qi == 0) if ctx_queries else False

    q = q_ref[...]
    lane = lax.broadcasted_iota(I32, (tq, LANES), 1)
    low = lane < HEAD_DIM
    zero = jnp.zeros_like(q)
    q2 = jnp.concatenate([jnp.where(low, q, zero), jnp.where(low, zero, q)], axis=0)

    if mode == "window":
        s0 = sink_ref[pl.ds(2 * j, 1), :]
        s1 = sink_ref[pl.ds(2 * j + 1, 1), :]
        row = lax.broadcasted_iota(I32, (2 * tq, LANES), 0)
        sink = jnp.where(row < tq, s0, s1) * LOG2E
        m0 = sink
    else:
        m0 = jnp.full((2 * tq, LANES), NEG_INF, F32)
    state = (m0, jnp.zeros((2 * tq, LANES), F32), jnp.zeros((2 * tq, LANES), F32))

    def chunk(state, start, valid, size=KV_TILE):
        m_prev, l_prev, acc_prev = state
        k = k_ref[pl.ds(start, size), :]
        v = v_ref[pl.ds(start, size), :]
        s = lax.dot_general(q2, k, (((1,), (1,)), ((), ())), preferred_element_type=F32)
        if valid is not None:
            s = jnp.where(valid, s, NEG_INF)
        m_new = jnp.maximum(m_prev, jnp.max(s, axis=1, keepdims=True))
        alpha = jnp.exp2(m_prev - m_new)
        p = jnp.exp2(s - jnp.concatenate([m_new] * (size // LANES), axis=1))
        part = p[:, :LANES]
        for c in range(1, size // LANES):
            part = part + p[:, c * LANES:(c + 1) * LANES]
        return (m_new, alpha * l_prev + part,
                alpha * acc_prev + jnp.dot(p.astype(BF16), v, preferred_element_type=F32))

    def save(state):
        m_ref[...], l_ref[...], acc_ref[...] = state

    def load():
        return (m_ref[...], l_ref[...], acc_ref[...])

    state = chunk(state, 0, None)
    if mode == "window":
        save(state)

        @pl.when(jnp.logical_not(is_ctx_q))
        def _():
            st = load()
            q0 = (qi - (1 if ctx_queries else 0)) * tq
            kstart = jnp.clip(q0 - WINDOW, 0, seq - 2 * KV_TILE)
            r = lax.broadcasted_iota(I32, (2 * tq, KV_TILE), 0)
            qpos = q0 + jnp.where(r >= tq, r - tq, r)
            col = lax.broadcasted_iota(I32, (2 * tq, KV_TILE), 1)
            for w in range(2):
                kpos = kstart + w * KV_TILE + col
                st = chunk(st, pl.multiple_of(ctx_len + kstart + w * KV_TILE, WINDOW),
                           jnp.abs(qpos - kpos) <= WINDOW)
            save(st)
    elif ctx_queries:
        save(state)

        @pl.when(jnp.logical_not(is_ctx_q))
        def _():
            st = load()
            for c in range(seq // LAT_KV_TILE):
                st = chunk(st, ctx_len + c * LAT_KV_TILE, None, LAT_KV_TILE)
            save(st)
    else:
        for c in range(seq // LAT_KV_TILE):
            state = chunk(state, ctx_len + c * LAT_KV_TILE, None, LAT_KV_TILE)
        save(state)

    l = jnp.sum(l_ref[...], axis=1, keepdims=True)
    if mode == "window":
        l = l + jnp.exp2(sink - m_ref[...])[:, :1]
    o2 = acc_ref[...] / l
    if mode == "diff":
        lp = lam_ref[...]
        lam = (jnp.exp(jnp.sum(lp[0:1] * lp[1:2], axis=1, keepdims=True))
               - jnp.exp(jnp.sum(lp[2:3] * lp[3:4], axis=1, keepdims=True)) + lambda_init)
        o = o2[:tq] - lam * o2[tq:]
        o = _rms(o, sg_ref[...]) * (1.0 - lambda_init)
    else:
        o = jnp.where(low, o2[:tq], o2[tq:])
    o_ref[...] = o.astype(BF16)


def _attention(p, extra, *, mode, batch, seq, ctx_len, n_qblocks, q_col0, k_col, v_col, ctx_queries,
               lambda_init=0.0):
    n_p = ctx_len + seq
    tiles = n_p // Q_TILE
    qt = tiles if ctx_queries else seq // Q_TILE
    q_off = 0 if ctx_queries else ctx_len // Q_TILE
    in_specs = [pl.BlockSpec((Q_TILE, LANES), lambda b, j, qi: (b * tiles + q_off + qi, q_col0 + j)),
                pl.BlockSpec((n_p, LANES), lambda b, j, qi: (b, k_col(j))),
                pl.BlockSpec((n_p, LANES), lambda b, j, qi: (b, v_col(j)))]
    args = [p, p, p]
    for e in extra:
        in_specs.append(pl.BlockSpec(e.shape, lambda b, j, qi: (0, 0)))
        args.append(e)
    kern = functools.partial(_attn_kernel, mode=mode, n_kv=n_p // KV_TILE, seq=seq, ctx_len=ctx_len,
                             ctx_queries=ctx_queries, lambda_init=lambda_init)
    return pl.pallas_call(
        kern,
        out_shape=jax.ShapeDtypeStruct((batch * qt * Q_TILE, n_qblocks * LANES), BF16),
        grid=(batch, n_qblocks, qt),
        in_specs=in_specs,
        out_specs=pl.BlockSpec((Q_TILE, LANES), lambda b, j, qi: (b * qt + qi, j)),
        scratch_shapes=[pltpu.VMEM((2 * Q_TILE, LANES), F32)] * 3,
        compiler_params=_cparams(("arbitrary", "arbitrary", "arbitrary")),
        name="attn_" + mode,
    )(*args)


def _out_kernel(*refs, n_o):
    o_refs = refs[:n_o]
    w_refs = refs[n_o:2 * n_o]
    x_ref, g1_ref, gate_ref, g2_ref, sh_ref, sc_ref, rw_ref, xn_ref, hp_ref, lg_ref = refs[2 * n_o:]
    slab = LANES
    for r0 in range(0, x_ref.shape[0], slab):
        rows = slice(r0, r0 + slab)
        y = jnp.dot(o_refs[0][rows, :], w_refs[0][...], preferred_element_type=F32)
        for a in range(1, n_o):
            y = y + jnp.dot(o_refs[a][rows, :], w_refs[a][...], preferred_element_type=F32)
        xn = x_ref[rows, :] + gate_ref[...] * _rms(y, g1_ref[...])
        xn_ref[rows, :] = xn
        h = _rms(xn, g2_ref[...]) * (1.0 + sc_ref[...]) + sh_ref[...]
        hp_ref[rows, :] = _pack_halves(h)
        lg_ref[:, rows] = lax.dot_general(rw_ref[...], h, (((1,), (1,)), ((), ())), precision=HIGHEST,
                                          preferred_element_type=F32)


def _out_project(os_, ws, x, x_tile, modr, mod_base, row_of_tile, g1, g2, rw_t, n_tiles):
    d = x.shape[1]
    n_o = len(os_)
    tm = n_tiles * ROW_TILE
    mspec = lambda which: pl.BlockSpec((None, 1, d), lambda i: (mod_base + row_of_tile(i) * 6 + which, 0, 0))
    in_specs = ([pl.BlockSpec((ROW_TILE, o.shape[1]), lambda i: (i, 0)) for o in os_]
                + [pl.BlockSpec(w.shape, lambda i: (0, 0)) for w in ws]
                + [pl.BlockSpec((ROW_TILE, d), lambda i: (x_tile(i), 0)),
                   pl.BlockSpec((1, d), lambda i: (0, 0)), mspec(2),
                   pl.BlockSpec((1, d), lambda i: (0, 0)), mspec(3), mspec(4),
                   pl.BlockSpec(rw_t.shape, lambda i: (0, 0))])
    return pl.pallas_call(
        functools.partial(_out_kernel, n_o=n_o),
        out_shape=(jax.ShapeDtypeStruct((tm, d), F32), jax.ShapeDtypeStruct((tm, HALF), U32),
                   jax.ShapeDtypeStruct((N_EXPERTS, tm), F32)),
        grid=(n_tiles,),
        in_specs=in_specs,
        out_specs=(pl.BlockSpec((ROW_TILE, d), lambda i: (i, 0)),
                   pl.BlockSpec((ROW_TILE, HALF), lambda i: (i, 0)),
                   pl.BlockSpec((N_EXPERTS, ROW_TILE), lambda i: (0, i))),
        compiler_params=_cparams(("arbitrary",)),
        name="out_proj",
    )(*os_, *ws, x, g1, modr, g2, modr, modr, rw_t)


def _route_kernel(lg_ref, bias_ref, tri_ref, ltri_ref, ls_ref, w_ref, segn_ref, segc_ref, cnt_ref, carry_ref):
    i = pl.program_id(0)

    @pl.when(i == 0)
    def _():
        carry_ref[...] = jnp.zeros(carry_ref.shape, F32)

    tr = lg_ref.shape[1]
    score = jax.nn.sigmoid(lg_ref[...])
    sel = score + bias_ref[...][:, :1]
    sel_j = [sel[j * GROUP_SIZE:(j + 1) * GROUP_SIZE] for j in range(GROUP_SIZE)]
    sc_j = [score[j * GROUP_SIZE:(j + 1) * GROUP_SIZE] for j in range(GROUP_SIZE)]
    gi = lax.broadcasted_iota(I32, (N_GROUPS, tr), 0)

    m1 = sel_j[0]
    m2 = jnp.full_like(m1, -jnp.inf)
    for j in range(1, GROUP_SIZE):
        m2 = jnp.maximum(m2, jnp.minimum(m1, sel_j[j]))
        m1 = jnp.maximum(m1, sel_j[j])
    gs = m1 + m2

    grank = jnp.zeros((N_GROUPS, tr), I32)
    for gp in range(N_GROUPS):
        rowv = gs[gp:gp + 1, :]
        grank = grank + jnp.where(gi > gp, jnp.where(rowv >= gs, 1, 0), jnp.where(rowv > gs, 1, 0))
    gmask = grank < TOPK_GROUPS
    val_j = [jnp.where(gmask, sel_j[j], NEG_INF) for j in range(GROUP_SIZE)]

    rank_j = [jnp.zeros((N_GROUPS, tr), I32) for _ in range(GROUP_SIZE)]
    for gp in range(N_GROUPS):
        after = gi > gp
        not_before = gi >= gp
        for jp in range(GROUP_SIZE):
            rowv = val_j[jp][gp:gp + 1, :]
            for j in range(GROUP_SIZE):
                tie = after if jp >= j else not_before
                rank_j[j] = rank_j[j] + jnp.where(tie, jnp.where(rowv >= val_j[j], 1, 0),
                                                  jnp.where(rowv > val_j[j], 1, 0))

    chosen = jnp.concatenate([jnp.where(rank_j[j] < TOP_K, 1.0, 0.0) for j in range(GROUP_SIZE)], axis=0)
    n_run = jnp.floor((jnp.sum(chosen, axis=1, keepdims=True) + (SEG_ALIGN - 1.0)) * (1.0 / SEG_ALIGN)) * SEG_ALIGN
    n_run = jnp.broadcast_to(n_run, (N_EXPERTS, LANES))
    run_start = jnp.dot(ltri_ref[...], n_run.astype(BF16), preferred_element_type=F32)
    local = jnp.dot(chosen.astype(BF16), tri_ref[...], preferred_element_type=F32) + run_start[:, :1]
    segn_ref[0] = n_run.astype(I32)
    segc_ref[0] = carry_ref[...].astype(I32)
    carry_ref[...] = carry_ref[...] + n_run
    cnt_ref[...] = carry_ref[...].astype(I32)

    w_rows, ls_rows = [], []
    for k in range(TOP_K):
        w_acc = jnp.zeros((N_GROUPS, tr), F32)
        p_acc = jnp.zeros((N_GROUPS, tr), F32)
        for j in range(GROUP_SIZE):
            hit = rank_j[j] == k
            w_acc = w_acc + jnp.where(hit, sc_j[j], 0.0)
            p_acc = p_acc + jnp.where(hit, local[j * GROUP_SIZE:(j + 1) * GROUP_SIZE], 0.0)
        w_rows.append(jnp.sum(w_acc, axis=0, keepdims=True))
        ls_rows.append(jnp.sum(p_acc, axis=0, keepdims=True))
    w_all = jnp.concatenate(w_rows, axis=0)
    w_ref[...] = w_all / jnp.sum(w_all, axis=0, keepdims=True) * ROUTED_SCALE
    ls_ref[...] = jnp.concatenate(ls_rows, axis=0).astype(I32)


def _route(lg_t, bias_rep):
    tm = lg_t.shape[1]
    n_tiles = tm // MOVE_TILE
    tri = jnp.asarray(np.triu(np.ones((MOVE_TILE, MOVE_TILE), np.float32), 1), BF16)
    ltri = jnp.asarray(np.tril(np.ones((N_EXPERTS, N_EXPERTS), np.float32), -1), BF16)
    tok = lambda i: (0, i)
    per_tile = pl.BlockSpec((1, N_EXPERTS, LANES), lambda i: (i, 0, 0))
    return pl.pallas_call(
        _route_kernel,
        out_shape=(jax.ShapeDtypeStruct((TOP_K, tm), I32), jax.ShapeDtypeStruct((TOP_K, tm), F32),
                   jax.ShapeDtypeStruct((n_tiles, N_EXPERTS, LANES), I32),
                   jax.ShapeDtypeStruct((n_tiles, N_EXPERTS, LANES), I32),
                   jax.ShapeDtypeStruct((N_EXPERTS, LANES), I32)),
        grid=(n_tiles,),
        in_specs=[pl.BlockSpec((N_EXPERTS, MOVE_TILE), tok),
                  pl.BlockSpec((N_EXPERTS, LANES), lambda i: (0, 0)),
                  pl.BlockSpec((MOVE_TILE, MOVE_TILE), lambda i: (0, 0)),
                  pl.BlockSpec((N_EXPERTS, N_EXPERTS), lambda i: (0, 0))],
        out_specs=(pl.BlockSpec((TOP_K, MOVE_TILE), tok), pl.BlockSpec((TOP_K, MOVE_TILE), tok),
                   per_tile, per_tile, pl.BlockSpec((N_EXPERTS, LANES), lambda i: (0, 0))),
        scratch_shapes=[pltpu.VMEM((N_EXPERTS, LANES), F32)],
        compiler_params=_cparams(("arbitrary",)),
        name="route",
    )(lg_t, bias_rep, tri, ltri)


def _start_pieces(n, src, dst, make, pieces):
    for size in pieces:
        above = n & (-2 * size)

        @pl.when((n & size) != 0)
        def _():
            make(pl.multiple_of(src + above, SEG_ALIGN), pl.multiple_of(dst + above, SEG_ALIGN), size).start()


def _start_run_copies(i, segn_ref, segd_ref, make):
    split = RUN_PIECES.index(64)

    def per_run(r, src):
        n = segn_ref[i * N_EXPERTS + r]
        dst = segd_ref[i * N_EXPERTS + r]

        @pl.when(n >= RUN_PIECES[split - 1])
        def _():
            _start_pieces(n, src, dst, make, RUN_PIECES[:split])

        _start_pieces(n, src, dst, make, RUN_PIECES[split:])
        return src + n

    return lax.fori_loop(0, N_EXPERTS, per_run, 0, unroll=2)


def _wait_rows(total, make):
    for size in WAIT_PIECES:
        @pl.when((total & size) != 0)
        def _():
            make(0, 0, size).wait()


def _fill_copies(zstart_ref, zlen_ref, make):
    pieces = tuple(p for p in RUN_PIECES if p < EXPERT_BLOCK)

    def per_expert(r, carry):
        n = zlen_ref[r]
        _start_pieces(n, 0, zstart_ref[r], make, pieces)
        _wait_rows(n, make)
        return carry

    lax.fori_loop(0, N_EXPERTS, per_expert, 0)


def _dispatch_kernel(segn_ref, segd_ref, zstart_ref, zlen_ref, ls_ref, hp_ref, xs_hbm, stage_ref, zero_ref,
                     rows_ref, sem, zsem):
    i = pl.program_id(0)
    buf = i % 2
    lo, hi = _unpack_halves(hp_ref[...])
    lo = lo.astype(BF16)
    hi = hi.astype(BF16)
    ls = ls_ref[...]
    pack_rows = 16
    ls16 = [jnp.tile(jnp.broadcast_to(ls[k:k + 1, :], (pack_rows, MOVE_TILE)).astype(jnp.int16),
                     (STAGE_CHUNK // pack_rows, 1)) for k in range(TOP_K)]
    slot0 = lax.broadcasted_iota(I32, (STAGE_CHUNK, MOVE_TILE), 0).astype(jnp.int16)
    one = jnp.ones((STAGE_CHUNK, MOVE_TILE), BF16)
    for c in range(STAGE_ROWS // STAGE_CHUNK):
        slot = slot0 + jnp.int16(c * STAGE_CHUNK)
        onehot = jnp.zeros((STAGE_CHUNK, MOVE_TILE), BF16)
        for k in range(TOP_K):
            onehot = jnp.where(ls16[k] == slot, one, onehot)
        a = lax.bitcast_convert_type(jnp.dot(onehot, lo, preferred_element_type=F32), U32) >> 16
        b = lax.bitcast_convert_type(jnp.dot(onehot, hi, preferred_element_type=F32), U32)
        stage_ref[buf, c * STAGE_CHUNK:(c + 1) * STAGE_CHUNK, :] = b | a

    def to_slots(which):
        def make(src, dst, size):
            return pltpu.make_async_copy(stage_ref.at[which, pl.ds(src, size), :],
                                         xs_hbm.at[pl.ds(dst, size), :], sem.at[which])
        return make

    rows_ref[buf] = _start_run_copies(i, segn_ref, segd_ref, to_slots(buf))

    @pl.when(i > 0)
    def _():
        _wait_rows(rows_ref[1 - buf], to_slots(1 - buf))

    @pl.when(i == pl.num_programs(0) - 1)
    def _():
        _wait_rows(rows_ref[buf], to_slots(buf))

    @pl.when(i == 0)
    def _():
        zero_ref[...] = jnp.zeros(zero_ref.shape, U32)

        def zeros_to_slots(src, dst, size):
            return pltpu.make_async_copy(zero_ref.at[pl.ds(0, size), :], xs_hbm.at[pl.ds(dst, size), :], zsem)

        _fill_copies(zstart_ref, zlen_ref, zeros_to_slots)


def _dispatch(segn, segd, zstart, zlen, ls, hp, n_slots):
    tm = hp.shape[0]
    tile = lambda i, *_: (i, 0)
    return pl.pallas_call(
        _dispatch_kernel,
        out_shape=jax.ShapeDtypeStruct((n_slots, HALF), U32),
        grid_spec=pltpu.PrefetchScalarGridSpec(
            num_scalar_prefetch=4,
            grid=(tm // MOVE_TILE,),
            in_specs=[pl.BlockSpec((TOP_K, MOVE_TILE), lambda i, *_: (0, i)),
                      pl.BlockSpec((MOVE_TILE, HALF), tile)],
            out_specs=pl.BlockSpec(memory_space=pl.ANY),
            scratch_shapes=[pltpu.VMEM((2, STAGE_ROWS, HALF), U32), pltpu.VMEM((EXPERT_BLOCK // 2, HALF), U32),
                            pltpu.SMEM((2,), I32), pltpu.SemaphoreType.DMA((2,)), pltpu.SemaphoreType.DMA]),
        compiler_params=_cparams(("arbitrary",)),
        name="dispatch",
    )(segn, segd, zstart, zlen, ls, hp)


def _expert_kernel(be_ref, nu_ref, xs_ref, w1_ref, w3_ref, w2_ref, ys_ref, w1b_ref, w3b_ref, w2b_ref):
    i = pl.program_id(0)

    @pl.when(jnp.logical_or(i == 0, be_ref[i] != be_ref[jnp.maximum(i - 1, 0)]))
    def _():
        w1b_ref[...] = w1_ref[0].astype(BF16)
        w3b_ref[...] = w3_ref[0].astype(BF16)
        w2b_ref[...] = w2_ref[0].astype(BF16)

    @pl.when(i < nu_ref[0])
    def _():
        for rows in (slice(r0, r0 + EXPERT_SLAB) for r0 in range(0, EXPERT_BLOCK, EXPERT_SLAB)):
            lo, hi = _unpack_halves(xs_ref[rows, :])
            lo = lo.astype(BF16)
            hi = hi.astype(BF16)
            h1 = (jnp.dot(lo, w1b_ref[:HALF, :], preferred_element_type=F32)
                  + jnp.dot(hi, w1b_ref[HALF:, :], preferred_element_type=F32))
            h3 = (jnp.dot(lo, w3b_ref[:HALF, :], preferred_element_type=F32)
                  + jnp.dot(hi, w3b_ref[HALF:, :], preferred_element_type=F32))
            g = (h1 * jax.nn.sigmoid(h1) * h3).astype(BF16)
            y = jnp.dot(g, w2b_ref[...], preferred_element_type=F32)
            ys_ref[rows, :] = _pack_halves(y)


def _experts(block_e, n_used, xs, w1, w3, w2, layer):
    n_slots = xs.shape[0]
    n_blocks = n_slots // EXPERT_BLOCK
    d, eh = w1.shape[2], w1.shape[3]
    rows = lambda i, be, nu: (jnp.minimum(i, nu[0] - 1), 0)
    return pl.pallas_call(
        _expert_kernel,
        out_shape=jax.ShapeDtypeStruct((n_slots, HALF), U32),
        grid_spec=pltpu.PrefetchScalarGridSpec(
            num_scalar_prefetch=2,
            grid=(n_blocks,),
            in_specs=[pl.BlockSpec((EXPERT_BLOCK, HALF), rows),
                      pl.BlockSpec((None, 1, d, eh), lambda i, be, nu: (layer, be[i], 0, 0)),
                      pl.BlockSpec((None, 1, d, eh), lambda i, be, nu: (layer, be[i], 0, 0)),
                      pl.BlockSpec((None, 1, eh, d), lambda i, be, nu: (layer, be[i], 0, 0))],
            out_specs=pl.BlockSpec((EXPERT_BLOCK, HALF), rows),
            scratch_shapes=[pltpu.VMEM((d, eh), BF16), pltpu.VMEM((d, eh), BF16), pltpu.VMEM((eh, d), BF16)]),
        compiler_params=_cparams(("arbitrary",)),
        name="experts",
    )(block_e, n_used, xs, w1, w3, w2)


def _combine_kernel(segn_ref, segd_ref, ys_hbm, lst_ref, wt_ref, hp_ref, x_ref, s1_ref, s3_ref, s2_ref, g_ref,
                    gate_ref, o_ref, stage_ref, rows_ref, sem):
    i = pl.program_id(0)
    buf = i % 2

    def from_slots(which):
        def make(src, dst, size):
            return pltpu.make_async_copy(ys_hbm.at[pl.ds(dst, size), :],
                                         stage_ref.at[which, pl.ds(src, size), :], sem.at[which])
        return make

    @pl.when(i == 0)
    def _():
        stage_ref[...] = jnp.zeros(stage_ref.shape, U32)
        rows_ref[0] = _start_run_copies(0, segn_ref, segd_ref, from_slots(0))

    @pl.when(i + 1 < pl.num_programs(0))
    def _():
        rows_ref[1 - buf] = _start_run_copies(i + 1, segn_ref, segd_ref, from_slots(1 - buf))

    lo, hi = _unpack_halves(hp_ref[...])
    lo = lo.astype(BF16)
    hi = hi.astype(BF16)
    h1 = (jnp.dot(lo, s1_ref[:HALF, :], preferred_element_type=F32)
          + jnp.dot(hi, s1_ref[HALF:, :], preferred_element_type=F32))
    h3 = (jnp.dot(lo, s3_ref[:HALF, :], preferred_element_type=F32)
          + jnp.dot(hi, s3_ref[HALF:, :], preferred_element_type=F32))
    y = jnp.dot((h1 * jax.nn.sigmoid(h1) * h3).astype(BF16), s2_ref[...], preferred_element_type=F32)

    _wait_rows(rows_ref[buf], from_slots(buf))

    lst = lst_ref[...]
    wt = wt_ref[...]
    lst16 = [jnp.broadcast_to(lst[:, k:k + 1], (MOVE_TILE, STAGE_CHUNK)).astype(jnp.int16) for k in range(TOP_K)]
    wt16 = [jnp.broadcast_to(wt[:, k:k + 1], (MOVE_TILE, STAGE_CHUNK)).astype(BF16) for k in range(TOP_K)]
    slot0 = lax.broadcasted_iota(I32, (MOVE_TILE, STAGE_CHUNK), 1).astype(jnp.int16)
    r_lo = jnp.zeros((MOVE_TILE, HALF), F32)
    r_hi = jnp.zeros((MOVE_TILE, HALF), F32)
    for c in range(STAGE_ROWS // STAGE_CHUNK):
        slot = slot0 + jnp.int16(c * STAGE_CHUNK)
        wsel = jnp.zeros((MOVE_TILE, STAGE_CHUNK), BF16)
        for k in range(TOP_K):
            wsel = jnp.where(lst16[k] == slot, wt16[k], wsel)
        a, b = _unpack_halves(stage_ref[buf, c * STAGE_CHUNK:(c + 1) * STAGE_CHUNK, :])
        r_lo = r_lo + jnp.dot(wsel, a.astype(BF16), preferred_element_type=F32)
        r_hi = r_hi + jnp.dot(wsel, b.astype(BF16), preferred_element_type=F32)
    y = y + jnp.concatenate([r_lo, r_hi], axis=1)
    o_ref[...] = x_ref[...] + gate_ref[...] * _rms(y, g_ref[...])


def _combine(segn, segd, ys, lst, wt, hp, x, s1, s3, s2, g, modr, mod_base, row_of_tile):
    tm, d = x.shape
    full = lambda a: pl.BlockSpec(a.shape, lambda i, *_: (0, 0))
    tile = lambda i, *_: (i, 0)
    return pl.pallas_call(
        _combine_kernel,
        out_shape=jax.ShapeDtypeStruct((tm, d), F32),
        grid_spec=pltpu.PrefetchScalarGridSpec(
            num_scalar_prefetch=2,
            grid=(tm // MOVE_TILE,),
            in_specs=[pl.BlockSpec(memory_space=pl.ANY),
                      pl.BlockSpec((MOVE_TILE, TOP_K), tile), pl.BlockSpec((MOVE_TILE, TOP_K), tile),
                      pl.BlockSpec((MOVE_TILE, HALF), tile), pl.BlockSpec((MOVE_TILE, d), tile),
                      full(s1), full(s3), full(s2), full(g),
                      pl.BlockSpec((None, 1, d), lambda i, *_: (mod_base + row_of_tile(i) * 6 + 5, 0, 0))],
            out_specs=pl.BlockSpec((MOVE_TILE, d), tile),
            scratch_shapes=[pltpu.VMEM((2, STAGE_ROWS, HALF), U32), pltpu.SMEM((2,), I32),
                            pltpu.SemaphoreType.DMA((2,))]),
        compiler_params=_cparams(("arbitrary",)),
        name="combine",
    )(segn, segd, ys, lst, wt, hp, x, s1, s3, s2, g, modr)


def _moe(xn, hp, lg_t, bias, expert_weights, s1, s3, s2, g, modr, mod_base, row_of_tile):
    tm = xn.shape[0]
    n_tiles = tm // MOVE_TILE
    perm = np.array([(r % GROUP_SIZE) * GROUP_SIZE + r // GROUP_SIZE for r in range(N_EXPERTS)])
    bias_rep = jnp.broadcast_to(bias.astype(F32)[perm][:, None], (N_EXPERTS, LANES))
    ls, w, segn, segc, cnt = _route(lg_t, bias_rep)

    total = cnt[:, 0]
    padded = (total + EXPERT_BLOCK - 1) // EXPERT_BLOCK * EXPERT_BLOCK
    pad_end = jnp.cumsum(padded)
    pad_start = pad_end - padded
    n_slots = tm * TOP_K + (SEG_ALIGN - 1) * N_EXPERTS * n_tiles + N_EXPERTS * EXPERT_BLOCK
    n_blocks = -(-n_slots // EXPERT_BLOCK)
    starts = jnp.arange(n_blocks, dtype=I32) * EXPERT_BLOCK
    region = jnp.minimum(jnp.sum((pad_end[None, :] <= starts[:, None]).astype(I32), axis=1), N_EXPERTS - 1)
    block_e = jnp.asarray(perm, I32)[region]
    n_used = (pad_end[-1:] // EXPERT_BLOCK).astype(I32)
    segn = segn[:, :, 0].reshape(-1)
    segd = (segc[:, :, 0] + pad_start[None, :]).reshape(-1).astype(I32)

    xs = _dispatch(segn, segd, (pad_start + total).astype(I32), (padded - total).astype(I32), ls, hp,
                   n_blocks * EXPERT_BLOCK)
    ys = _experts(block_e, n_used, xs, *expert_weights)
    return _combine(segn, segd, ys, ls.T, w.T, hp, xn, s1.astype(BF16), s3.astype(BF16), s2.astype(BF16), g, modr,
                    mod_base, row_of_tile)


_DEINTERLEAVE = np.concatenate([np.arange(0, HEAD_DIM, 2), np.arange(1, HEAD_DIM, 2)])


def _rope_tables(seq, ctx_len):
    t = np.arange(seq)
    n_pair = HEAD_DIM // 4
    inv = jnp.asarray(ROPE_THETA, F32) ** (-jnp.arange(n_pair, dtype=F32) / n_pair)
    r = jnp.asarray(t // GRID_W, F32)
    c = jnp.asarray(t % GRID_W, F32)
    ang = jnp.concatenate([r[:, None] * inv, c[:, None] * inv], axis=-1)
    cos, sin = jnp.cos(ang), jnp.sin(ang)
    cos_t = jnp.tile(jnp.concatenate([cos, cos], axis=-1), (1, LANES // HEAD_DIM))
    sin_t = jnp.tile(jnp.concatenate([-sin, sin], axis=-1), (1, LANES // HEAD_DIM))
    cos_t = jnp.concatenate([jnp.ones((ctx_len, LANES), F32), cos_t], axis=0)
    sin_t = jnp.concatenate([jnp.zeros((ctx_len, LANES), F32), sin_t], axis=0)
    return cos_t, sin_t


def _ab_layout():
    cols, blocks, gain_kind = [], [], []
    for base_q, base_k, base_v, normed in ((0, 512, 640, True), (768, 1280, 1408, False)):
        for jb in range(4):
            cols += [base_q + h * HEAD_DIM + _DEINTERLEAVE for h in (2 * jb, 2 * jb + 1)]
            blocks.append((normed, True, Q_SCALE))
            gain_kind.append("q" if normed else None)
        for kvh in range(2):
            cols += [base_k + kvh * HEAD_DIM + _DEINTERLEAVE] * 2
            blocks.append((normed, True, 1.0))
            gain_kind.append("k" if normed else None)
        for kvh in range(2):
            cols += [base_v + kvh * HEAD_DIM + np.arange(HEAD_DIM)] * 2
            blocks.append((False, False, 1.0))
            gain_kind.append(None)
    return np.concatenate(cols), tuple(blocks), gain_kind


def _c_layout():
    cols, blocks = [], []
    for base, rope, scale in ((0, True, Q_SCALE), (1024, True, 1.0)):
        for h in range(8):
            cols += [base + (2 * h + m) * HEAD_DIM + _DEINTERLEAVE for m in range(2)]
            blocks.append((False, rope, scale))
    for h in range(8):
        cols.append(2048 + h * LANES + np.arange(LANES))
        blocks.append((False, False, 1.0))
    return np.concatenate(cols), tuple(blocks)


def kernel(x, c, ctx, c_ctx, ada_w, ada_b, norm_g, ab_w_in, ab_w_out, a_q_norm, a_k_norm, b_sink,
           c_w_in, c_w_out, c_lambda, c_subln_g, router_w, router_bias, exp_w1, exp_w3, exp_w2,
           sh_w1, sh_w3, sh_w2):
    batch, seq, d = x.shape
    ctx_len = ctx.shape[1]
    depth = ada_w.shape[0]
    assert d == D_MODEL and ctx_len == ROW_TILE and depth == 2
    assert seq % Q_TILE == 0 and seq >= 2 * KV_TILE and batch <= 4
    n_p = ctx_len + seq
    tiles = n_p // ROW_TILE
    lat_tiles = seq // ROW_TILE
    t_all = batch * n_p

    cond = jnp.zeros((8, d), F32).at[:batch].set(c).at[4].set(c_ctx)
    modr = _modulation(cond, ada_w, ada_b).reshape(depth * 8 * 6, 1, d)
    row_all = lambda i: jnp.where(i % tiles == 0, 4, i // tiles)
    cos_t, sin_t = _rope_tables(seq, ctx_len)
    rperm = np.array([(r % GROUP_SIZE) * GROUP_SIZE + r // GROUP_SIZE for r in range(N_EXPERTS)])
    xt = jnp.concatenate([ctx, x], axis=1).reshape(t_all, d)

    cols, blocks, gain_kind = _ab_layout()
    w0 = ab_w_in[0][:, cols].astype(BF16)
    gq = jnp.tile(a_q_norm[0][_DEINTERLEAVE], 2)
    gk = jnp.tile(a_k_norm[0][_DEINTERLEAVE], 2)
    ones = jnp.ones((LANES,), F32)
    head_gain = jnp.concatenate([{"q": gq, "k": gk, None: ones}[kind] for kind in gain_kind])[None, :]
    p0 = _project(xt, modr, 0, row_all, norm_g[0, 0][None, :], w0, cos_t, sin_t, head_gain, blocks, tiles)
    common = dict(batch=batch, seq=seq, ctx_len=ctx_len, n_qblocks=4, ctx_queries=True)
    oa = _attention(p0, [], mode="dense", q_col0=0, k_col=lambda j: 4 + j // 2, v_col=lambda j: 6 + j // 2,
                    **common)
    sink = jnp.broadcast_to(b_sink[0].astype(F32)[:, None], (8, LANES))
    ob = _attention(p0, [sink], mode="window", q_col0=8, k_col=lambda j: 12 + j // 2,
                    v_col=lambda j: 14 + j // 2, **common)
    w_out = ab_w_out[0].astype(BF16)
    x1, hp, lg = _out_project([oa, ob], [w_out[:512], w_out[512:]], xt, lambda i: i, modr, 0, row_all,
                              norm_g[0, 1][None, :], norm_g[0, 2][None, :], router_w[0].T[rperm],
                              t_all // ROW_TILE)
    x2 = _moe(x1, hp, lg, router_bias[0], (exp_w1, exp_w3, exp_w2, 0), sh_w1[0], sh_w3[0], sh_w2[0],
              norm_g[0, 3][None, :], modr, 0, lambda i: row_all(i // (ROW_TILE // MOVE_TILE)))

    base1 = 8 * 6
    lambda_init = 0.8 - 0.6 * math.exp(-0.3 * 1)
    cols1, blocks1 = _c_layout()
    w1p = c_w_in[0][:, cols1].astype(BF16)
    p1 = _project(x2, modr, base1, row_all, norm_g[1, 0][None, :], w1p, cos_t, sin_t,
                  jnp.ones((1, w1p.shape[1]), F32), blocks1, tiles)
    lam = jnp.zeros((8, LANES), F32).at[:4, :HEAD_DIM].set(c_lambda[0].astype(F32))
    oc = _attention(p1, [lam, c_subln_g[0][None, :]], mode="diff", batch=batch, seq=seq, ctx_len=ctx_len,
                    n_qblocks=8, q_col0=0, k_col=lambda j: 8 + j, v_col=lambda j: 16 + j, ctx_queries=False,
                    lambda_init=lambda_init)
    row_lat = lambda i: i // lat_tiles
    x3, hp1, lg1 = _out_project([oc], [c_w_out[0].astype(BF16)], x2,
                                lambda i: (i // lat_tiles) * tiles + 1 + i % lat_tiles, modr, base1, row_lat,
                                norm_g[1, 1][None, :], norm_g[1, 2][None, :], router_w[1].T[rperm],
                                batch * lat_tiles)
    out = _moe(x3, hp1, lg1, router_bias[1], (exp_w1, exp_w3, exp_w2, 1), sh_w1[1], sh_w3[1], sh_w2[1],
               norm_g[1, 3][None, :], modr, base1, lambda i: row_lat(i // (ROW_TILE // MOVE_TILE)))
    return out.reshape(batch, seq, d)
```

```python
import functools
import math

import numpy as np
import jax
import jax.numpy as jnp
from jax import lax
from jax.experimental import pallas as pl
from jax.experimental.pallas import tpu as pltpu

F32 = jnp.float32
BF16 = jnp.bfloat16
U32 = jnp.uint32
I32 = jnp.int32
HIGHEST = lax.Precision.HIGHEST

D_MODEL = 1024
HEAD_DIM = 64
LANES = 128
SUBLANES = 8
GRID_W = 64
ROPE_THETA = 10000.0
EPS = 1e-6
NEG_INF = -1e30
WINDOW = 128
N_EXPERTS = 64
TOP_K = 8
N_GROUPS = 8
TOPK_GROUPS = 4
GROUP_SIZE = N_EXPERTS // N_GROUPS
ROUTED_SCALE = 2.5
LOG2E = 1.4426950408889634
Q_SCALE = HEAD_DIM ** -0.5 * LOG2E

ROW_TILE = 256
Q_TILE = 256
KV_TILE = 256
LAT_KV_TILE = 256
MOVE_TILE = 256
EXPERT_BLOCK = 512
EXPERT_SLAB = 256
SEG_ALIGN = 8
RUN_PIECES = (256, 128, 64, 32, 16, 8)
WAIT_PIECES = (2048, 1024, 512) + RUN_PIECES
STAGE_ROWS = 2560
STAGE_CHUNK = 512
HALF = D_MODEL // 2
VMEM_LIMIT = 48 * 1024 * 1024


def _cparams(sem):
    return pltpu.CompilerParams(dimension_semantics=sem, vmem_limit_bytes=VMEM_LIMIT)


def _rms(x, g):
    ms = jnp.mean(x * x, axis=-1, keepdims=True)
    return x * lax.rsqrt(ms + EPS) * g


def _pack_halves(h):
    lo = lax.bitcast_convert_type(h[:, :HALF].astype(BF16).astype(F32), U32) >> 16
    hi = lax.bitcast_convert_type(h[:, HALF:].astype(BF16).astype(F32), U32) & jnp.uint32(0xFFFF0000)
    return hi | lo


def _unpack_halves(u):
    lo = lax.bitcast_convert_type(u << 16, F32)
    hi = lax.bitcast_convert_type(u & jnp.uint32(0xFFFF0000), F32)
    return lo, hi


def _mod_kernel(c_ref, w_ref, b_ref, o_ref):
    c = c_ref[...]
    sc = c * jax.nn.sigmoid(c)
    o_ref[0] = jnp.dot(sc, w_ref[0], precision=HIGHEST, preferred_element_type=F32) + b_ref[0]


def _modulation(cond, ada_w, ada_b):
    depth, d, n = ada_w.shape
    nt = 1536
    return pl.pallas_call(
        _mod_kernel,
        out_shape=jax.ShapeDtypeStruct((depth, 8, n), F32),
        grid=(depth, n // nt),
        in_specs=[pl.BlockSpec((8, d), lambda l, j: (0, 0)),
                  pl.BlockSpec((1, d, nt), lambda l, j: (l, 0, j)),
                  pl.BlockSpec((1, 1, nt), lambda l, j: (l, 0, j))],
        out_specs=pl.BlockSpec((1, 8, nt), lambda l, j: (l, 0, j)),
        compiler_params=_cparams(("arbitrary", "arbitrary")),
        name="ada_mod",
    )(cond, ada_w, ada_b.reshape(depth, 1, n))


def _proj_kernel(x_ref, sh_ref, sc_ref, g_ref, w_ref, cos_ref, sin_ref, hg_ref, gm_ref, o_ref, *, blocks):
    h = _rms(x_ref[...], g_ref[...]) * (1.0 + sc_ref[...]) + sh_ref[...]
    y = jnp.dot(h.astype(BF16), w_ref[...], preferred_element_type=F32)
    lane = lax.broadcasted_iota(I32, (x_ref.shape[0], LANES), 1)
    first_half = (lane % HEAD_DIM) < (HEAD_DIM // 2)
    for jb, (norm, rope, scale) in enumerate(blocks):
        cols = slice(jb * LANES, (jb + 1) * LANES)
        yb = y[:, cols]
        if norm:
            ms = jnp.dot(yb * yb, gm_ref[...], precision=HIGHEST, preferred_element_type=F32)
            yb = yb * lax.rsqrt(ms + EPS) * hg_ref[:, cols]
        if rope:
            swapped = jnp.where(first_half, pltpu.roll(yb, LANES - HEAD_DIM // 2, 1),
                                pltpu.roll(yb, HEAD_DIM // 2, 1))
            yb = yb * cos_ref[...] + swapped * sin_ref[...]
        if scale != 1.0:
            yb = yb * scale
        o_ref[:, cols] = yb.astype(BF16)


def _project(x, modr, mod_base, row_of_tile, g, w, cos_t, sin_t, head_gain, blocks, tiles_per_batch):
    t, d = x.shape
    n = w.shape[1]
    group_mean = jnp.asarray(np.kron(np.eye(LANES // HEAD_DIM), np.full((HEAD_DIM, HEAD_DIM), 1.0 / HEAD_DIM)), F32)
    return pl.pallas_call(
        functools.partial(_proj_kernel, blocks=blocks),
        out_shape=jax.ShapeDtypeStruct((t, n), BF16),
        grid=(t // ROW_TILE,),
        in_specs=[pl.BlockSpec((ROW_TILE, d), lambda i: (i, 0)),
                  pl.BlockSpec((None, 1, d), lambda i: (mod_base + row_of_tile(i) * 6 + 0, 0, 0)),
                  pl.BlockSpec((None, 1, d), lambda i: (mod_base + row_of_tile(i) * 6 + 1, 0, 0)),
                  pl.BlockSpec((1, d), lambda i: (0, 0)),
                  pl.BlockSpec((d, n), lambda i: (0, 0)),
                  pl.BlockSpec((ROW_TILE, LANES), lambda i: (i % tiles_per_batch, 0)),
                  pl.BlockSpec((ROW_TILE, LANES), lambda i: (i % tiles_per_batch, 0)),
                  pl.BlockSpec((1, n), lambda i: (0, 0)),
                  pl.BlockSpec((LANES, LANES), lambda i: (0, 0))],
        out_specs=pl.BlockSpec((ROW_TILE, n), lambda i: (i, 0)),
        compiler_params=_cparams(("arbitrary",)),
        name="prenorm_proj",
    )(x, modr, modr, g, w, cos_t, sin_t, head_gain, group_mean)


def _attn_kernel(*refs, mode, n_kv, seq, ctx_len, ctx_queries, lambda_init):
    if mode == "window":
        q_ref, k_ref, v_ref, sink_ref, o_ref, m_ref, l_ref, acc_ref = refs
    elif mode == "diff":
        q_ref, k_ref, v_ref, lam_ref, sg_ref, o_ref, m_ref, l_ref, acc_ref = refs
    else:
        q_ref, k_ref, v_ref, o_ref, m_ref, l_ref, acc_ref = refs
    tq = q_ref.shape[0]
    j = pl.program_id(1)
    qi = pl.program_id(2)
    is_ctx_q = (qi == 0) if ctx_queries else False

    q = q_ref[...]
    lane = lax.broadcasted_iota(I32, (tq, LANES), 1)
    low = lane < HEAD_DIM
    zero = jnp.zeros_like(q)
    q2 = jnp.concatenate([jnp.where(low, q, zero), jnp.where(low, zero, q)], axis=0)

    if mode == "window":
        s0 = sink_ref[pl.ds(2 * j, 1), :]
        s1 = sink_ref[pl.ds(2 * j + 1, 1), :]
        row = lax.broadcasted_iota(I32, (2 * tq, LANES), 0)
        sink = jnp.where(row < tq, s0, s1) * LOG2E
        m0 = sink
    else:
        m0 = jnp.full((2 * tq, LANES), NEG_INF, F32)
    state = (m0, jnp.zeros((2 * tq, LANES), F32), jnp.zeros((2 * tq, LANES), F32))

    def chunk(state, start, valid, size=KV_TILE):
        m_prev, l_prev, acc_prev = state
        k = k_ref[pl.ds(start, size), :]
        v = v_ref[pl.ds(start, size), :]
        s = lax.dot_general(q2, k, (((1,), (1,)), ((), ())), preferred_element_type=F32)
        if valid is not None:
            s = jnp.where(valid, s, NEG_INF)
        m_new = jnp.maximum(m_prev, jnp.max(s, axis=1, keepdims=True))
        alpha = jnp.exp2(m_prev - m_new)
        p = jnp.exp2(s - jnp.concatenate([m_new] * (size // LANES), axis=1))
        part = p[:, :LANES]
        for c in range(1, size // LANES):
            part = part + p[:, c * LANES:(c + 1) * LANES]
        return (m_new, alpha * l_prev + part,
                alpha * acc_prev + jnp.dot(p.astype(BF16), v, preferred_element_type=F32))

    def save(state):
        m_ref[...], l_ref[...], acc_ref[...] = state

    def load():
        return (m_ref[...], l_ref[...], acc_ref[...])

    state = chunk(state, 0, None)
    if mode == "window":
        save(state)

        @pl.when(jnp.logical_not(is_ctx_q))
        def _():
            st = load()
            q0 = (qi - (1 if ctx_queries else 0)) * tq
            kstart = jnp.clip(q0 - WINDOW, 0, seq - 2 * KV_TILE)
            r = lax.broadcasted_iota(I32, (2 * tq, KV_TILE), 0)
            qpos = q0 + jnp.where(r >= tq, r - tq, r)
            col = lax.broadcasted_iota(I32, (2 * tq, KV_TILE), 1)
            for w in range(2):
                kpos = kstart + w * KV_TILE + col
                st = chunk(st, pl.multiple_of(ctx_len + kstart + w * KV_TILE, WINDOW),
                           jnp.abs(qpos - kpos) <= WINDOW)
            save(st)
    elif ctx_queries:
        save(state)

        @pl.when(jnp.logical_not(is_ctx_q))
        def _():
            st = load()
            for c in range(seq // LAT_KV_TILE):
                st = chunk(st, ctx_len + c * LAT_KV_TILE, None, LAT_KV_TILE)
            save(st)
    else:
        for c in range(seq // LAT_KV_TILE):
            state = chunk(state, ctx_len + c * LAT_KV_TILE, None, LAT_KV_TILE)
        save(state)

    l = jnp.sum(l_ref[...], axis=1, keepdims=True)
    if mode == "window":
        l = l + jnp.exp2(sink - m_ref[...])[:, :1]
    o2 = acc_ref[...] / l
    if mode == "diff":
        lp = lam_ref[...]
        lam = (jnp.exp(jnp.sum(lp[0:1] * lp[1:2], axis=1, keepdims=True))
               - jnp.exp(jnp.sum(lp[2:3] * lp[3:4], axis=1, keepdims=True)) + lambda_init)
        o = o2[:tq] - lam * o2[tq:]
        o = _rms(o, sg_ref[...]) * (1.0 - lambda_init)
    else:
        o = jnp.where(low, o2[:tq], o2[tq:])
    o_ref[...] = o.astype(BF16)


def _attention(p, extra, *, mode, batch, seq, ctx_len, n_qblocks, q_col0, k_col, v_col, ctx_queries,
               lambda_init=0.0):
    n_p = ctx_len + seq
    tiles = n_p // Q_TILE
    qt = tiles if ctx_queries else seq // Q_TILE
    q_off = 0 if ctx_queries else ctx_len // Q_TILE
    in_specs = [pl.BlockSpec((Q_TILE, LANES), lambda b, j, qi: (b * tiles + q_off + qi, q_col0 + j)),
                pl.BlockSpec((n_p, LANES), lambda b, j, qi: (b, k_col(j))),
                pl.BlockSpec((n_p, LANES), lambda b, j, qi: (b, v_col(j)))]
    args = [p, p, p]
    for e in extra:
        in_specs.append(pl.BlockSpec(e.shape, lambda b, j, qi: (0, 0)))
        args.append(e)
    kern = functools.partial(_attn_kernel, mode=mode, n_kv=n_p // KV_TILE, seq=seq, ctx_len=ctx_len,
                             ctx_queries=ctx_queries, lambda_init=lambda_init)
    return pl.pallas_call(
        kern,
        out_shape=jax.ShapeDtypeStruct((batch * qt * Q_TILE, n_qblocks * LANES), BF16),
        grid=(batch, n_qblocks, qt),
        in_specs=in_specs,
        out_specs=pl.BlockSpec((Q_TILE, LANES), lambda b, j, qi: (b * qt + qi, j)),
        scratch_shapes=[pltpu.VMEM((2 * Q_TILE, LANES), F32)] * 3,
        compiler_params=_cparams(("arbitrary", "arbitrary", "arbitrary")),
        name="attn_" + mode,
    )(*args)


def _out_kernel(*refs, n_o):
    o_refs = refs[:n_o]
    w_refs = refs[n_o:2 * n_o]
    x_ref, g1_ref, gate_ref, g2_ref, sh_ref, sc_ref, rw_ref, xn_ref, hp_ref, lg_ref = refs[2 * n_o:]
    slab = LANES
    for r0 in range(0, x_ref.shape[0], slab):
        rows = slice(r0, r0 + slab)
        y = jnp.dot(o_refs[0][rows, :], w_refs[0][...], preferred_element_type=F32)
        for a in range(1, n_o):
            y = y + jnp.dot(o_refs[a][rows, :], w_refs[a][...], preferred_element_type=F32)
        xn = x_ref[rows, :] + gate_ref[...] * _rms(y, g1_ref[...])
        xn_ref[rows, :] = xn
        h = _rms(xn, g2_ref[...]) * (1.0 + sc_ref[...]) + sh_ref[...]
        hp_ref[rows, :] = _pack_halves(h)
        lg_ref[:, rows] = lax.dot_general(rw_ref[...], h, (((1,), (1,)), ((), ())), precision=HIGHEST,
                                          preferred_element_type=F32)


def _out_project(os_, ws, x, x_tile, modr, mod_base, row_of_tile, g1, g2, rw_t, n_tiles):
    d = x.shape[1]
    n_o = len(os_)
    tm = n_tiles * ROW_TILE
    mspec = lambda which: pl.BlockSpec((None, 1, d), lambda i: (mod_base + row_of_tile(i) * 6 + which, 0, 0))
    in_specs = ([pl.BlockSpec((ROW_TILE, o.shape[1]), lambda i: (i, 0)) for o in os_]
                + [pl.BlockSpec(w.shape, lambda i: (0, 0)) for w in ws]
                + [pl.BlockSpec((ROW_TILE, d), lambda i: (x_tile(i), 0)),
                   pl.BlockSpec((1, d), lambda i: (0, 0)), mspec(2),
                   pl.BlockSpec((1, d), lambda i: (0, 0)), mspec(3), mspec(4),
                   pl.BlockSpec(rw_t.shape, lambda i: (0, 0))])
    return pl.pallas_call(
        functools.partial(_out_kernel, n_o=n_o),
        out_shape=(jax.ShapeDtypeStruct((tm, d), F32), jax.ShapeDtypeStruct((tm, HALF), U32),
                   jax.ShapeDtypeStruct((N_EXPERTS, tm), F32)),
        grid=(n_tiles,),
        in_specs=in_specs,
        out_specs=(pl.BlockSpec((ROW_TILE, d), lambda i: (i, 0)),
                   pl.BlockSpec((ROW_TILE, HALF), lambda i: (i, 0)),
                   pl.BlockSpec((N_EXPERTS, ROW_TILE), lambda i: (0, i))),
        compiler_params=_cparams(("arbitrary",)),
        name="out_proj",
    )(*os_, *ws, x, g1, modr, g2, modr, modr, rw_t)


def _route_kernel(lg_ref, bias_ref, tri_ref, ltri_ref, ls_ref, w_ref, segn_ref, segc_ref, cnt_ref, carry_ref):
    i = pl.program_id(0)

    @pl.when(i == 0)
    def _():
        carry_ref[...] = jnp.zeros(carry_ref.shape, F32)

    tr = lg_ref.shape[1]
    score = jax.nn.sigmoid(lg_ref[...])
    sel = score + bias_ref[...][:, :1]
    sel_j = [sel[j * GROUP_SIZE:(j + 1) * GROUP_SIZE] for j in range(GROUP_SIZE)]
    sc_j = [score[j * GROUP_SIZE:(j + 1) * GROUP_SIZE] for j in range(GROUP_SIZE)]
    gi = lax.broadcasted_iota(I32, (N_GROUPS, tr), 0)

    m1 = sel_j[0]
    m2 = jnp.full_like(m1, -jnp.inf)
    for j in range(1, GROUP_SIZE):
        m2 = jnp.maximum(m2, jnp.minimum(m1, sel_j[j]))
        m1 = jnp.maximum(m1, sel_j[j])
    gs = m1 + m2

    grank = jnp.zeros((N_GROUPS, tr), I32)
    for gp in range(N_GROUPS):
        rowv = gs[gp:gp + 1, :]
        grank = grank + jnp.where(gi > gp, jnp.where(rowv >= gs, 1, 0), jnp.where(rowv > gs, 1, 0))
    gmask = grank < TOPK_GROUPS
    val_j = [jnp.where(gmask, sel_j[j], NEG_INF) for j in range(GROUP_SIZE)]

    rank_j = [jnp.zeros((N_GROUPS, tr), I32) for _ in range(GROUP_SIZE)]
    for gp in range(N_GROUPS):
        after = gi > gp
        not_before = gi >= gp
        for jp in range(GROUP_SIZE):
            rowv = val_j[jp][gp:gp + 1, :]
            for j in range(GROUP_SIZE):
                tie = after if jp >= j else not_before
                rank_j[j] = rank_j[j] + jnp.where(tie, jnp.where(rowv >= val_j[j], 1, 0),
                                                  jnp.where(rowv > val_j[j], 1, 0))

    chosen = jnp.concatenate([jnp.where(rank_j[j] < TOP_K, 1.0, 0.0) for j in range(GROUP_SIZE)], axis=0)
    n_run = jnp.floor((jnp.sum(chosen, axis=1, keepdims=True) + (SEG_ALIGN - 1.0)) * (1.0 / SEG_ALIGN)) * SEG_ALIGN
    n_run = jnp.broadcast_to(n_run, (N_EXPERTS, LANES))
    run_start = jnp.dot(ltri_ref[...], n_run.astype(BF16), preferred_element_type=F32)
    local = jnp.dot(chosen.astype(BF16), tri_ref[...], preferred_element_type=F32) + run_start[:, :1]
    segn_ref[0] = n_run.astype(I32)
    segc_ref[0] = carry_ref[...].astype(I32)
    carry_ref[...] = carry_ref[...] + n_run
    cnt_ref[...] = carry_ref[...].astype(I32)

    w_rows, ls_rows = [], []
    for k in range(TOP_K):
        w_acc = jnp.zeros((N_GROUPS, tr), F32)
        p_acc = jnp.zeros((N_GROUPS, tr), F32)
        for j in range(GROUP_SIZE):
            hit = rank_j[j] == k
            w_acc = w_acc + jnp.where(hit, sc_j[j], 0.0)
            p_acc = p_acc + jnp.where(hit, local[j * GROUP_SIZE:(j + 1) * GROUP_SIZE], 0.0)
        w_rows.append(jnp.sum(w_acc, axis=0, keepdims=True))
        ls_rows.append(jnp.sum(p_acc, axis=0, keepdims=True))
    w_all = jnp.concatenate(w_rows, axis=0)
    w_ref[...] = w_all / jnp.sum(w_all, axis=0, keepdims=True) * ROUTED_SCALE
    ls_ref[...] = jnp.concatenate(ls_rows, axis=0).astype(I32)


def _route(lg_t, bias_rep):
    tm = lg_t.shape[1]
    n_tiles = tm // MOVE_TILE
    tri = jnp.asarray(np.triu(np.ones((MOVE_TILE, MOVE_TILE), np.float32), 1), BF16)
    ltri = jnp.asarray(np.tril(np.ones((N_EXPERTS, N_EXPERTS), np.float32), -1), BF16)
    tok = lambda i: (0, i)
    per_tile = pl.BlockSpec((1, N_EXPERTS, LANES), lambda i: (i, 0, 0))
    return pl.pallas_call(
        _route_kernel,
        out_shape=(jax.ShapeDtypeStruct((TOP_K, tm), I32), jax.ShapeDtypeStruct((TOP_K, tm), F32),
                   jax.ShapeDtypeStruct((n_tiles, N_EXPERTS, LANES), I32),
                   jax.ShapeDtypeStruct((n_tiles, N_EXPERTS, LANES), I32),
                   jax.ShapeDtypeStruct((N_EXPERTS, LANES), I32)),
        grid=(n_tiles,),
        in_specs=[pl.BlockSpec((N_EXPERTS, MOVE_TILE), tok),
                  pl.BlockSpec((N_EXPERTS, LANES), lambda i: (0, 0)),
                  pl.BlockSpec((MOVE_TILE, MOVE_TILE), lambda i: (0, 0)),
                  pl.BlockSpec((N_EXPERTS, N_EXPERTS), lambda i: (0, 0))],
        out_specs=(pl.BlockSpec((TOP_K, MOVE_TILE), tok), pl.BlockSpec((TOP_K, MOVE_TILE), tok),
                   per_tile, per_tile, pl.BlockSpec((N_EXPERTS, LANES), lambda i: (0, 0))),
        scratch_shapes=[pltpu.VMEM((N_EXPERTS, LANES), F32)],
        compiler_params=_cparams(("arbitrary",)),
        name="route",
    )(lg_t, bias_rep, tri, ltri)


def _start_pieces(n, src, dst, make, pieces):
    for size in pieces:
        above = n & (-2 * size)

        @pl.when((n & size) != 0)
        def _():
            make(pl.multiple_of(src + above, SEG_ALIGN), pl.multiple_of(dst + above, SEG_ALIGN), size).start()


def _start_run_copies(i, segn_ref, segd_ref, make, straight_line=False):
    split = RUN_PIECES.index(64)

    def per_run(r, src):
        n = segn_ref[i * N_EXPERTS + r]
        dst = segd_ref[i * N_EXPERTS + r]

        @pl.when(n >= RUN_PIECES[split - 1])
        def _():
            _start_pieces(n, src, dst, make, RUN_PIECES[:split])

        _start_pieces(n, src, dst, make, RUN_PIECES[split:])
        return src + n

    if not straight_line:
        return lax.fori_loop(0, N_EXPERTS, per_run, 0, unroll=2)
    src = 0
    for r in range(N_EXPERTS):
        src = per_run(r, src)
    return src


def _wait_rows(total, make):
    for size in WAIT_PIECES:
        @pl.when((total & size) != 0)
        def _():
            make(0, 0, size).wait()


def _fill_copies(zstart_ref, zlen_ref, make):
    pieces = tuple(p for p in RUN_PIECES if p < EXPERT_BLOCK)

    def per_expert(r, carry):
        n = zlen_ref[r]
        _start_pieces(n, 0, zstart_ref[r], make, pieces)
        _wait_rows(n, make)
        return carry

    lax.fori_loop(0, N_EXPERTS, per_expert, 0)


def _dispatch_kernel(segn_ref, segd_ref, zstart_ref, zlen_ref, ls_ref, hp_ref, xs_hbm, stage_ref, zero_ref,
                     rows_ref, sem, zsem):
    i = pl.program_id(0)
    buf = i % 2
    lo, hi = _unpack_halves(hp_ref[...])
    lo = lo.astype(BF16)
    hi = hi.astype(BF16)
    ls = ls_ref[...]
    pack_rows = 16
    ls16 = [jnp.tile(jnp.broadcast_to(ls[k:k + 1, :], (pack_rows, MOVE_TILE)).astype(jnp.int16),
                     (STAGE_CHUNK // pack_rows, 1)) for k in range(TOP_K)]
    slot0 = lax.broadcasted_iota(I32, (STAGE_CHUNK, MOVE_TILE), 0).astype(jnp.int16)
    one = jnp.ones((STAGE_CHUNK, MOVE_TILE), BF16)
    for c in range(STAGE_ROWS // STAGE_CHUNK):
        slot = slot0 + jnp.int16(c * STAGE_CHUNK)
        onehot = jnp.zeros((STAGE_CHUNK, MOVE_TILE), BF16)
        for k in range(TOP_K):
            onehot = jnp.where(ls16[k] == slot, one, onehot)
        a = lax.bitcast_convert_type(jnp.dot(onehot, lo, preferred_element_type=F32), U32) >> 16
        b = lax.bitcast_convert_type(jnp.dot(onehot, hi, preferred_element_type=F32), U32)
        stage_ref[buf, c * STAGE_CHUNK:(c + 1) * STAGE_CHUNK, :] = b | a

    def to_slots(which):
        def make(src, dst, size):
            return pltpu.make_async_copy(stage_ref.at[which, pl.ds(src, size), :],
                                         xs_hbm.at[pl.ds(dst, size), :], sem.at[which])
        return make

    rows_ref[buf] = _start_run_copies(i, segn_ref, segd_ref, to_slots(buf))

    @pl.when(i > 0)
    def _():
        _wait_rows(rows_ref[1 - buf], to_slots(1 - buf))

    @pl.when(i == pl.num_programs(0) - 1)
    def _():
        _wait_rows(rows_ref[buf], to_slots(buf))

    @pl.when(i == 0)
    def _():
        zero_ref[...] = jnp.zeros(zero_ref.shape, U32)

        def zeros_to_slots(src, dst, size):
            return pltpu.make_async_copy(zero_ref.at[pl.ds(0, size), :], xs_hbm.at[pl.ds(dst, size), :], zsem)

        _fill_copies(zstart_ref, zlen_ref, zeros_to_slots)


def _dispatch(segn, segd, zstart, zlen, ls, hp, n_slots):
    tm = hp.shape[0]
    tile = lambda i, *_: (i, 0)
    return pl.pallas_call(
        _dispatch_kernel,
        out_shape=jax.ShapeDtypeStruct((n_slots, HALF), U32),
        grid_spec=pltpu.PrefetchScalarGridSpec(
            num_scalar_prefetch=4,
            grid=(tm // MOVE_TILE,),
            in_specs=[pl.BlockSpec((TOP_K, MOVE_TILE), lambda i, *_: (0, i)),
                      pl.BlockSpec((MOVE_TILE, HALF), tile)],
            out_specs=pl.BlockSpec(memory_space=pl.ANY),
            scratch_shapes=[pltpu.VMEM((2, STAGE_ROWS, HALF), U32), pltpu.VMEM((EXPERT_BLOCK // 2, HALF), U32),
                            pltpu.SMEM((2,), I32), pltpu.SemaphoreType.DMA((2,)), pltpu.SemaphoreType.DMA]),
        compiler_params=_cparams(("arbitrary",)),
        name="dispatch",
    )(segn, segd, zstart, zlen, ls, hp)


def _expert_kernel(be_ref, nu_ref, nxt_ref, xs_ref, w1_hbm, w3_hbm, w2_hbm, ys_ref, w1f_ref, w3f_ref, w2f_ref,
                   w1b_ref, w3b_ref, w2b_ref, run_ref, sem, *, layer):
    i = pl.program_id(0)

    def fetch(e, slot):
        return [pltpu.make_async_copy(src.at[layer, e], dst.at[slot], sem.at[slot])
                for src, dst in ((w1_hbm, w1f_ref), (w3_hbm, w3f_ref), (w2_hbm, w2f_ref))]

    @pl.when(i == 0)
    def _():
        run_ref[0] = 0
        for cp in fetch(be_ref[0], 0):
            cp.start()

    first = jnp.logical_or(i == 0, be_ref[i] != be_ref[jnp.maximum(i - 1, 0)])

    @pl.when(jnp.logical_and(first, i < nu_ref[0]))
    def _():
        slot = run_ref[0] % 2
        for cp in fetch(be_ref[i], slot):
            cp.wait()
        w1b_ref[...] = w1f_ref[slot].astype(BF16)
        w3b_ref[...] = w3f_ref[slot].astype(BF16)
        w2b_ref[...] = w2f_ref[slot].astype(BF16)
        nxt = nxt_ref[i]

        @pl.when(nxt >= 0)
        def _():
            for cp in fetch(nxt, 1 - slot):
                cp.start()

        run_ref[0] = run_ref[0] + 1

    @pl.when(i < nu_ref[0])
    def _():
        for rows in (slice(r0, r0 + EXPERT_SLAB) for r0 in range(0, EXPERT_BLOCK, EXPERT_SLAB)):
            lo, hi = _unpack_halves(xs_ref[rows, :])
            lo = lo.astype(BF16)
            hi = hi.astype(BF16)
            h1 = (jnp.dot(lo, w1b_ref[:HALF, :], preferred_element_type=F32)
                  + jnp.dot(hi, w1b_ref[HALF:, :], preferred_element_type=F32))
            h3 = (jnp.dot(lo, w3b_ref[:HALF, :], preferred_element_type=F32)
                  + jnp.dot(hi, w3b_ref[HALF:, :], preferred_element_type=F32))
            g = (h1 * jax.nn.sigmoid(h1) * h3).astype(BF16)
            y = jnp.dot(g, w2b_ref[...], preferred_element_type=F32)
            ys_ref[rows, :] = _pack_halves(y)


def _experts(block_e, n_used, next_e, xs, w1, w3, w2, layer):
    n_slots = xs.shape[0]
    n_blocks = n_slots // EXPERT_BLOCK
    d, eh = w1.shape[2], w1.shape[3]
    rows = lambda i, be, nu, nx: (jnp.minimum(i, nu[0] - 1), 0)
    anyspace = pl.BlockSpec(memory_space=pl.ANY)
    return pl.pallas_call(
        functools.partial(_expert_kernel, layer=layer),
        out_shape=jax.ShapeDtypeStruct((n_slots, HALF), U32),
        grid_spec=pltpu.PrefetchScalarGridSpec(
            num_scalar_prefetch=3,
            grid=(n_blocks,),
            in_specs=[pl.BlockSpec((EXPERT_BLOCK, HALF), rows), anyspace, anyspace, anyspace],
            out_specs=pl.BlockSpec((EXPERT_BLOCK, HALF), rows),
            scratch_shapes=[pltpu.VMEM((2, d, eh), F32), pltpu.VMEM((2, d, eh), F32), pltpu.VMEM((2, eh, d), F32),
                            pltpu.VMEM((d, eh), BF16), pltpu.VMEM((d, eh), BF16), pltpu.VMEM((eh, d), BF16),
                            pltpu.SMEM((1,), I32), pltpu.SemaphoreType.DMA((2,))]),
        compiler_params=_cparams(("arbitrary",)),
        name="experts",
    )(block_e, n_used, next_e, xs, w1, w3, w2)


def _combine_kernel(segn_ref, segd_ref, ys_hbm, lst_ref, wt_ref, hp_ref, x_ref, s1_ref, s3_ref, s2_ref, g_ref,
                    gate_ref, o_ref, stage_ref, rows_ref, sem):
    i = pl.program_id(0)
    buf = i % 2

    def from_slots(which):
        def make(src, dst, size):
            return pltpu.make_async_copy(ys_hbm.at[pl.ds(dst, size), :],
                                         stage_ref.at[which, pl.ds(src, size), :], sem.at[which])
        return make

    @pl.when(i == 0)
    def _():
        stage_ref[...] = jnp.zeros(stage_ref.shape, U32)
        rows_ref[0] = _start_run_copies(0, segn_ref, segd_ref, from_slots(0))

    last = pl.num_programs(0) - 1
    rows_ref[1 - buf] = _start_run_copies(jnp.minimum(i + 1, last), segn_ref, segd_ref, from_slots(1 - buf),
                                          straight_line=True)

    lo, hi = _unpack_halves(hp_ref[...])
    lo = lo.astype(BF16)
    hi = hi.astype(BF16)
    h1 = (jnp.dot(lo, s1_ref[:HALF, :], preferred_element_type=F32)
          + jnp.dot(hi, s1_ref[HALF:, :], preferred_element_type=F32))
    h3 = (jnp.dot(lo, s3_ref[:HALF, :], preferred_element_type=F32)
          + jnp.dot(hi, s3_ref[HALF:, :], preferred_element_type=F32))
    y = jnp.dot((h1 * jax.nn.sigmoid(h1) * h3).astype(BF16), s2_ref[...], preferred_element_type=F32)

    _wait_rows(rows_ref[buf], from_slots(buf))

    @pl.when(i == last)
    def _():
        _wait_rows(rows_ref[1 - buf], from_slots(1 - buf))

    lst = lst_ref[...]
    wt = wt_ref[...]
    lst16 = [jnp.broadcast_to(lst[:, k:k + 1], (MOVE_TILE, STAGE_CHUNK)).astype(jnp.int16) for k in range(TOP_K)]
    wt16 = [jnp.broadcast_to(wt[:, k:k + 1], (MOVE_TILE, STAGE_CHUNK)).astype(BF16) for k in range(TOP_K)]
    slot0 = lax.broadcasted_iota(I32, (MOVE_TILE, STAGE_CHUNK), 1).astype(jnp.int16)
    r_lo = jnp.zeros((MOVE_TILE, HALF), F32)
    r_hi = jnp.zeros((MOVE_TILE, HALF), F32)
    for c in range(STAGE_ROWS // STAGE_CHUNK):
        slot = slot0 + jnp.int16(c * STAGE_CHUNK)
        wsel = jnp.zeros((MOVE_TILE, STAGE_CHUNK), BF16)
        for k in range(TOP_K):
            wsel = jnp.where(lst16[k] == slot, wt16[k], wsel)
        a, b = _unpack_halves(stage_ref[buf, c * STAGE_CHUNK:(c + 1) * STAGE_CHUNK, :])
        r_lo = r_lo + jnp.dot(wsel, a.astype(BF16), preferred_element_type=F32)
        r_hi = r_hi + jnp.dot(wsel, b.astype(BF16), preferred_element_type=F32)
    y = y + jnp.concatenate([r_lo, r_hi], axis=1)
    o_ref[...] = x_ref[...] + gate_ref[...] * _rms(y, g_ref[...])


def _combine(segn, segd, ys, lst, wt, hp, x, s1, s3, s2, g, modr, mod_base, row_of_tile):
    tm, d = x.shape
    full = lambda a: pl.BlockSpec(a.shape, lambda i, *_: (0, 0))
    tile = lambda i, *_: (i, 0)
    return pl.pallas_call(
        _combine_kernel,
        out_shape=jax.ShapeDtypeStruct((tm, d), F32),
        grid_spec=pltpu.PrefetchScalarGridSpec(
            num_scalar_prefetch=2,
            grid=(tm // MOVE_TILE,),
            in_specs=[pl.BlockSpec(memory_space=pl.ANY),
                      pl.BlockSpec((MOVE_TILE, TOP_K), tile), pl.BlockSpec((MOVE_TILE, TOP_K), tile),
                      pl.BlockSpec((MOVE_TILE, HALF), tile), pl.BlockSpec((MOVE_TILE, d), tile),
                      full(s1), full(s3), full(s2), full(g),
                      pl.BlockSpec((None, 1, d), lambda i, *_: (mod_base + row_of_tile(i) * 6 + 5, 0, 0))],
            out_specs=pl.BlockSpec((MOVE_TILE, d), tile),
            scratch_shapes=[pltpu.VMEM((2, STAGE_ROWS, HALF), U32), pltpu.SMEM((2,), I32),
                            pltpu.SemaphoreType.DMA((2,))]),
        compiler_params=_cparams(("arbitrary",)),
        name="combine",
    )(segn, segd, ys, lst, wt, hp, x, s1, s3, s2, g, modr)


def _moe(xn, hp, lg_t, bias, expert_weights, s1, s3, s2, g, modr, mod_base, row_of_tile):
    tm = xn.shape[0]
    n_tiles = tm // MOVE_TILE
    perm = np.array([(r % GROUP_SIZE) * GROUP_SIZE + r // GROUP_SIZE for r in range(N_EXPERTS)])
    bias_rep = jnp.broadcast_to(bias.astype(F32)[perm][:, None], (N_EXPERTS, LANES))
    ls, w, segn, segc, cnt = _route(lg_t, bias_rep)

    total = cnt[:, 0]
    padded = (total + EXPERT_BLOCK - 1) // EXPERT_BLOCK * EXPERT_BLOCK
    pad_end = jnp.cumsum(padded)
    pad_start = pad_end - padded
    n_slots = tm * TOP_K + (SEG_ALIGN - 1) * N_EXPERTS * n_tiles + N_EXPERTS * EXPERT_BLOCK
    n_blocks = -(-n_slots // EXPERT_BLOCK)
    starts = jnp.arange(n_blocks, dtype=I32) * EXPERT_BLOCK
    region = jnp.minimum(jnp.sum((pad_end[None, :] <= starts[:, None]).astype(I32), axis=1), N_EXPERTS - 1)
    block_e = jnp.asarray(perm, I32)[region]
    n_used = (pad_end[-1:] // EXPERT_BLOCK).astype(I32)
    after = pad_end[region] // EXPERT_BLOCK
    next_e = jnp.where(after < n_used[0], block_e[jnp.minimum(after, n_blocks - 1)], -1).astype(I32)
    segn = segn[:, :, 0].reshape(-1)
    segd = (segc[:, :, 0] + pad_start[None, :]).reshape(-1).astype(I32)

    xs = _dispatch(segn, segd, (pad_start + total).astype(I32), (padded - total).astype(I32), ls, hp,
                   n_blocks * EXPERT_BLOCK)
    ys = _experts(block_e, n_used, next_e, xs, *expert_weights)
    return _combine(segn, segd, ys, ls.T, w.T, hp, xn, s1.astype(BF16), s3.astype(BF16), s2.astype(BF16), g, modr,
                    mod_base, row_of_tile)


_DEINTERLEAVE = np.concatenate([np.arange(0, HEAD_DIM, 2), np.arange(1, HEAD_DIM, 2)])


def _rope_tables(seq, ctx_len):
    t = np.arange(seq)
    n_pair = HEAD_DIM // 4
    inv = jnp.asarray(ROPE_THETA, F32) ** (-jnp.arange(n_pair, dtype=F32) / n_pair)
    r = jnp.asarray(t // GRID_W, F32)
    c = jnp.asarray(t % GRID_W, F32)
    ang = jnp.concatenate([r[:, None] * inv, c[:, None] * inv], axis=-1)
    cos, sin = jnp.cos(ang), jnp.sin(ang)
    cos_t = jnp.tile(jnp.concatenate([cos, cos], axis=-1), (1, LANES // HEAD_DIM))
    sin_t = jnp.tile(jnp.concatenate([-sin, sin], axis=-1), (1, LANES // HEAD_DIM))
    cos_t = jnp.concatenate([jnp.ones((ctx_len, LANES), F32), cos_t], axis=0)
    sin_t = jnp.concatenate([jnp.zeros((ctx_len, LANES), F32), sin_t], axis=0)
    return cos_t, sin_t


def _ab_layout():
    cols, blocks, gain_kind = [], [], []
    for base_q, base_k, base_v, normed in ((0, 512, 640, True), (768, 1280, 1408, False)):
        for jb in range(4):
            cols += [base_q + h * HEAD_DIM + _DEINTERLEAVE for h in (2 * jb, 2 * jb + 1)]
            blocks.append((normed, True, Q_SCALE))
            gain_kind.append("q" if normed else None)
        for kvh in range(2):
            cols += [base_k + kvh * HEAD_DIM + _DEINTERLEAVE] * 2
            blocks.append((normed, True, 1.0))
            gain_kind.append("k" if normed else None)
        for kvh in range(2):
            cols += [base_v + kvh * HEAD_DIM + np.arange(HEAD_DIM)] * 2
            blocks.append((False, False, 1.0))
            gain_kind.append(None)
    return np.concatenate(cols), tuple(blocks), gain_kind


def _c_layout():
    cols, blocks = [], []
    for base, rope, scale in ((0, True, Q_SCALE), (1024, True, 1.0)):
        for h in range(8):
            cols += [base + (2 * h + m) * HEAD_DIM + _DEINTERLEAVE for m in range(2)]
            blocks.append((False, rope, scale))
    for h in range(8):
        cols.append(2048 + h * LANES + np.arange(LANES))
        blocks.append((False, False, 1.0))
    return np.concatenate(cols), tuple(blocks)


def kernel(x, c, ctx, c_ctx, ada_w, ada_b, norm_g, ab_w_in, ab_w_out, a_q_norm, a_k_norm, b_sink,
           c_w_in, c_w_out, c_lambda, c_subln_g, router_w, router_bias, exp_w1, exp_w3, exp_w2,
           sh_w1, sh_w3, sh_w2):
    batch, seq, d = x.shape
    ctx_len = ctx.shape[1]
    depth = ada_w.shape[0]
    assert d == D_MODEL and ctx_len == ROW_TILE and depth == 2
    assert seq % Q_TILE == 0 and seq >= 2 * KV_TILE and batch <= 4
    n_p = ctx_len + seq
    tiles = n_p // ROW_TILE
    lat_tiles = seq // ROW_TILE
    t_all = batch * n_p

    cond = jnp.zeros((8, d), F32).at[:batch].set(c).at[4].set(c_ctx)
    modr = _modulation(cond, ada_w, ada_b).reshape(depth * 8 * 6, 1, d)
    row_all = lambda i: jnp.where(i % tiles == 0, 4, i // tiles)
    cos_t, sin_t = _rope_tables(seq, ctx_len)
    rperm = np.array([(r % GROUP_SIZE) * GROUP_SIZE + r // GROUP_SIZE for r in range(N_EXPERTS)])
    xt = jnp.concatenate([ctx, x], axis=1).reshape(t_all, d)

    cols, blocks, gain_kind = _ab_layout()
    w0 = ab_w_in[0][:, cols].astype(BF16)
    gq = jnp.tile(a_q_norm[0][_DEINTERLEAVE], 2)
    gk = jnp.tile(a_k_norm[0][_DEINTERLEAVE], 2)
    ones = jnp.ones((LANES,), F32)
    head_gain = jnp.concatenate([{"q": gq, "k": gk, None: ones}[kind] for kind in gain_kind])[None, :]
    p0 = _project(xt, modr, 0, row_all, norm_g[0, 0][None, :], w0, cos_t, sin_t, head_gain, blocks, tiles)
    common = dict(batch=batch, seq=seq, ctx_len=ctx_len, n_qblocks=4, ctx_queries=True)
    oa = _attention(p0, [], mode="dense", q_col0=0, k_col=lambda j: 4 + j // 2, v_col=lambda j: 6 + j // 2,
                    **common)
    sink = jnp.broadcast_to(b_sink[0].astype(F32)[:, None], (8, LANES))
    ob = _attention(p0, [sink], mode="window", q_col0=8, k_col=lambda j: 12 + j // 2,
                    v_col=lambda j: 14 + j // 2, **common)
    w_out = ab_w_out[0].astype(BF16)
    x1, hp, lg = _out_project([oa, ob], [w_out[:512], w_out[512:]], xt, lambda i: i, modr, 0, row_all,
                              norm_g[0, 1][None, :], norm_g[0, 2][None, :], router_w[0].T[rperm],
                              t_all // ROW_TILE)
    x2 = _moe(x1, hp, lg, router_bias[0], (exp_w1, exp_w3, exp_w2, 0), sh_w1[0], sh_w3[0], sh_w2[0],
              norm_g[0, 3][None, :], modr, 0, lambda i: row_all(i // (ROW_TILE // MOVE_TILE)))

    base1 = 8 * 6
    lambda_init = 0.8 - 0.6 * math.exp(-0.3 * 1)
    cols1, blocks1 = _c_layout()
    w1p = c_w_in[0][:, cols1].astype(BF16)
    p1 = _project(x2, modr, base1, row_all, norm_g[1, 0][None, :], w1p, cos_t, sin_t,
                  jnp.ones((1, w1p.shape[1]), F32), blocks1, tiles)
    lam = jnp.zeros((8, LANES), F32).at[:4, :HEAD_DIM].set(c_lambda[0].astype(F32))
    oc = _attention(p1, [lam, c_subln_g[0][None, :]], mode="diff", batch=batch, seq=seq, ctx_len=ctx_len,
                    n_qblocks=8, q_col0=0, k_col=lambda j: 8 + j, v_col=lambda j: 16 + j, ctx_queries=False,
                    lambda_init=lambda_init)
    row_lat = lambda i: i // lat_tiles
    x3, hp1, lg1 = _out_project([oc], [c_w_out[0].astype(BF16)], x2,
                                lambda i: (i // lat_tiles) * tiles + 1 + i % lat_tiles, modr, base1, row_lat,
                                norm_g[1, 1][None, :], norm_g[1, 2][None, :], router_w[1].T[rperm],
                                batch * lat_tiles)
    out = _moe(x3, hp1, lg1, router_bias[1], (exp_w1, exp_w3, exp_w2, 1), sh_w1[1], sh_w3[1], sh_w2[1],
               norm_g[1, 3][None, :], modr, base1, lambda i: row_lat(i // (ROW_TILE // MOVE_TILE)))
    return out.reshape(batch, seq, d)
```

```python
import functools
import math

import numpy as np
import jax
import jax.numpy as jnp
from jax import lax
from jax.experimental import pallas as pl
from jax.experimental.pallas import tpu as pltpu

F32 = jnp.float32
BF16 = jnp.bfloat16
U32 = jnp.uint32
I32 = jnp.int32
HIGHEST = lax.Precision.HIGHEST

D_MODEL = 1024
HEAD_DIM = 64
LANES = 128
SUBLANES = 8
GRID_W = 64
ROPE_THETA = 10000.0
EPS = 1e-6
NEG_INF = -1e30
WINDOW = 128
N_EXPERTS = 64
TOP_K = 8
N_GROUPS = 8
TOPK_GROUPS = 4
GROUP_SIZE = N_EXPERTS // N_GROUPS
ROUTED_SCALE = 2.5
LOG2E = 1.4426950408889634
Q_SCALE = HEAD_DIM ** -0.5 * LOG2E

ROW_TILE = 256
Q_TILE = 256
KV_TILE = 256
LAT_KV_TILE = 256
MOVE_TILE = 256
EXPERT_BLOCK = 1024
EXPERT_SLAB = 1024
SEG_ALIGN = 8
RUN_PIECES = (256, 128, 64, 32, 16, 8)
WAIT_PIECES = (2048, 1024, 512) + RUN_PIECES
STAGE_ROWS = 2560
STAGE_CHUNK = 512
HALF = D_MODEL // 2
VMEM_LIMIT = 48 * 1024 * 1024


def _cparams(sem):
    return pltpu.CompilerParams(dimension_semantics=sem, vmem_limit_bytes=VMEM_LIMIT)


def _rms(x, g):
    ms = jnp.mean(x * x, axis=-1, keepdims=True)
    return x * lax.rsqrt(ms + EPS) * g


def _pack_halves(h):
    lo = lax.bitcast_convert_type(h[:, :HALF].astype(BF16).astype(F32), U32) >> 16
    hi = lax.bitcast_convert_type(h[:, HALF:].astype(BF16).astype(F32), U32) & jnp.uint32(0xFFFF0000)
    return hi | lo


def _unpack_halves(u):
    lo = lax.bitcast_convert_type(u << 16, F32)
    hi = lax.bitcast_convert_type(u & jnp.uint32(0xFFFF0000), F32)
    return lo, hi


def _mod_kernel(c_ref, w_ref, b_ref, o_ref):
    c = c_ref[...]
    sc = c * jax.nn.sigmoid(c)
    o_ref[0] = jnp.dot(sc, w_ref[0], precision=HIGHEST, preferred_element_type=F32) + b_ref[0]


def _modulation(cond, ada_w, ada_b):
    depth, d, n = ada_w.shape
    nt = 1536
    return pl.pallas_call(
        _mod_kernel,
        out_shape=jax.ShapeDtypeStruct((depth, 8, n), F32),
        grid=(depth, n // nt),
        in_specs=[pl.BlockSpec((8, d), lambda l, j: (0, 0)),
                  pl.BlockSpec((1, d, nt), lambda l, j: (l, 0, j)),
                  pl.BlockSpec((1, 1, nt), lambda l, j: (l, 0, j))],
        out_specs=pl.BlockSpec((1, 8, nt), lambda l, j: (l, 0, j)),
        compiler_params=_cparams(("arbitrary", "arbitrary")),
        name="ada_mod",
    )(cond, ada_w, ada_b.reshape(depth, 1, n))


def _proj_kernel(x_ref, sh_ref, sc_ref, g_ref, w_ref, cos_ref, sin_ref, hg_ref, gm_ref, o_ref, *, blocks):
    h = _rms(x_ref[...], g_ref[...]) * (1.0 + sc_ref[...]) + sh_ref[...]
    y = jnp.dot(h.astype(BF16), w_ref[...], preferred_element_type=F32)
    lane = lax.broadcasted_iota(I32, (x_ref.shape[0], LANES), 1)
    first_half = (lane % HEAD_DIM) < (HEAD_DIM // 2)
    for jb, (norm, rope, scale) in enumerate(blocks):
        cols = slice(jb * LANES, (jb + 1) * LANES)
        yb = y[:, cols]
        if norm:
            ms = jnp.dot(yb * yb, gm_ref[...], precision=HIGHEST, preferred_element_type=F32)
            yb = yb * lax.rsqrt(ms + EPS) * hg_ref[:, cols]
        if rope:
            swapped = jnp.where(first_half, pltpu.roll(yb, LANES - HEAD_DIM // 2, 1),
                                pltpu.roll(yb, HEAD_DIM // 2, 1))
            yb = yb * cos_ref[...] + swapped * sin_ref[...]
        if scale != 1.0:
            yb = yb * scale
        o_ref[:, cols] = yb.astype(BF16)


def _project(x, modr, mod_base, row_of_tile, g, w, cos_t, sin_t, head_gain, blocks, tiles_per_batch):
    t, d = x.shape
    n = w.shape[1]
    group_mean = jnp.asarray(np.kron(np.eye(LANES // HEAD_DIM), np.full((HEAD_DIM, HEAD_DIM), 1.0 / HEAD_DIM)), F32)
    return pl.pallas_call(
        functools.partial(_proj_kernel, blocks=blocks),
        out_shape=jax.ShapeDtypeStruct((t, n), BF16),
        grid=(t // ROW_TILE,),
        in_specs=[pl.BlockSpec((ROW_TILE, d), lambda i: (i, 0)),
                  pl.BlockSpec((None, 1, d), lambda i: (mod_base + row_of_tile(i) * 6 + 0, 0, 0)),
                  pl.BlockSpec((None, 1, d), lambda i: (mod_base + row_of_tile(i) * 6 + 1, 0, 0)),
                  pl.BlockSpec((1, d), lambda i: (0, 0)),
                  pl.BlockSpec((d, n), lambda i: (0, 0)),
                  pl.BlockSpec((ROW_TILE, LANES), lambda i: (i % tiles_per_batch, 0)),
                  pl.BlockSpec((ROW_TILE, LANES), lambda i: (i % tiles_per_batch, 0)),
                  pl.BlockSpec((1, n), lambda i: (0, 0)),
                  pl.BlockSpec((LANES, LANES), lambda i: (0, 0))],
        out_specs=pl.BlockSpec((ROW_TILE, n), lambda i: (i, 0)),
        compiler_params=_cparams(("arbitrary",)),
        name="prenorm_proj",
    )(x, modr, modr, g, w, cos_t, sin_t, head_gain, group_mean)


def _attn_kernel(*refs, mode, n_kv, seq, ctx_len, ctx_queries, lambda_init, q_parts):
    q_refs, refs = refs[:q_parts], refs[q_parts:]
    if mode == "window":
        k_ref, v_ref, sink_ref, o_ref, m_ref, l_ref, acc_ref = refs
    elif mode == "diff":
        k_ref, v_ref, lam_ref, sg_ref, o_ref, m_ref, l_ref, acc_ref = refs
    else:
        k_ref, v_ref, o_ref, m_ref, l_ref, acc_ref = refs
    tq = q_parts * q_refs[0].shape[0]
    j = pl.program_id(1)
    qi = pl.program_id(2)
    is_ctx_q = (qi == 0) if ctx_queries else False

    q = q_refs[0][...] if q_parts == 1 else jnp.concatenate([r[...] for r in q_refs], axis=0)
    lane = lax.broadcasted_iota(I32, (tq, LANES), 1)
    low = lane < HEAD_DIM
    zero = jnp.zeros_like(q)
    q2 = jnp.concatenate([jnp.where(low, q, zero), jnp.where(low, zero, q)], axis=0)

    if mode == "window":
        s0 = sink_ref[pl.ds(2 * j, 1), :]
        s1 = sink_ref[pl.ds(2 * j + 1, 1), :]
        row = lax.broadcasted_iota(I32, (2 * tq, LANES), 0)
        sink = jnp.where(row < tq, s0, s1) * LOG2E
        m0 = sink
    else:
        m0 = jnp.full((2 * tq, LANES), NEG_INF, F32)
    state = (m0, jnp.zeros((2 * tq, LANES), F32), jnp.zeros((2 * tq, LANES), F32))

    def chunk(state, start, valid, size=KV_TILE):
        m_prev, l_prev, acc_prev = state
        k = k_ref[pl.ds(start, size), :]
        v = v_ref[pl.ds(start, size), :]
        s = lax.dot_general(q2, k, (((1,), (1,)), ((), ())), preferred_element_type=F32)
        if valid is not None:
            s = jnp.where(valid, s, NEG_INF)
        m_new = jnp.maximum(m_prev, jnp.max(s, axis=1, keepdims=True))
        alpha = jnp.exp2(m_prev - m_new)
        p = jnp.exp2(s - jnp.concatenate([m_new] * (size // LANES), axis=1))
        part = p[:, :LANES]
        for c in range(1, size // LANES):
            part = part + p[:, c * LANES:(c + 1) * LANES]
        return (m_new, alpha * l_prev + part,
                alpha * acc_prev + jnp.dot(p.astype(BF16), v, preferred_element_type=F32))

    def save(state):
        m_ref[...], l_ref[...], acc_ref[...] = state

    def load():
        return (m_ref[...], l_ref[...], acc_ref[...])

    state = chunk(state, 0, None)
    if mode == "window":
        save(state)

        @pl.when(jnp.logical_not(is_ctx_q))
        def _():
            st = load()
            q0 = (qi - (1 if ctx_queries else 0)) * tq
            kstart = jnp.clip(q0 - WINDOW, 0, seq - 2 * KV_TILE)
            r = lax.broadcasted_iota(I32, (2 * tq, KV_TILE), 0)
            qpos = q0 + jnp.where(r >= tq, r - tq, r)
            col = lax.broadcasted_iota(I32, (2 * tq, KV_TILE), 1)
            for w in range(2):
                kpos = kstart + w * KV_TILE + col
                st = chunk(st, pl.multiple_of(ctx_len + kstart + w * KV_TILE, WINDOW),
                           jnp.abs(qpos - kpos) <= WINDOW)
            save(st)
    elif ctx_queries:
        save(state)

        @pl.when(jnp.logical_not(is_ctx_q))
        def _():
            st = load()
            for c in range(seq // LAT_KV_TILE):
                st = chunk(st, ctx_len + c * LAT_KV_TILE, None, LAT_KV_TILE)
            save(st)
    else:
        for c in range(seq // LAT_KV_TILE):
            state = chunk(state, ctx_len + c * LAT_KV_TILE, None, LAT_KV_TILE)
        save(state)

    l = jnp.sum(l_ref[...], axis=1, keepdims=True)
    if mode == "window":
        l = l + jnp.exp2(sink - m_ref[...])[:, :1]
    o2 = acc_ref[...] / l
    if mode == "diff":
        lp = lam_ref[...]
        lam = (jnp.exp(jnp.sum(lp[0:1] * lp[1:2], axis=1, keepdims=True))
               - jnp.exp(jnp.sum(lp[2:3] * lp[3:4], axis=1, keepdims=True)) + lambda_init)
        o = o2[:tq] - lam * o2[tq:]
        o = _rms(o, sg_ref[...]) * (1.0 - lambda_init)
    else:
        o = jnp.where(low, o2[:tq], o2[tq:])
    o_ref[...] = o.astype(BF16)


def _attention(p, extra, *, mode, batch, seq, ctx_len, n_qblocks, q_col0, k_col, v_col, ctx_queries,
               lambda_init=0.0, q_parts=1):
    n_p = ctx_len + seq
    tiles = n_p // Q_TILE
    tq = q_parts * Q_TILE
    qt = (tiles if ctx_queries else seq // Q_TILE) // q_parts
    q_off = 0 if ctx_queries else ctx_len // Q_TILE
    in_specs = [pl.BlockSpec((Q_TILE, LANES),
                             lambda b, j, qi, part=part: (b * tiles + q_off + qi * q_parts + part, q_col0 + j))
                for part in range(q_parts)]
    in_specs += [pl.BlockSpec((n_p, LANES), lambda b, j, qi: (b, k_col(j))),
                 pl.BlockSpec((n_p, LANES), lambda b, j, qi: (b, v_col(j)))]
    args = [p] * (q_parts + 2)
    for e in extra:
        in_specs.append(pl.BlockSpec(e.shape, lambda b, j, qi: (0, 0)))
        args.append(e)
    kern = functools.partial(_attn_kernel, mode=mode, n_kv=n_p // KV_TILE, seq=seq, ctx_len=ctx_len,
                             ctx_queries=ctx_queries, lambda_init=lambda_init, q_parts=q_parts)
    return pl.pallas_call(
        kern,
        out_shape=jax.ShapeDtypeStruct((batch * qt * tq, n_qblocks * LANES), BF16),
        grid=(batch, n_qblocks, qt),
        in_specs=in_specs,
        out_specs=pl.BlockSpec((tq, LANES), lambda b, j, qi: (b * qt + qi, j)),
        scratch_shapes=[pltpu.VMEM((2 * tq, LANES), F32)] * 3,
        compiler_params=_cparams(("arbitrary", "arbitrary", "arbitrary")),
        name="attn_" + mode,
    )(*args)


def _out_kernel(*refs, n_o):
    o_refs = refs[:n_o]
    w_refs = refs[n_o:2 * n_o]
    x_ref, g1_ref, gate_ref, g2_ref, sh_ref, sc_ref, rw_ref, xn_ref, hp_ref, lg_ref = refs[2 * n_o:]
    slab = LANES
    for r0 in range(0, x_ref.shape[0], slab):
        rows = slice(r0, r0 + slab)
        y = jnp.dot(o_refs[0][rows, :], w_refs[0][...], preferred_element_type=F32)
        for a in range(1, n_o):
            y = y + jnp.dot(o_refs[a][rows, :], w_refs[a][...], preferred_element_type=F32)
        xn = x_ref[rows, :] + gate_ref[...] * _rms(y, g1_ref[...])
        xn_ref[rows, :] = xn
        h = _rms(xn, g2_ref[...]) * (1.0 + sc_ref[...]) + sh_ref[...]
        hp_ref[rows, :] = _pack_halves(h)
        lg_ref[:, rows] = lax.dot_general(rw_ref[...], h, (((1,), (1,)), ((), ())), precision=HIGHEST,
                                          preferred_element_type=F32)


def _out_project(os_, ws, x, x_tile, modr, mod_base, row_of_tile, g1, g2, rw_t, n_tiles):
    d = x.shape[1]
    n_o = len(os_)
    tm = n_tiles * ROW_TILE
    mspec = lambda which: pl.BlockSpec((None, 1, d), lambda i: (mod_base + row_of_tile(i) * 6 + which, 0, 0))
    in_specs = ([pl.BlockSpec((ROW_TILE, o.shape[1]), lambda i: (i, 0)) for o in os_]
                + [pl.BlockSpec(w.shape, lambda i: (0, 0)) for w in ws]
                + [pl.BlockSpec((ROW_TILE, d), lambda i: (x_tile(i), 0)),
                   pl.BlockSpec((1, d), lambda i: (0, 0)), mspec(2),
                   pl.BlockSpec((1, d), lambda i: (0, 0)), mspec(3), mspec(4),
                   pl.BlockSpec(rw_t.shape, lambda i: (0, 0))])
    return pl.pallas_call(
        functools.partial(_out_kernel, n_o=n_o),
        out_shape=(jax.ShapeDtypeStruct((tm, d), F32), jax.ShapeDtypeStruct((tm, HALF), U32),
                   jax.ShapeDtypeStruct((N_EXPERTS, tm), F32)),
        grid=(n_tiles,),
        in_specs=in_specs,
        out_specs=(pl.BlockSpec((ROW_TILE, d), lambda i: (i, 0)),
                   pl.BlockSpec((ROW_TILE, HALF), lambda i: (i, 0)),
                   pl.BlockSpec((N_EXPERTS, ROW_TILE), lambda i: (0, i))),
        compiler_params=_cparams(("arbitrary",)),
        name="out_proj",
    )(*os_, *ws, x, g1, modr, g2, modr, modr, rw_t)


def _route_kernel(lg_ref, bias_ref, tri_ref, ltri_ref, ls_ref, w_ref, segn_ref, segc_ref, cnt_ref, carry_ref):
    i = pl.program_id(0)

    @pl.when(i == 0)
    def _():
        carry_ref[...] = jnp.zeros(carry_ref.shape, F32)

    tr = lg_ref.shape[1]
    score = jax.nn.sigmoid(lg_ref[...])
    sel = score + bias_ref[...][:, :1]
    sel_j = [sel[j * GROUP_SIZE:(j + 1) * GROUP_SIZE] for j in range(GROUP_SIZE)]
    sc_j = [score[j * GROUP_SIZE:(j + 1) * GROUP_SIZE] for j in range(GROUP_SIZE)]
    gi = lax.broadcasted_iota(I32, (N_GROUPS, tr), 0)

    m1 = sel_j[0]
    m2 = jnp.full_like(m1, -jnp.inf)
    for j in range(1, GROUP_SIZE):
        m2 = jnp.maximum(m2, jnp.minimum(m1, sel_j[j]))
        m1 = jnp.maximum(m1, sel_j[j])
    gs = m1 + m2

    grank = jnp.zeros((N_GROUPS, tr), I32)
    for gp in range(N_GROUPS):
        rowv = gs[gp:gp + 1, :]
        grank = grank + jnp.where(gi > gp, jnp.where(rowv >= gs, 1, 0), jnp.where(rowv > gs, 1, 0))
    gmask = grank < TOPK_GROUPS
    val_j = [jnp.where(gmask, sel_j[j], NEG_INF) for j in range(GROUP_SIZE)]

    rank_j = [jnp.zeros((N_GROUPS, tr), I32) for _ in range(GROUP_SIZE)]
    for gp in range(N_GROUPS):
        after = gi > gp
        not_before = gi >= gp
        for jp in range(GROUP_SIZE):
            rowv = val_j[jp][gp:gp + 1, :]
            for j in range(GROUP_SIZE):
                tie = after if jp >= j else not_before
                rank_j[j] = rank_j[j] + jnp.where(tie, jnp.where(rowv >= val_j[j], 1, 0),
                                                  jnp.where(rowv > val_j[j], 1, 0))

    chosen = jnp.concatenate([jnp.where(rank_j[j] < TOP_K, 1.0, 0.0) for j in range(GROUP_SIZE)], axis=0)
    n_run = jnp.floor((jnp.sum(chosen, axis=1, keepdims=True) + (SEG_ALIGN - 1.0)) * (1.0 / SEG_ALIGN)) * SEG_ALIGN
    n_run = jnp.broadcast_to(n_run, (N_EXPERTS, LANES))
    run_start = jnp.dot(ltri_ref[...], n_run.astype(BF16), preferred_element_type=F32)
    local = jnp.dot(chosen.astype(BF16), tri_ref[...], preferred_element_type=F32) + run_start[:, :1]
    segn_ref[0] = n_run.astype(I32)
    segc_ref[0] = carry_ref[...].astype(I32)
    carry_ref[...] = carry_ref[...] + n_run
    cnt_ref[...] = carry_ref[...].astype(I32)

    w_rows, ls_rows = [], []
    for k in range(TOP_K):
        w_acc = jnp.zeros((N_GROUPS, tr), F32)
        p_acc = jnp.zeros((N_GROUPS, tr), F32)
        for j in range(GROUP_SIZE):
            hit = rank_j[j] == k
            w_acc = w_acc + jnp.where(hit, sc_j[j], 0.0)
            p_acc = p_acc + jnp.where(hit, local[j * GROUP_SIZE:(j + 1) * GROUP_SIZE], 0.0)
        w_rows.append(jnp.sum(w_acc, axis=0, keepdims=True))
        ls_rows.append(jnp.sum(p_acc, axis=0, keepdims=True))
    w_all = jnp.concatenate(w_rows, axis=0)
    w_ref[...] = w_all / jnp.sum(w_all, axis=0, keepdims=True) * ROUTED_SCALE
    ls_ref[...] = jnp.concatenate(ls_rows, axis=0).astype(I32)


def _route(lg_t, bias_rep):
    tm = lg_t.shape[1]
    n_tiles = tm // MOVE_TILE
    tri = jnp.asarray(np.triu(np.ones((MOVE_TILE, MOVE_TILE), np.float32), 1), BF16)
    ltri = jnp.asarray(np.tril(np.ones((N_EXPERTS, N_EXPERTS), np.float32), -1), BF16)
    tok = lambda i: (0, i)
    per_tile = pl.BlockSpec((1, N_EXPERTS, LANES), lambda i: (i, 0, 0))
    return pl.pallas_call(
        _route_kernel,
        out_shape=(jax.ShapeDtypeStruct((TOP_K, tm), I32), jax.ShapeDtypeStruct((TOP_K, tm), F32),
                   jax.ShapeDtypeStruct((n_tiles, N_EXPERTS, LANES), I32),
                   jax.ShapeDtypeStruct((n_tiles, N_EXPERTS, LANES), I32),
                   jax.ShapeDtypeStruct((N_EXPERTS, LANES), I32)),
        grid=(n_tiles,),
        in_specs=[pl.BlockSpec((N_EXPERTS, MOVE_TILE), tok),
                  pl.BlockSpec((N_EXPERTS, LANES), lambda i: (0, 0)),
                  pl.BlockSpec((MOVE_TILE, MOVE_TILE), lambda i: (0, 0)),
                  pl.BlockSpec((N_EXPERTS, N_EXPERTS), lambda i: (0, 0))],
        out_specs=(pl.BlockSpec((TOP_K, MOVE_TILE), tok), pl.BlockSpec((TOP_K, MOVE_TILE), tok),
                   per_tile, per_tile, pl.BlockSpec((N_EXPERTS, LANES), lambda i: (0, 0))),
        scratch_shapes=[pltpu.VMEM((N_EXPERTS, LANES), F32)],
        compiler_params=_cparams(("arbitrary",)),
        name="route",
    )(lg_t, bias_rep, tri, ltri)


def _start_pieces(n, src, dst, make, pieces):
    for size in pieces:
        above = n & (-2 * size)

        @pl.when((n & size) != 0)
        def _():
            make(pl.multiple_of(src + above, SEG_ALIGN), pl.multiple_of(dst + above, SEG_ALIGN), size).start()


def _start_run_copies(i, segn_ref, segd_ref, make, straight_line=False):
    split = RUN_PIECES.index(64)

    def per_run(r, src):
        n = segn_ref[i * N_EXPERTS + r]
        dst = segd_ref[i * N_EXPERTS + r]

        @pl.when(n >= RUN_PIECES[split - 1])
        def _():
            _start_pieces(n, src, dst, make, RUN_PIECES[:split])

        _start_pieces(n, src, dst, make, RUN_PIECES[split:])
        return src + n

    if not straight_line:
        return lax.fori_loop(0, N_EXPERTS, per_run, 0, unroll=2)
    src = 0
    for r in range(N_EXPERTS):
        src = per_run(r, src)
    return src


def _wait_rows(total, make):
    for size in WAIT_PIECES:
        @pl.when((total & size) != 0)
        def _():
            make(0, 0, size).wait()


def _fill_copies(zstart_ref, zlen_ref, make):
    pieces = tuple(p for p in WAIT_PIECES if p < EXPERT_BLOCK)

    def per_expert(r, carry):
        n = zlen_ref[r]
        _start_pieces(n, 0, zstart_ref[r], make, pieces)
        _wait_rows(n, make)
        return carry

    lax.fori_loop(0, N_EXPERTS, per_expert, 0)


def _dispatch_kernel(segn_ref, segd_ref, zstart_ref, zlen_ref, ls_ref, hp_ref, xs_hbm, stage_ref, zero_ref,
                     rows_ref, sem, zsem):
    i = pl.program_id(0)
    buf = i % 2
    lo, hi = _unpack_halves(hp_ref[...])
    lo = lo.astype(BF16)
    hi = hi.astype(BF16)
    ls = ls_ref[...]
    pack_rows = 16
    ls16 = [jnp.tile(jnp.broadcast_to(ls[k:k + 1, :], (pack_rows, MOVE_TILE)).astype(jnp.int16),
                     (STAGE_CHUNK // pack_rows, 1)) for k in range(TOP_K)]
    slot0 = lax.broadcasted_iota(I32, (STAGE_CHUNK, MOVE_TILE), 0).astype(jnp.int16)
    one = jnp.ones((STAGE_CHUNK, MOVE_TILE), BF16)
    for c in range(STAGE_ROWS // STAGE_CHUNK):
        slot = slot0 + jnp.int16(c * STAGE_CHUNK)
        onehot = jnp.zeros((STAGE_CHUNK, MOVE_TILE), BF16)
        for k in range(TOP_K):
            onehot = jnp.where(ls16[k] == slot, one, onehot)
        a = lax.bitcast_convert_type(jnp.dot(onehot, lo, preferred_element_type=F32), U32) >> 16
        b = lax.bitcast_convert_type(jnp.dot(onehot, hi, preferred_element_type=F32), U32)
        stage_ref[buf, c * STAGE_CHUNK:(c + 1) * STAGE_CHUNK, :] = b | a

    def to_slots(which):
        def make(src, dst, size):
            return pltpu.make_async_copy(stage_ref.at[which, pl.ds(src, size), :],
                                         xs_hbm.at[pl.ds(dst, size), :], sem.at[which])
        return make

    rows_ref[buf] = _start_run_copies(i, segn_ref, segd_ref, to_slots(buf))

    @pl.when(i > 0)
    def _():
        _wait_rows(rows_ref[1 - buf], to_slots(1 - buf))

    @pl.when(i == pl.num_programs(0) - 1)
    def _():
        _wait_rows(rows_ref[buf], to_slots(buf))

    @pl.when(i == 0)
    def _():
        zero_ref[...] = jnp.zeros(zero_ref.shape, U32)

        def zeros_to_slots(src, dst, size):
            return pltpu.make_async_copy(zero_ref.at[pl.ds(0, size), :], xs_hbm.at[pl.ds(dst, size), :], zsem)

        _fill_copies(zstart_ref, zlen_ref, zeros_to_slots)


def _dispatch(segn, segd, zstart, zlen, ls, hp, n_slots):
    tm = hp.shape[0]
    tile = lambda i, *_: (i, 0)
    return pl.pallas_call(
        _dispatch_kernel,
        out_shape=jax.ShapeDtypeStruct((n_slots, HALF), U32),
        grid_spec=pltpu.PrefetchScalarGridSpec(
            num_scalar_prefetch=4,
            grid=(tm // MOVE_TILE,),
            in_specs=[pl.BlockSpec((TOP_K, MOVE_TILE), lambda i, *_: (0, i)),
                      pl.BlockSpec((MOVE_TILE, HALF), tile)],
            out_specs=pl.BlockSpec(memory_space=pl.ANY),
            scratch_shapes=[pltpu.VMEM((2, STAGE_ROWS, HALF), U32), pltpu.VMEM((EXPERT_BLOCK // 2, HALF), U32),
                            pltpu.SMEM((2,), I32), pltpu.SemaphoreType.DMA((2,)), pltpu.SemaphoreType.DMA]),
        compiler_params=_cparams(("arbitrary",)),
        name="dispatch",
    )(segn, segd, zstart, zlen, ls, hp)


def _expert_kernel(be_ref, nu_ref, nxt_ref, xs_ref, w1_hbm, w3_hbm, w2_hbm, ys_ref, w1f_ref, w3f_ref, w2f_ref,
                   w1b_ref, w3b_ref, w2b_ref, run_ref, sem, *, layer):
    i = pl.program_id(0)

    def fetch(e, slot):
        return [pltpu.make_async_copy(src.at[layer, e], dst.at[slot], sem.at[slot])
                for src, dst in ((w1_hbm, w1f_ref), (w3_hbm, w3f_ref), (w2_hbm, w2f_ref))]

    @pl.when(i == 0)
    def _():
        run_ref[0] = 0
        for cp in fetch(be_ref[0], 0):
            cp.start()

    first = jnp.logical_or(i == 0, be_ref[i] != be_ref[jnp.maximum(i - 1, 0)])

    @pl.when(jnp.logical_and(first, i < nu_ref[0]))
    def _():
        slot = run_ref[0] % 2
        for cp in fetch(be_ref[i], slot):
            cp.wait()
        w1b_ref[...] = w1f_ref[slot].astype(BF16)
        w3b_ref[...] = w3f_ref[slot].astype(BF16)
        w2b_ref[...] = w2f_ref[slot].astype(BF16)
        nxt = nxt_ref[i]

        @pl.when(nxt >= 0)
        def _():
            for cp in fetch(nxt, 1 - slot):
                cp.start()

        run_ref[0] = run_ref[0] + 1

    @pl.when(i < nu_ref[0])
    def _():
        for rows in (slice(r0, r0 + EXPERT_SLAB) for r0 in range(0, EXPERT_BLOCK, EXPERT_SLAB)):
            lo, hi = _unpack_halves(xs_ref[rows, :])
            lo = lo.astype(BF16)
            hi = hi.astype(BF16)
            h1 = (jnp.dot(lo, w1b_ref[:HALF, :], preferred_element_type=F32)
                  + jnp.dot(hi, w1b_ref[HALF:, :], preferred_element_type=F32))
            h3 = (jnp.dot(lo, w3b_ref[:HALF, :], preferred_element_type=F32)
                  + jnp.dot(hi, w3b_ref[HALF:, :], preferred_element_type=F32))
            g = (h1 * jax.nn.sigmoid(h1) * h3).astype(BF16)
            y = jnp.dot(g, w2b_ref[...], preferred_element_type=F32)
            ys_ref[rows, :] = _pack_halves(y)


def _experts(block_e, n_used, next_e, xs, w1, w3, w2, layer):
    n_slots = xs.shape[0]
    n_blocks = n_slots // EXPERT_BLOCK
    d, eh = w1.shape[2], w1.shape[3]
    rows = lambda i, be, nu, nx: (jnp.minimum(i, nu[0] - 1), 0)
    anyspace = pl.BlockSpec(memory_space=pl.ANY)
    return pl.pallas_call(
        functools.partial(_expert_kernel, layer=layer),
        out_shape=jax.ShapeDtypeStruct((n_slots, HALF), U32),
        grid_spec=pltpu.PrefetchScalarGridSpec(
            num_scalar_prefetch=3,
            grid=(n_blocks,),
            in_specs=[pl.BlockSpec((EXPERT_BLOCK, HALF), rows), anyspace, anyspace, anyspace],
            out_specs=pl.BlockSpec((EXPERT_BLOCK, HALF), rows),
            scratch_shapes=[pltpu.VMEM((2, d, eh), F32), pltpu.VMEM((2, d, eh), F32), pltpu.VMEM((2, eh, d), F32),
                            pltpu.VMEM((d, eh), BF16), pltpu.VMEM((d, eh), BF16), pltpu.VMEM((eh, d), BF16),
                            pltpu.SMEM((1,), I32), pltpu.SemaphoreType.DMA((2,))]),
        compiler_params=_cparams(("arbitrary",)),
        name="experts",
    )(block_e, n_used, next_e, xs, w1, w3, w2)


def _combine_kernel(segn_ref, segd_ref, ys_hbm, lst_ref, wt_ref, hp_ref, x_ref, s1_ref, s3_ref, s2_ref, g_ref,
                    gate_ref, o_ref, stage_ref, rows_ref, sem):
    i = pl.program_id(0)
    buf = i % 2

    def from_slots(which):
        def make(src, dst, size):
            return pltpu.make_async_copy(ys_hbm.at[pl.ds(dst, size), :],
                                         stage_ref.at[which, pl.ds(src, size), :], sem.at[which])
        return make

    @pl.when(i == 0)
    def _():
        stage_ref[...] = jnp.zeros(stage_ref.shape, U32)
        rows_ref[0] = _start_run_copies(0, segn_ref, segd_ref, from_slots(0))

    last = pl.num_programs(0) - 1
    rows_ref[1 - buf] = _start_run_copies(jnp.minimum(i + 1, last), segn_ref, segd_ref, from_slots(1 - buf),
                                          straight_line=True)

    lo, hi = _unpack_halves(hp_ref[...])
    lo = lo.astype(BF16)
    hi = hi.astype(BF16)
    h1 = (jnp.dot(lo, s1_ref[:HALF, :], preferred_element_type=F32)
          + jnp.dot(hi, s1_ref[HALF:, :], preferred_element_type=F32))
    h3 = (jnp.dot(lo, s3_ref[:HALF, :], preferred_element_type=F32)
          + jnp.dot(hi, s3_ref[HALF:, :], preferred_element_type=F32))
    y = jnp.dot((h1 * jax.nn.sigmoid(h1) * h3).astype(BF16), s2_ref[...], preferred_element_type=F32)

    _wait_rows(rows_ref[buf], from_slots(buf))

    @pl.when(i == last)
    def _():
        _wait_rows(rows_ref[1 - buf], from_slots(1 - buf))

    lst = lst_ref[...]
    wt = wt_ref[...]
    lst16 = [jnp.broadcast_to(lst[:, k:k + 1], (MOVE_TILE, STAGE_CHUNK)).astype(jnp.int16) for k in range(TOP_K)]
    wt16 = [jnp.broadcast_to(wt[:, k:k + 1], (MOVE_TILE, STAGE_CHUNK)).astype(BF16) for k in range(TOP_K)]
    slot0 = lax.broadcasted_iota(I32, (MOVE_TILE, STAGE_CHUNK), 1).astype(jnp.int16)
    r_lo = jnp.zeros((MOVE_TILE, HALF), F32)
    r_hi = jnp.zeros((MOVE_TILE, HALF), F32)
    for c in range(STAGE_ROWS // STAGE_CHUNK):
        slot = slot0 + jnp.int16(c * STAGE_CHUNK)
        wsel = jnp.zeros((MOVE_TILE, STAGE_CHUNK), BF16)
        for k in range(TOP_K):
            wsel = jnp.where(lst16[k] == slot, wt16[k], wsel)
        a, b = _unpack_halves(stage_ref[buf, c * STAGE_CHUNK:(c + 1) * STAGE_CHUNK, :])
        r_lo = r_lo + jnp.dot(wsel, a.astype(BF16), preferred_element_type=F32)
        r_hi = r_hi + jnp.dot(wsel, b.astype(BF16), preferred_element_type=F32)
    y = y + jnp.concatenate([r_lo, r_hi], axis=1)
    o_ref[...] = x_ref[...] + gate_ref[...] * _rms(y, g_ref[...])


def _combine(segn, segd, ys, lst, wt, hp, x, s1, s3, s2, g, modr, mod_base, row_of_tile):
    tm, d = x.shape
    full = lambda a: pl.BlockSpec(a.shape, lambda i, *_: (0, 0))
    tile = lambda i, *_: (i, 0)
    return pl.pallas_call(
        _combine_kernel,
        out_shape=jax.ShapeDtypeStruct((tm, d), F32),
        grid_spec=pltpu.PrefetchScalarGridSpec(
            num_scalar_prefetch=2,
            grid=(tm // MOVE_TILE,),
            in_specs=[pl.BlockSpec(memory_space=pl.ANY),
                      pl.BlockSpec((MOVE_TILE, TOP_K), tile), pl.BlockSpec((MOVE_TILE, TOP_K), tile),
                      pl.BlockSpec((MOVE_TILE, HALF), tile), pl.BlockSpec((MOVE_TILE, d), tile),
                      full(s1), full(s3), full(s2), full(g),
                      pl.BlockSpec((None, 1, d), lambda i, *_: (mod_base + row_of_tile(i) * 6 + 5, 0, 0))],
            out_specs=pl.BlockSpec((MOVE_TILE, d), tile),
            scratch_shapes=[pltpu.VMEM((2, STAGE_ROWS, HALF), U32), pltpu.SMEM((2,), I32),
                            pltpu.SemaphoreType.DMA((2,))]),
        compiler_params=_cparams(("arbitrary",)),
        name="combine",
    )(segn, segd, ys, lst, wt, hp, x, s1, s3, s2, g, modr)


def _moe(xn, hp, lg_t, bias, expert_weights, s1, s3, s2, g, modr, mod_base, row_of_tile):
    tm = xn.shape[0]
    n_tiles = tm // MOVE_TILE
    perm = np.array([(r % GROUP_SIZE) * GROUP_SIZE + r // GROUP_SIZE for r in range(N_EXPERTS)])
    bias_rep = jnp.broadcast_to(bias.astype(F32)[perm][:, None], (N_EXPERTS, LANES))
    ls, w, segn, segc, cnt = _route(lg_t, bias_rep)

    total = cnt[:, 0]
    padded = (total + EXPERT_BLOCK - 1) // EXPERT_BLOCK * EXPERT_BLOCK
    pad_end = jnp.cumsum(padded)
    pad_start = pad_end - padded
    n_slots = tm * TOP_K + (SEG_ALIGN - 1) * N_EXPERTS * n_tiles + N_EXPERTS * EXPERT_BLOCK
    n_blocks = -(-n_slots // EXPERT_BLOCK)
    starts = jnp.arange(n_blocks, dtype=I32) * EXPERT_BLOCK
    region = jnp.minimum(jnp.sum((pad_end[None, :] <= starts[:, None]).astype(I32), axis=1), N_EXPERTS - 1)
    block_e = jnp.asarray(perm, I32)[region]
    n_used = (pad_end[-1:] // EXPERT_BLOCK).astype(I32)
    after = pad_end[region] // EXPERT_BLOCK
    next_e = jnp.where(after < n_used[0], block_e[jnp.minimum(after, n_blocks - 1)], -1).astype(I32)
    segn = segn[:, :, 0].reshape(-1)
    segd = (segc[:, :, 0] + pad_start[None, :]).reshape(-1).astype(I32)

    xs = _dispatch(segn, segd, (pad_start + total).astype(I32), (padded - total).astype(I32), ls, hp,
                   n_blocks * EXPERT_BLOCK)
    ys = _experts(block_e, n_used, next_e, xs, *expert_weights)
    return _combine(segn, segd, ys, ls.T, w.T, hp, xn, s1.astype(BF16), s3.astype(BF16), s2.astype(BF16), g, modr,
                    mod_base, row_of_tile)


_DEINTERLEAVE = np.concatenate([np.arange(0, HEAD_DIM, 2), np.arange(1, HEAD_DIM, 2)])


def _rope_tables(seq, ctx_len):
    t = np.arange(seq)
    n_pair = HEAD_DIM // 4
    inv = jnp.asarray(ROPE_THETA, F32) ** (-jnp.arange(n_pair, dtype=F32) / n_pair)
    r = jnp.asarray(t // GRID_W, F32)
    c = jnp.asarray(t % GRID_W, F32)
    ang = jnp.concatenate([r[:, None] * inv, c[:, None] * inv], axis=-1)
    cos, sin = jnp.cos(ang), jnp.sin(ang)
    cos_t = jnp.tile(jnp.concatenate([cos, cos], axis=-1), (1, LANES // HEAD_DIM))
    sin_t = jnp.tile(jnp.concatenate([-sin, sin], axis=-1), (1, LANES // HEAD_DIM))
    cos_t = jnp.concatenate([jnp.ones((ctx_len, LANES), F32), cos_t], axis=0)
    sin_t = jnp.concatenate([jnp.zeros((ctx_len, LANES), F32), sin_t], axis=0)
    return cos_t, sin_t


def _ab_layout():
    cols, blocks, gain_kind = [], [], []
    for base_q, base_k, base_v, normed in ((0, 512, 640, True), (768, 1280, 1408, False)):
        for jb in range(4):
            cols += [base_q + h * HEAD_DIM + _DEINTERLEAVE for h in (2 * jb, 2 * jb + 1)]
            blocks.append((normed, True, Q_SCALE))
            gain_kind.append("q" if normed else None)
        for kvh in range(2):
            cols += [base_k + kvh * HEAD_DIM + _DEINTERLEAVE] * 2
            blocks.append((normed, True, 1.0))
            gain_kind.append("k" if normed else None)
        for kvh in range(2):
            cols += [base_v + kvh * HEAD_DIM + np.arange(HEAD_DIM)] * 2
            blocks.append((False, False, 1.0))
            gain_kind.append(None)
    return np.concatenate(cols), tuple(blocks), gain_kind


def _c_layout():
    cols, blocks = [], []
    for base, rope, scale in ((0, True, Q_SCALE), (1024, True, 1.0)):
        for h in range(8):
            cols += [base + (2 * h + m) * HEAD_DIM + _DEINTERLEAVE for m in range(2)]
            blocks.append((False, rope, scale))
    for h in range(8):
        cols.append(2048 + h * LANES + np.arange(LANES))
        blocks.append((False, False, 1.0))
    return np.concatenate(cols), tuple(blocks)


def kernel(x, c, ctx, c_ctx, ada_w, ada_b, norm_g, ab_w_in, ab_w_out, a_q_norm, a_k_norm, b_sink,
           c_w_in, c_w_out, c_lambda, c_subln_g, router_w, router_bias, exp_w1, exp_w3, exp_w2,
           sh_w1, sh_w3, sh_w2):
    batch, seq, d = x.shape
    ctx_len = ctx.shape[1]
    depth = ada_w.shape[0]
    assert d == D_MODEL and ctx_len == ROW_TILE and depth == 2
    assert seq % (2 * Q_TILE) == 0 and seq >= 2 * KV_TILE and batch <= 4
    n_p = ctx_len + seq
    tiles = n_p // ROW_TILE
    lat_tiles = seq // ROW_TILE
    t_all = batch * n_p

    cond = jnp.zeros((8, d), F32).at[:batch].set(c).at[4].set(c_ctx)
    modr = _modulation(cond, ada_w, ada_b).reshape(depth * 8 * 6, 1, d)
    row_all = lambda i: jnp.where(i % tiles == 0, 4, i // tiles)
    cos_t, sin_t = _rope_tables(seq, ctx_len)
    rperm = np.array([(r % GROUP_SIZE) * GROUP_SIZE + r // GROUP_SIZE for r in range(N_EXPERTS)])
    xt = jnp.concatenate([ctx, x], axis=1).reshape(t_all, d)

    cols, blocks, gain_kind = _ab_layout()
    w0 = ab_w_in[0][:, cols].astype(BF16)
    gq = jnp.tile(a_q_norm[0][_DEINTERLEAVE], 2)
    gk = jnp.tile(a_k_norm[0][_DEINTERLEAVE], 2)
    ones = jnp.ones((LANES,), F32)
    head_gain = jnp.concatenate([{"q": gq, "k": gk, None: ones}[kind] for kind in gain_kind])[None, :]
    p0 = _project(xt, modr, 0, row_all, norm_g[0, 0][None, :], w0, cos_t, sin_t, head_gain, blocks, tiles)
    common = dict(batch=batch, seq=seq, ctx_len=ctx_len, n_qblocks=4, ctx_queries=True)
    oa = _attention(p0, [], mode="dense", q_col0=0, k_col=lambda j: 4 + j // 2, v_col=lambda j: 6 + j // 2,
                    **common)
    sink = jnp.broadcast_to(b_sink[0].astype(F32)[:, None], (8, LANES))
    ob = _attention(p0, [sink], mode="window", q_col0=8, k_col=lambda j: 12 + j // 2,
                    v_col=lambda j: 14 + j // 2, **common)
    w_out = ab_w_out[0].astype(BF16)
    x1, hp, lg = _out_project([oa, ob], [w_out[:512], w_out[512:]], xt, lambda i: i, modr, 0, row_all,
                              norm_g[0, 1][None, :], norm_g[0, 2][None, :], router_w[0].T[rperm],
                              t_all // ROW_TILE)
    x2 = _moe(x1, hp, lg, router_bias[0], (exp_w1, exp_w3, exp_w2, 0), sh_w1[0], sh_w3[0], sh_w2[0],
              norm_g[0, 3][None, :], modr, 0, lambda i: row_all(i // (ROW_TILE // MOVE_TILE)))

    base1 = 8 * 6
    lambda_init = 0.8 - 0.6 * math.exp(-0.3 * 1)
    cols1, blocks1 = _c_layout()
    w1p = c_w_in[0][:, cols1].astype(BF16)
    p1 = _project(x2, modr, base1, row_all, norm_g[1, 0][None, :], w1p, cos_t, sin_t,
                  jnp.ones((1, w1p.shape[1]), F32), blocks1, tiles)
    lam = jnp.zeros((8, LANES), F32).at[:4, :HEAD_DIM].set(c_lambda[0].astype(F32))
    oc = _attention(p1, [lam, c_subln_g[0][None, :]], mode="diff", batch=batch, seq=seq, ctx_len=ctx_len,
                    n_qblocks=8, q_col0=0, k_col=lambda j: 8 + j, v_col=lambda j: 16 + j, ctx_queries=False,
                    lambda_init=lambda_init, q_parts=2)
    row_lat = lambda i: i // lat_tiles
    x3, hp1, lg1 = _out_project([oc], [c_w_out[0].astype(BF16)], x2,
                                lambda i: (i // lat_tiles) * tiles + 1 + i % lat_tiles, modr, base1, row_lat,
                                norm_g[1, 1][None, :], norm_g[1, 2][None, :], router_w[1].T[rperm],
                                batch * lat_tiles)
    out = _moe(x3, hp1, lg1, router_bias[1], (exp_w1, exp_w3, exp_w2, 1), sh_w1[1], sh_w3[1], sh_w2[1],
               norm_g[1, 3][None, :], modr, base1, lambda i: row_lat(i // (ROW_TILE // MOVE_TILE)))
    return out.reshape(batch, seq, d)
```

```python
import functools
import math

import numpy as np
import jax
import jax.numpy as jnp
from jax import lax
from jax.experimental import pallas as pl
from jax.experimental.pallas import tpu as pltpu

F32 = jnp.float32
BF16 = jnp.bfloat16
U32 = jnp.uint32
I32 = jnp.int32
HIGHEST = lax.Precision.HIGHEST

D_MODEL = 1024
HEAD_DIM = 64
LANES = 128
SUBLANES = 8
GRID_W = 64
ROPE_THETA = 10000.0
EPS = 1e-6
NEG_INF = -1e30
WINDOW = 128
N_EXPERTS = 64
TOP_K = 8
N_GROUPS = 8
TOPK_GROUPS = 4
GROUP_SIZE = N_EXPERTS // N_GROUPS
ROUTED_SCALE = 2.5
LOG2E = 1.4426950408889634
Q_SCALE = HEAD_DIM ** -0.5 * LOG2E

ROW_TILE = 256
Q_TILE = 256
KV_TILE = 256
LAT_KV_TILE = 256
MOVE_TILE = 256
EXPERT_BLOCK = 1024
EXPERT_SLAB = 1024
SEG_ALIGN = 8
RUN_PIECES = (256, 128, 64, 32, 16, 8)
WAIT_PIECES = (2048, 1024, 512) + RUN_PIECES
STAGE_ROWS = 2560
STAGE_CHUNK = 512
HALF = D_MODEL // 2
VMEM_LIMIT = 48 * 1024 * 1024


def _cparams(sem):
    return pltpu.CompilerParams(dimension_semantics=sem, vmem_limit_bytes=VMEM_LIMIT)


def _rms(x, g):
    ms = jnp.mean(x * x, axis=-1, keepdims=True)
    return x * lax.rsqrt(ms + EPS) * g


def _pack_halves(h):
    lo = lax.bitcast_convert_type(h[:, :HALF].astype(BF16).astype(F32), U32) >> 16
    hi = lax.bitcast_convert_type(h[:, HALF:].astype(BF16).astype(F32), U32) & jnp.uint32(0xFFFF0000)
    return hi | lo


def _unpack_halves(u):
    lo = lax.bitcast_convert_type(u << 16, F32)
    hi = lax.bitcast_convert_type(u & jnp.uint32(0xFFFF0000), F32)
    return lo, hi


def _mod_kernel(c_ref, w_ref, b_ref, o_ref):
    c = c_ref[...]
    sc = c * jax.nn.sigmoid(c)
    o_ref[0] = jnp.dot(sc, w_ref[0], precision=HIGHEST, preferred_element_type=F32) + b_ref[0]


def _modulation(cond, ada_w, ada_b):
    depth, d, n = ada_w.shape
    nt = 1536
    return pl.pallas_call(
        _mod_kernel,
        out_shape=jax.ShapeDtypeStruct((depth, 8, n), F32),
        grid=(depth, n // nt),
        in_specs=[pl.BlockSpec((8, d), lambda l, j: (0, 0)),
                  pl.BlockSpec((1, d, nt), lambda l, j: (l, 0, j)),
                  pl.BlockSpec((1, 1, nt), lambda l, j: (l, 0, j))],
        out_specs=pl.BlockSpec((1, 8, nt), lambda l, j: (l, 0, j)),
        compiler_params=_cparams(("arbitrary", "arbitrary")),
        name="ada_mod",
    )(cond, ada_w, ada_b.reshape(depth, 1, n))


def _proj_kernel(x_ref, sh_ref, sc_ref, g_ref, w_ref, cos_ref, sin_ref, hg_ref, gm_ref, o_ref, *, blocks):
    h = _rms(x_ref[...], g_ref[...]) * (1.0 + sc_ref[...]) + sh_ref[...]
    y = jnp.dot(h.astype(BF16), w_ref[...], preferred_element_type=F32)
    lane = lax.broadcasted_iota(I32, (x_ref.shape[0], LANES), 1)
    first_half = (lane % HEAD_DIM) < (HEAD_DIM // 2)
    for jb, (norm, rope, scale) in enumerate(blocks):
        cols = slice(jb * LANES, (jb + 1) * LANES)
        yb = y[:, cols]
        if norm:
            ms = jnp.dot(yb * yb, gm_ref[...], precision=HIGHEST, preferred_element_type=F32)
            yb = yb * lax.rsqrt(ms + EPS) * hg_ref[:, cols]
        if rope:
            swapped = jnp.where(first_half, pltpu.roll(yb, LANES - HEAD_DIM // 2, 1),
                                pltpu.roll(yb, HEAD_DIM // 2, 1))
            yb = yb * cos_ref[...] + swapped * sin_ref[...]
        if scale != 1.0:
            yb = yb * scale
        o_ref[:, cols] = yb.astype(BF16)


def _project(x, modr, mod_base, row_of_tile, g, w, cos_t, sin_t, head_gain, blocks, tiles_per_batch):
    t, d = x.shape
    n = w.shape[1]
    group_mean = jnp.asarray(np.kron(np.eye(LANES // HEAD_DIM), np.full((HEAD_DIM, HEAD_DIM), 1.0 / HEAD_DIM)), F32)
    return pl.pallas_call(
        functools.partial(_proj_kernel, blocks=blocks),
        out_shape=jax.ShapeDtypeStruct((t, n), BF16),
        grid=(t // ROW_TILE,),
        in_specs=[pl.BlockSpec((ROW_TILE, d), lambda i: (i, 0)),
                  pl.BlockSpec((None, 1, d), lambda i: (mod_base + row_of_tile(i) * 6 + 0, 0, 0)),
                  pl.BlockSpec((None, 1, d), lambda i: (mod_base + row_of_tile(i) * 6 + 1, 0, 0)),
                  pl.BlockSpec((1, d), lambda i: (0, 0)),
                  pl.BlockSpec((d, n), lambda i: (0, 0)),
                  pl.BlockSpec((ROW_TILE, LANES), lambda i: (i % tiles_per_batch, 0)),
                  pl.BlockSpec((ROW_TILE, LANES), lambda i: (i % tiles_per_batch, 0)),
                  pl.BlockSpec((1, n), lambda i: (0, 0)),
                  pl.BlockSpec((LANES, LANES), lambda i: (0, 0))],
        out_specs=pl.BlockSpec((ROW_TILE, n), lambda i: (i, 0)),
        compiler_params=_cparams(("arbitrary",)),
        name="prenorm_proj",
    )(x, modr, modr, g, w, cos_t, sin_t, head_gain, group_mean)


def _attn_kernel(*refs, mode, queries, seq, ctx_len, lambda_init, q_parts):
    q_refs, refs = refs[:q_parts], refs[q_parts:]
    if mode == "window":
        k_ref, v_ref, sink_ref, o_ref = refs
    elif mode == "diff":
        k_ref, v_ref, lam_ref, sg_ref, o_ref = refs
    else:
        k_ref, v_ref, o_ref = refs
    tq = q_parts * q_refs[0].shape[0]
    j = pl.program_id(1)
    qi = pl.program_id(2)

    q = q_refs[0][...] if q_parts == 1 else jnp.concatenate([r[...] for r in q_refs], axis=0)
    lane = lax.broadcasted_iota(I32, (tq, LANES), 1)
    low = lane < HEAD_DIM
    zero = jnp.zeros_like(q)
    q2 = jnp.concatenate([jnp.where(low, q, zero), jnp.where(low, zero, q)], axis=0)

    if mode == "window":
        s0 = sink_ref[pl.ds(2 * j, 1), :]
        s1 = sink_ref[pl.ds(2 * j + 1, 1), :]
        row = lax.broadcasted_iota(I32, (2 * tq, LANES), 0)
        sink = jnp.where(row < tq, s0, s1) * LOG2E
        m0 = sink
    else:
        m0 = jnp.full((2 * tq, LANES), NEG_INF, F32)
    state = (m0, jnp.zeros((2 * tq, LANES), F32), jnp.zeros((2 * tq, LANES), F32))

    def chunk(state, start, valid, size=KV_TILE):
        m_prev, l_prev, acc_prev = state
        k = k_ref[pl.ds(start, size), :]
        v = v_ref[pl.ds(start, size), :]
        s = lax.dot_general(q2, k, (((1,), (1,)), ((), ())), preferred_element_type=F32)
        if valid is not None:
            s = jnp.where(valid, s, NEG_INF)
        m_new = jnp.maximum(m_prev, jnp.max(s, axis=1, keepdims=True))
        alpha = jnp.exp2(m_prev - m_new)
        p = jnp.exp2(s - jnp.concatenate([m_new] * (size // LANES), axis=1))
        part = p[:, :LANES]
        for c in range(1, size // LANES):
            part = part + p[:, c * LANES:(c + 1) * LANES]
        return (m_new, alpha * l_prev + part,
                alpha * acc_prev + jnp.dot(p.astype(BF16), v, preferred_element_type=F32))

    state = chunk(state, 0, None)
    if queries == "latent" and mode == "window":
        n_win = (tq + 2 * WINDOW) // KV_TILE
        q0 = qi * tq
        kstart = jnp.clip(q0 - WINDOW, 0, seq - n_win * KV_TILE)
        r = lax.broadcasted_iota(I32, (2 * tq, KV_TILE), 0)
        qpos = q0 + jnp.where(r >= tq, r - tq, r)
        col = lax.broadcasted_iota(I32, (2 * tq, KV_TILE), 1)
        for w in range(n_win):
            kpos = kstart + w * KV_TILE + col
            state = chunk(state, pl.multiple_of(ctx_len + kstart + w * KV_TILE, WINDOW),
                          jnp.abs(qpos - kpos) <= WINDOW)
    elif queries == "latent":
        for c in range(seq // LAT_KV_TILE):
            state = chunk(state, ctx_len + c * LAT_KV_TILE, None, LAT_KV_TILE)

    m_fin, l_part, acc = state
    l = jnp.sum(l_part, axis=1, keepdims=True)
    if mode == "window":
        l = l + jnp.exp2(sink - m_fin)[:, :1]
    o2 = acc / l
    if mode == "diff":
        lp = lam_ref[...]
        lam = (jnp.exp(jnp.sum(lp[0:1] * lp[1:2], axis=1, keepdims=True))
               - jnp.exp(jnp.sum(lp[2:3] * lp[3:4], axis=1, keepdims=True)) + lambda_init)
        o = o2[:tq] - lam * o2[tq:]
        o = _rms(o, sg_ref[...]) * (1.0 - lambda_init)
    else:
        o = jnp.where(low, o2[:tq], o2[tq:])
    o_ref[...] = o.astype(BF16)


def _attention(p, extra, *, mode, queries, batch, seq, ctx_len, n_qblocks, q_col0, k_col, v_col,
               lambda_init=0.0, q_parts=1):
    n_p = ctx_len + seq
    tiles = n_p // Q_TILE
    tq = q_parts * Q_TILE
    n_rows = seq if queries == "latent" else ctx_len
    qt = n_rows // tq
    q_off = ctx_len // Q_TILE if queries == "latent" else 0
    in_specs = [pl.BlockSpec((Q_TILE, LANES),
                             lambda b, j, qi, part=part: (b * tiles + q_off + qi * q_parts + part, q_col0 + j))
                for part in range(q_parts)]
    in_specs += [pl.BlockSpec((n_p, LANES), lambda b, j, qi: (b, k_col(j))),
                 pl.BlockSpec((n_p, LANES), lambda b, j, qi: (b, v_col(j)))]
    args = [p] * (q_parts + 2)
    for e in extra:
        in_specs.append(pl.BlockSpec(e.shape, lambda b, j, qi: (0, 0)))
        args.append(e)
    kern = functools.partial(_attn_kernel, mode=mode, queries=queries, seq=seq, ctx_len=ctx_len,
                             lambda_init=lambda_init, q_parts=q_parts)
    return pl.pallas_call(
        kern,
        out_shape=jax.ShapeDtypeStruct((batch * n_rows, n_qblocks * LANES), BF16),
        grid=(batch, n_qblocks, qt),
        in_specs=in_specs,
        out_specs=pl.BlockSpec((tq, LANES), lambda b, j, qi: (b * qt + qi, j)),
        compiler_params=_cparams(("arbitrary", "arbitrary", "arbitrary")),
        name="attn_" + mode + "_" + queries,
    )(*args)


def _out_kernel(*refs, n_o, tiles):
    n_in = 2 * n_o if tiles else n_o
    o_refs = refs[:n_in]
    w_refs = refs[n_in:n_in + n_o]
    x_ref, g1_ref, gate_ref, g2_ref, sh_ref, sc_ref, rw_ref, xn_ref, hp_ref, lg_ref = refs[n_in + n_o:]
    is_ctx = (pl.program_id(0) % tiles == 0) if tiles else None

    def mixer_rows(a, rows):
        if not tiles:
            return o_refs[a][rows, :]
        return jnp.where(is_ctx, o_refs[2 * a][rows, :], o_refs[2 * a + 1][rows, :])

    slab = LANES
    for r0 in range(0, x_ref.shape[0], slab):
        rows = slice(r0, r0 + slab)
        y = jnp.dot(mixer_rows(0, rows), w_refs[0][...], preferred_element_type=F32)
        for a in range(1, n_o):
            y = y + jnp.dot(mixer_rows(a, rows), w_refs[a][...], preferred_element_type=F32)
        xn = x_ref[rows, :] + gate_ref[...] * _rms(y, g1_ref[...])
        xn_ref[rows, :] = xn
        h = _rms(xn, g2_ref[...]) * (1.0 + sc_ref[...]) + sh_ref[...]
        hp_ref[rows, :] = _pack_halves(h)
        lg_ref[:, rows] = lax.dot_general(rw_ref[...], h, (((1,), (1,)), ((), ())), precision=HIGHEST,
                                          preferred_element_type=F32)


def _out_project(os_, ws, x, x_tile, modr, mod_base, row_of_tile, g1, g2, rw_t, n_tiles, tiles=None):
    d = x.shape[1]
    n_o = len(os_)
    tm = n_tiles * ROW_TILE
    mspec = lambda which: pl.BlockSpec((None, 1, d), lambda i: (mod_base + row_of_tile(i) * 6 + which, 0, 0))
    if tiles:
        o_specs, o_args = [], []
        for o_ctx, o_lat in os_:
            o_specs += [pl.BlockSpec((ROW_TILE, o_ctx.shape[1]), lambda i: (i // tiles, 0)),
                        pl.BlockSpec((ROW_TILE, o_lat.shape[1]),
                                     lambda i: ((i // tiles) * (tiles - 1) + jnp.maximum(i % tiles - 1, 0), 0))]
            o_args += [o_ctx, o_lat]
    else:
        o_specs = [pl.BlockSpec((ROW_TILE, o.shape[1]), lambda i: (i, 0)) for o in os_]
        o_args = list(os_)
    in_specs = (o_specs
                + [pl.BlockSpec(w.shape, lambda i: (0, 0)) for w in ws]
                + [pl.BlockSpec((ROW_TILE, d), lambda i: (x_tile(i), 0)),
                   pl.BlockSpec((1, d), lambda i: (0, 0)), mspec(2),
                   pl.BlockSpec((1, d), lambda i: (0, 0)), mspec(3), mspec(4),
                   pl.BlockSpec(rw_t.shape, lambda i: (0, 0))])
    return pl.pallas_call(
        functools.partial(_out_kernel, n_o=n_o, tiles=tiles),
        out_shape=(jax.ShapeDtypeStruct((tm, d), F32), jax.ShapeDtypeStruct((tm, HALF), U32),
                   jax.ShapeDtypeStruct((N_EXPERTS, tm), F32)),
        grid=(n_tiles,),
        in_specs=in_specs,
        out_specs=(pl.BlockSpec((ROW_TILE, d), lambda i: (i, 0)),
                   pl.BlockSpec((ROW_TILE, HALF), lambda i: (i, 0)),
                   pl.BlockSpec((N_EXPERTS, ROW_TILE), lambda i: (0, i))),
        compiler_params=_cparams(("arbitrary",)),
        name="out_proj",
    )(*o_args, *ws, x, g1, modr, g2, modr, modr, rw_t)


def _route_kernel(lg_ref, bias_ref, tri_ref, ltri_ref, ls_ref, w_ref, segn_ref, segc_ref, cnt_ref, carry_ref):
    i = pl.program_id(0)

    @pl.when(i == 0)
    def _():
        carry_ref[...] = jnp.zeros(carry_ref.shape, F32)

    tr = lg_ref.shape[1]
    score = jax.nn.sigmoid(lg_ref[...])
    sel = score + bias_ref[...][:, :1]
    sel_j = [sel[j * GROUP_SIZE:(j + 1) * GROUP_SIZE] for j in range(GROUP_SIZE)]
    sc_j = [score[j * GROUP_SIZE:(j + 1) * GROUP_SIZE] for j in range(GROUP_SIZE)]
    gi = lax.broadcasted_iota(I32, (N_GROUPS, tr), 0)

    m1 = sel_j[0]
    m2 = jnp.full_like(m1, -jnp.inf)
    for j in range(1, GROUP_SIZE):
        m2 = jnp.maximum(m2, jnp.minimum(m1, sel_j[j]))
        m1 = jnp.maximum(m1, sel_j[j])
    gs = m1 + m2

    grank = jnp.zeros((N_GROUPS, tr), I32)
    for gp in range(N_GROUPS):
        rowv = gs[gp:gp + 1, :]
        grank = grank + jnp.where(gi > gp, jnp.where(rowv >= gs, 1, 0), jnp.where(rowv > gs, 1, 0))
    gmask = grank < TOPK_GROUPS
    val_j = [jnp.where(gmask, sel_j[j], NEG_INF) for j in range(GROUP_SIZE)]

    rank_j = [jnp.zeros((N_GROUPS, tr), I32) for _ in range(GROUP_SIZE)]
    for gp in range(N_GROUPS):
        after = gi > gp
        not_before = gi >= gp
        for jp in range(GROUP_SIZE):
            rowv = val_j[jp][gp:gp + 1, :]
            for j in range(GROUP_SIZE):
                tie = after if jp >= j else not_before
                rank_j[j] = rank_j[j] + jnp.where(tie, jnp.where(rowv >= val_j[j], 1, 0),
                                                  jnp.where(rowv > val_j[j], 1, 0))

    chosen = jnp.concatenate([jnp.where(rank_j[j] < TOP_K, 1.0, 0.0) for j in range(GROUP_SIZE)], axis=0)
    n_run = jnp.floor((jnp.sum(chosen, axis=1, keepdims=True) + (SEG_ALIGN - 1.0)) * (1.0 / SEG_ALIGN)) * SEG_ALIGN
    n_run = jnp.broadcast_to(n_run, (N_EXPERTS, LANES))
    run_start = jnp.dot(ltri_ref[...], n_run.astype(BF16), preferred_element_type=F32)
    local = jnp.dot(chosen.astype(BF16), tri_ref[...], preferred_element_type=F32) + run_start[:, :1]
    segn_ref[0] = n_run.astype(I32)
    segc_ref[0] = carry_ref[...].astype(I32)
    carry_ref[...] = carry_ref[...] + n_run
    cnt_ref[...] = carry_ref[...].astype(I32)

    w_rows, ls_rows = [], []
    for k in range(TOP_K):
        w_acc = jnp.zeros((N_GROUPS, tr), F32)
        p_acc = jnp.zeros((N_GROUPS, tr), F32)
        for j in range(GROUP_SIZE):
            hit = rank_j[j] == k
            w_acc = w_acc + jnp.where(hit, sc_j[j], 0.0)
            p_acc = p_acc + jnp.where(hit, local[j * GROUP_SIZE:(j + 1) * GROUP_SIZE], 0.0)
        w_rows.append(jnp.sum(w_acc, axis=0, keepdims=True))
        ls_rows.append(jnp.sum(p_acc, axis=0, keepdims=True))
    w_all = jnp.concatenate(w_rows, axis=0)
    w_ref[...] = w_all / jnp.sum(w_all, axis=0, keepdims=True) * ROUTED_SCALE
    ls_ref[...] = jnp.concatenate(ls_rows, axis=0).astype(I32)


def _route(lg_t, bias_rep):
    tm = lg_t.shape[1]
    n_tiles = tm // MOVE_TILE
    tri = jnp.asarray(np.triu(np.ones((MOVE_TILE, MOVE_TILE), np.float32), 1), BF16)
    ltri = jnp.asarray(np.tril(np.ones((N_EXPERTS, N_EXPERTS), np.float32), -1), BF16)
    tok = lambda i: (0, i)
    per_tile = pl.BlockSpec((1, N_EXPERTS, LANES), lambda i: (i, 0, 0))
    return pl.pallas_call(
        _route_kernel,
        out_shape=(jax.ShapeDtypeStruct((TOP_K, tm), I32), jax.ShapeDtypeStruct((TOP_K, tm), F32),
                   jax.ShapeDtypeStruct((n_tiles, N_EXPERTS, LANES), I32),
                   jax.ShapeDtypeStruct((n_tiles, N_EXPERTS, LANES), I32),
                   jax.ShapeDtypeStruct((N_EXPERTS, LANES), I32)),
        grid=(n_tiles,),
        in_specs=[pl.BlockSpec((N_EXPERTS, MOVE_TILE), tok),
                  pl.BlockSpec((N_EXPERTS, LANES), lambda i: (0, 0)),
                  pl.BlockSpec((MOVE_TILE, MOVE_TILE), lambda i: (0, 0)),
                  pl.BlockSpec((N_EXPERTS, N_EXPERTS), lambda i: (0, 0))],
        out_specs=(pl.BlockSpec((TOP_K, MOVE_TILE), tok), pl.BlockSpec((TOP_K, MOVE_TILE), tok),
                   per_tile, per_tile, pl.BlockSpec((N_EXPERTS, LANES), lambda i: (0, 0))),
        scratch_shapes=[pltpu.VMEM((N_EXPERTS, LANES), F32)],
        compiler_params=_cparams(("arbitrary",)),
        name="route",
    )(lg_t, bias_rep, tri, ltri)


def _start_pieces(n, src, dst, make, pieces):
    for size in pieces:
        above = n & (-2 * size)

        @pl.when((n & size) != 0)
        def _():
            make(pl.multiple_of(src + above, SEG_ALIGN), pl.multiple_of(dst + above, SEG_ALIGN), size).start()


def _start_run_copies(i, segn_ref, segd_ref, make, straight_line=False):
    split = RUN_PIECES.index(64)

    def per_run(r, src):
        n = segn_ref[i * N_EXPERTS + r]
        dst = segd_ref[i * N_EXPERTS + r]

        @pl.when(n >= RUN_PIECES[split - 1])
        def _():
            _start_pieces(n, src, dst, make, RUN_PIECES[:split])

        _start_pieces(n, src, dst, make, RUN_PIECES[split:])
        return src + n

    if not straight_line:
        return lax.fori_loop(0, N_EXPERTS, per_run, 0, unroll=2)
    src = 0
    for r in range(N_EXPERTS):
        src = per_run(r, src)
    return src


def _wait_rows(total, make):
    for size in WAIT_PIECES:
        @pl.when((total & size) != 0)
        def _():
            make(0, 0, size).wait()


def _fill_copies(zstart_ref, zlen_ref, make):
    pieces = tuple(p for p in WAIT_PIECES if p < EXPERT_BLOCK)

    def per_expert(r, carry):
        n = zlen_ref[r]
        _start_pieces(n, 0, zstart_ref[r], make, pieces)
        _wait_rows(n, make)
        return carry

    lax.fori_loop(0, N_EXPERTS, per_expert, 0)


def _dispatch_kernel(segn_ref, segd_ref, zstart_ref, zlen_ref, ls_ref, hp_ref, xs_hbm, stage_ref, zero_ref,
                     rows_ref, sem, zsem):
    i = pl.program_id(0)
    buf = i % 2
    lo, hi = _unpack_halves(hp_ref[...])
    lo = lo.astype(BF16)
    hi = hi.astype(BF16)
    ls = ls_ref[...]
    pack_rows = 16
    ls16 = [jnp.tile(jnp.broadcast_to(ls[k:k + 1, :], (pack_rows, MOVE_TILE)).astype(jnp.int16),
                     (STAGE_CHUNK // pack_rows, 1)) for k in range(TOP_K)]
    slot0 = lax.broadcasted_iota(I32, (STAGE_CHUNK, MOVE_TILE), 0).astype(jnp.int16)
    one = jnp.ones((STAGE_CHUNK, MOVE_TILE), BF16)
    for c in range(STAGE_ROWS // STAGE_CHUNK):
        slot = slot0 + jnp.int16(c * STAGE_CHUNK)
        onehot = jnp.zeros((STAGE_CHUNK, MOVE_TILE), BF16)
        for k in range(TOP_K):
            onehot = jnp.where(ls16[k] == slot, one, onehot)
        a = lax.bitcast_convert_type(jnp.dot(onehot, lo, preferred_element_type=F32), U32) >> 16
        b = lax.bitcast_convert_type(jnp.dot(onehot, hi, preferred_element_type=F32), U32)
        stage_ref[buf, c * STAGE_CHUNK:(c + 1) * STAGE_CHUNK, :] = b | a

    def to_slots(which):
        def make(src, dst, size):
            return pltpu.make_async_copy(stage_ref.at[which, pl.ds(src, size), :],
                                         xs_hbm.at[pl.ds(dst, size), :], sem.at[which])
        return make

    rows_ref[buf] = _start_run_copies(i, segn_ref, segd_ref, to_slots(buf))

    @pl.when(i > 0)
    def _():
        _wait_rows(rows_ref[1 - buf], to_slots(1 - buf))

    @pl.when(i == pl.num_programs(0) - 1)
    def _():
        _wait_rows(rows_ref[buf], to_slots(buf))

    @pl.when(i == 0)
    def _():
        zero_ref[...] = jnp.zeros(zero_ref.shape, U32)

        def zeros_to_slots(src, dst, size):
            return pltpu.make_async_copy(zero_ref.at[pl.ds(0, size), :], xs_hbm.at[pl.ds(dst, size), :], zsem)

        _fill_copies(zstart_ref, zlen_ref, zeros_to_slots)


def _dispatch(segn, segd, zstart, zlen, ls, hp, n_slots):
    tm = hp.shape[0]
    tile = lambda i, *_: (i, 0)
    return pl.pallas_call(
        _dispatch_kernel,
        out_shape=jax.ShapeDtypeStruct((n_slots, HALF), U32),
        grid_spec=pltpu.PrefetchScalarGridSpec(
            num_scalar_prefetch=4,
            grid=(tm // MOVE_TILE,),
            in_specs=[pl.BlockSpec((TOP_K, MOVE_TILE), lambda i, *_: (0, i)),
                      pl.BlockSpec((MOVE_TILE, HALF), tile)],
            out_specs=pl.BlockSpec(memory_space=pl.ANY),
            scratch_shapes=[pltpu.VMEM((2, STAGE_ROWS, HALF), U32), pltpu.VMEM((EXPERT_BLOCK // 2, HALF), U32),
                            pltpu.SMEM((2,), I32), pltpu.SemaphoreType.DMA((2,)), pltpu.SemaphoreType.DMA]),
        compiler_params=_cparams(("arbitrary",)),
        name="dispatch",
    )(segn, segd, zstart, zlen, ls, hp)


def _expert_kernel(be_ref, nu_ref, nxt_ref, xs_ref, w1_hbm, w3_hbm, w2_hbm, ys_ref, w1f_ref, w3f_ref, w2f_ref,
                   w1b_ref, w3b_ref, w2b_ref, run_ref, sem, *, layer):
    i = pl.program_id(0)

    def fetch(e, slot):
        return [pltpu.make_async_copy(src.at[layer, e], dst.at[slot], sem.at[slot])
                for src, dst in ((w1_hbm, w1f_ref), (w3_hbm, w3f_ref), (w2_hbm, w2f_ref))]

    @pl.when(i == 0)
    def _():
        run_ref[0] = 0
        for cp in fetch(be_ref[0], 0):
            cp.start()

    first = jnp.logical_or(i == 0, be_ref[i] != be_ref[jnp.maximum(i - 1, 0)])

    @pl.when(jnp.logical_and(first, i < nu_ref[0]))
    def _():
        slot = run_ref[0] % 2
        for cp in fetch(be_ref[i], slot):
            cp.wait()
        w1b_ref[...] = w1f_ref[slot].astype(BF16)
        w3b_ref[...] = w3f_ref[slot].astype(BF16)
        w2b_ref[...] = w2f_ref[slot].astype(BF16)
        nxt = nxt_ref[i]

        @pl.when(nxt >= 0)
        def _():
            for cp in fetch(nxt, 1 - slot):
                cp.start()

        run_ref[0] = run_ref[0] + 1

    @pl.when(i < nu_ref[0])
    def _():
        for rows in (slice(r0, r0 + EXPERT_SLAB) for r0 in range(0, EXPERT_BLOCK, EXPERT_SLAB)):
            lo, hi = _unpack_halves(xs_ref[rows, :])
            lo = lo.astype(BF16)
            hi = hi.astype(BF16)
            h1 = (jnp.dot(lo, w1b_ref[:HALF, :], preferred_element_type=F32)
                  + jnp.dot(hi, w1b_ref[HALF:, :], preferred_element_type=F32))
            h3 = (jnp.dot(lo, w3b_ref[:HALF, :], preferred_element_type=F32)
                  + jnp.dot(hi, w3b_ref[HALF:, :], preferred_element_type=F32))
            g = (h1 * jax.nn.sigmoid(h1) * h3).astype(BF16)
            y = jnp.dot(g, w2b_ref[...], preferred_element_type=F32)
            ys_ref[rows, :] = _pack_halves(y)


def _experts(block_e, n_used, next_e, xs, w1, w3, w2, layer):
    n_slots = xs.shape[0]
    n_blocks = n_slots // EXPERT_BLOCK
    d, eh = w1.shape[2], w1.shape[3]
    rows = lambda i, be, nu, nx: (jnp.minimum(i, nu[0] - 1), 0)
    anyspace = pl.BlockSpec(memory_space=pl.ANY)
    return pl.pallas_call(
        functools.partial(_expert_kernel, layer=layer),
        out_shape=jax.ShapeDtypeStruct((n_slots, HALF), U32),
        grid_spec=pltpu.PrefetchScalarGridSpec(
            num_scalar_prefetch=3,
            grid=(n_blocks,),
            in_specs=[pl.BlockSpec((EXPERT_BLOCK, HALF), rows), anyspace, anyspace, anyspace],
            out_specs=pl.BlockSpec((EXPERT_BLOCK, HALF), rows),
            scratch_shapes=[pltpu.VMEM((2, d, eh), F32), pltpu.VMEM((2, d, eh), F32), pltpu.VMEM((2, eh, d), F32),
                            pltpu.VMEM((d, eh), BF16), pltpu.VMEM((d, eh), BF16), pltpu.VMEM((eh, d), BF16),
                            pltpu.SMEM((1,), I32), pltpu.SemaphoreType.DMA((2,))]),
        compiler_params=_cparams(("arbitrary",)),
        name="experts",
    )(block_e, n_used, next_e, xs, w1, w3, w2)


def _combine_kernel(segn_ref, segd_ref, ys_hbm, lst_ref, wt_ref, hp_ref, x_ref, s1_ref, s3_ref, s2_ref, g_ref,
                    gate_ref, o_ref, stage_ref, rows_ref, sem):
    i = pl.program_id(0)
    buf = i % 2

    def from_slots(which):
        def make(src, dst, size):
            return pltpu.make_async_copy(ys_hbm.at[pl.ds(dst, size), :],
                                         stage_ref.at[which, pl.ds(src, size), :], sem.at[which])
        return make

    @pl.when(i == 0)
    def _():
        stage_ref[...] = jnp.zeros(stage_ref.shape, U32)
        rows_ref[0] = _start_run_copies(0, segn_ref, segd_ref, from_slots(0))

    last = pl.num_programs(0) - 1
    rows_ref[1 - buf] = _start_run_copies(jnp.minimum(i + 1, last), segn_ref, segd_ref, from_slots(1 - buf),
                                          straight_line=True)

    lo, hi = _unpack_halves(hp_ref[...])
    lo = lo.astype(BF16)
    hi = hi.astype(BF16)
    h1 = (jnp.dot(lo, s1_ref[:HALF, :], preferred_element_type=F32)
          + jnp.dot(hi, s1_ref[HALF:, :], preferred_element_type=F32))
    h3 = (jnp.dot(lo, s3_ref[:HALF, :], preferred_element_type=F32)
          + jnp.dot(hi, s3_ref[HALF:, :], preferred_element_type=F32))
    y = jnp.dot((h1 * jax.nn.sigmoid(h1) * h3).astype(BF16), s2_ref[...], preferred_element_type=F32)

    _wait_rows(rows_ref[buf], from_slots(buf))

    @pl.when(i == last)
    def _():
        _wait_rows(rows_ref[1 - buf], from_slots(1 - buf))

    lst = lst_ref[...]
    wt = wt_ref[...]
    lst16 = [jnp.broadcast_to(lst[:, k:k + 1], (MOVE_TILE, STAGE_CHUNK)).astype(jnp.int16) for k in range(TOP_K)]
    wt16 = [jnp.broadcast_to(wt[:, k:k + 1], (MOVE_TILE, STAGE_CHUNK)).astype(BF16) for k in range(TOP_K)]
    slot0 = lax.broadcasted_iota(I32, (MOVE_TILE, STAGE_CHUNK), 1).astype(jnp.int16)
    r_lo = jnp.zeros((MOVE_TILE, HALF), F32)
    r_hi = jnp.zeros((MOVE_TILE, HALF), F32)
    for c in range(STAGE_ROWS // STAGE_CHUNK):
        slot = slot0 + jnp.int16(c * STAGE_CHUNK)
        wsel = jnp.zeros((MOVE_TILE, STAGE_CHUNK), BF16)
        for k in range(TOP_K):
            wsel = jnp.where(lst16[k] == slot, wt16[k], wsel)
        a, b = _unpack_halves(stage_ref[buf, c * STAGE_CHUNK:(c + 1) * STAGE_CHUNK, :])
        r_lo = r_lo + jnp.dot(wsel, a.astype(BF16), preferred_element_type=F32)
        r_hi = r_hi + jnp.dot(wsel, b.astype(BF16), preferred_element_type=F32)
    y = y + jnp.concatenate([r_lo, r_hi], axis=1)
    o_ref[...] = x_ref[...] + gate_ref[...] * _rms(y, g_ref[...])


def _combine(segn, segd, ys, lst, wt, hp, x, s1, s3, s2, g, modr, mod_base, row_of_tile):
    tm, d = x.shape
    full = lambda a: pl.BlockSpec(a.shape, lambda i, *_: (0, 0))
    tile = lambda i, *_: (i, 0)
    return pl.pallas_call(
        _combine_kernel,
        out_shape=jax.ShapeDtypeStruct((tm, d), F32),
        grid_spec=pltpu.PrefetchScalarGridSpec(
            num_scalar_prefetch=2,
            grid=(tm // MOVE_TILE,),
            in_specs=[pl.BlockSpec(memory_space=pl.ANY),
                      pl.BlockSpec((MOVE_TILE, TOP_K), tile), pl.BlockSpec((MOVE_TILE, TOP_K), tile),
                      pl.BlockSpec((MOVE_TILE, HALF), tile), pl.BlockSpec((MOVE_TILE, d), tile),
                      full(s1), full(s3), full(s2), full(g),
                      pl.BlockSpec((None, 1, d), lambda i, *_: (mod_base + row_of_tile(i) * 6 + 5, 0, 0))],
            out_specs=pl.BlockSpec((MOVE_TILE, d), tile),
            scratch_shapes=[pltpu.VMEM((2, STAGE_ROWS, HALF), U32), pltpu.SMEM((2,), I32),
                            pltpu.SemaphoreType.DMA((2,))]),
        compiler_params=_cparams(("arbitrary",)),
        name="combine",
    )(segn, segd, ys, lst, wt, hp, x, s1, s3, s2, g, modr)


def _moe(xn, hp, lg_t, bias, expert_weights, s1, s3, s2, g, modr, mod_base, row_of_tile):
    tm = xn.shape[0]
    n_tiles = tm // MOVE_TILE
    perm = np.array([(r % GROUP_SIZE) * GROUP_SIZE + r // GROUP_SIZE for r in range(N_EXPERTS)])
    bias_rep = jnp.broadcast_to(bias.astype(F32)[perm][:, None], (N_EXPERTS, LANES))
    ls, w, segn, segc, cnt = _route(lg_t, bias_rep)

    total = cnt[:, 0]
    padded = (total + EXPERT_BLOCK - 1) // EXPERT_BLOCK * EXPERT_BLOCK
    pad_end = jnp.cumsum(padded)
    pad_start = pad_end - padded
    n_slots = tm * TOP_K + (SEG_ALIGN - 1) * N_EXPERTS * n_tiles + N_EXPERTS * EXPERT_BLOCK
    n_blocks = -(-n_slots // EXPERT_BLOCK)
    starts = jnp.arange(n_blocks, dtype=I32) * EXPERT_BLOCK
    region = jnp.minimum(jnp.sum((pad_end[None, :] <= starts[:, None]).astype(I32), axis=1), N_EXPERTS - 1)
    block_e = jnp.asarray(perm, I32)[region]
    n_used = (pad_end[-1:] // EXPERT_BLOCK).astype(I32)
    after = pad_end[region] // EXPERT_BLOCK
    next_e = jnp.where(after < n_used[0], block_e[jnp.minimum(after, n_blocks - 1)], -1).astype(I32)
    segn = segn[:, :, 0].reshape(-1)
    segd = (segc[:, :, 0] + pad_start[None, :]).reshape(-1).astype(I32)

    xs = _dispatch(segn, segd, (pad_start + total).astype(I32), (padded - total).astype(I32), ls, hp,
                   n_blocks * EXPERT_BLOCK)
    ys = _experts(block_e, n_used, next_e, xs, *expert_weights)
    return _combine(segn, segd, ys, ls.T, w.T, hp, xn, s1.astype(BF16), s3.astype(BF16), s2.astype(BF16), g, modr,
                    mod_base, row_of_tile)


_DEINTERLEAVE = np.concatenate([np.arange(0, HEAD_DIM, 2), np.arange(1, HEAD_DIM, 2)])


def _rope_tables(seq, ctx_len):
    t = np.arange(seq)
    n_pair = HEAD_DIM // 4
    inv = jnp.asarray(ROPE_THETA, F32) ** (-jnp.arange(n_pair, dtype=F32) / n_pair)
    r = jnp.asarray(t // GRID_W, F32)
    c = jnp.asarray(t % GRID_W, F32)
    ang = jnp.concatenate([r[:, None] * inv, c[:, None] * inv], axis=-1)
    cos, sin = jnp.cos(ang), jnp.sin(ang)
    cos_t = jnp.tile(jnp.concatenate([cos, cos], axis=-1), (1, LANES // HEAD_DIM))
    sin_t = jnp.tile(jnp.concatenate([-sin, sin], axis=-1), (1, LANES // HEAD_DIM))
    cos_t = jnp.concatenate([jnp.ones((ctx_len, LANES), F32), cos_t], axis=0)
    sin_t = jnp.concatenate([jnp.zeros((ctx_len, LANES), F32), sin_t], axis=0)
    return cos_t, sin_t


def _ab_layout():
    cols, blocks, gain_kind = [], [], []
    for base_q, base_k, base_v, normed in ((0, 512, 640, True), (768, 1280, 1408, False)):
        for jb in range(4):
            cols += [base_q + h * HEAD_DIM + _DEINTERLEAVE for h in (2 * jb, 2 * jb + 1)]
            blocks.append((normed, True, Q_SCALE))
            gain_kind.append("q" if normed else None)
        for kvh in range(2):
            cols += [base_k + kvh * HEAD_DIM + _DEINTERLEAVE] * 2
            blocks.append((normed, True, 1.0))
            gain_kind.append("k" if normed else None)
        for kvh in range(2):
            cols += [base_v + kvh * HEAD_DIM + np.arange(HEAD_DIM)] * 2
            blocks.append((False, False, 1.0))
            gain_kind.append(None)
    return np.concatenate(cols), tuple(blocks), gain_kind


def _c_layout():
    cols, blocks = [], []
    for base, rope, scale in ((0, True, Q_SCALE), (1024, True, 1.0)):
        for h in range(8):
            cols += [base + (2 * h + m) * HEAD_DIM + _DEINTERLEAVE for m in range(2)]
            blocks.append((False, rope, scale))
    for h in range(8):
        cols.append(2048 + h * LANES + np.arange(LANES))
        blocks.append((False, False, 1.0))
    return np.concatenate(cols), tuple(blocks)


def kernel(x, c, ctx, c_ctx, ada_w, ada_b, norm_g, ab_w_in, ab_w_out, a_q_norm, a_k_norm, b_sink,
           c_w_in, c_w_out, c_lambda, c_subln_g, router_w, router_bias, exp_w1, exp_w3, exp_w2,
           sh_w1, sh_w3, sh_w2):
    batch, seq, d = x.shape
    ctx_len = ctx.shape[1]
    depth = ada_w.shape[0]
    assert d == D_MODEL and ctx_len == ROW_TILE and depth == 2
    assert seq % (4 * Q_TILE) == 0 and batch <= 4
    n_p = ctx_len + seq
    tiles = n_p // ROW_TILE
    lat_tiles = seq // ROW_TILE
    t_all = batch * n_p

    cond = jnp.zeros((8, d), F32).at[:batch].set(c).at[4].set(c_ctx)
    modr = _modulation(cond, ada_w, ada_b).reshape(depth * 8 * 6, 1, d)
    row_all = lambda i: jnp.where(i % tiles == 0, 4, i // tiles)
    cos_t, sin_t = _rope_tables(seq, ctx_len)
    rperm = np.array([(r % GROUP_SIZE) * GROUP_SIZE + r // GROUP_SIZE for r in range(N_EXPERTS)])
    xt = jnp.concatenate([ctx, x], axis=1).reshape(t_all, d)

    cols, blocks, gain_kind = _ab_layout()
    w0 = ab_w_in[0][:, cols].astype(BF16)
    gq = jnp.tile(a_q_norm[0][_DEINTERLEAVE], 2)
    gk = jnp.tile(a_k_norm[0][_DEINTERLEAVE], 2)
    ones = jnp.ones((LANES,), F32)
    head_gain = jnp.concatenate([{"q": gq, "k": gk, None: ones}[kind] for kind in gain_kind])[None, :]
    p0 = _project(xt, modr, 0, row_all, norm_g[0, 0][None, :], w0, cos_t, sin_t, head_gain, blocks, tiles)
    common = dict(batch=batch, seq=seq, ctx_len=ctx_len, n_qblocks=4)
    dense = dict(mode="dense", q_col0=0, k_col=lambda j: 4 + j // 2, v_col=lambda j: 6 + j // 2, **common)
    window = dict(mode="window", q_col0=8, k_col=lambda j: 12 + j // 2, v_col=lambda j: 14 + j // 2, **common)
    sink = jnp.broadcast_to(b_sink[0].astype(F32)[:, None], (8, LANES))
    oa = (_attention(p0, [], queries="context", **dense), _attention(p0, [], queries="latent", q_parts=2, **dense))
    ob = (_attention(p0, [sink], queries="context", **window),
          _attention(p0, [sink], queries="latent", **window))
    w_out = ab_w_out[0].astype(BF16)
    x1, hp, lg = _out_project([oa, ob], [w_out[:512], w_out[512:]], xt, lambda i: i, modr, 0, row_all,
                              norm_g[0, 1][None, :], norm_g[0, 2][None, :], router_w[0].T[rperm],
                              t_all // ROW_TILE, tiles=tiles)
    x2 = _moe(x1, hp, lg, router_bias[0], (exp_w1, exp_w3, exp_w2, 0), sh_w1[0], sh_w3[0], sh_w2[0],
              norm_g[0, 3][None, :], modr, 0, lambda i: row_all(i // (ROW_TILE // MOVE_TILE)))

    base1 = 8 * 6
    lambda_init = 0.8 - 0.6 * math.exp(-0.3 * 1)
    cols1, blocks1 = _c_layout()
    w1p = c_w_in[0][:, cols1].astype(BF16)
    p1 = _project(x2, modr, base1, row_all, norm_g[1, 0][None, :], w1p, cos_t, sin_t,
                  jnp.ones((1, w1p.shape[1]), F32), blocks1, tiles)
    lam = jnp.zeros((8, LANES), F32).at[:4, :HEAD_DIM].set(c_lambda[0].astype(F32))
    oc = _attention(p1, [lam, c_subln_g[0][None, :]], mode="diff", queries="latent", batch=batch, seq=seq,
                    ctx_len=ctx_len, n_qblocks=8, q_col0=0, k_col=lambda j: 8 + j, v_col=lambda j: 16 + j,
                    lambda_init=lambda_init, q_parts=4)
    row_lat = lambda i: i // lat_tiles
    x3, hp1, lg1 = _out_project([oc], [c_w_out[0].astype(BF16)], x2,
                                lambda i: (i // lat_tiles) * tiles + 1 + i % lat_tiles, modr, base1, row_lat,
                                norm_g[1, 1][None, :], norm_g[1, 2][None, :], router_w[1].T[rperm],
                                batch * lat_tiles)
    out = _moe(x3, hp1, lg1, router_bias[1], (exp_w1, exp_w3, exp_w2, 1), sh_w1[1], sh_w3[1], sh_w2[1],
               norm_g[1, 3][None, :], modr, base1, lambda i: row_lat(i // (ROW_TILE // MOVE_TILE)))
    return out.reshape(batch, seq, d)
```

```python
import functools
import math

import numpy as np
import jax
import jax.numpy as jnp
from jax import lax
from jax.experimental import pallas as pl
from jax.experimental.pallas import tpu as pltpu

F32 = jnp.float32
BF16 = jnp.bfloat16
U32 = jnp.uint32
I32 = jnp.int32
HIGHEST = lax.Precision.HIGHEST

D_MODEL = 1024
HEAD_DIM = 64
LANES = 128
SUBLANES = 8
GRID_W = 64
ROPE_THETA = 10000.0
EPS = 1e-6
NEG_INF = -1e30
WINDOW = 128
N_EXPERTS = 64
TOP_K = 8
N_GROUPS = 8
TOPK_GROUPS = 4
GROUP_SIZE = N_EXPERTS // N_GROUPS
ROUTED_SCALE = 2.5
LOG2E = 1.4426950408889634
Q_SCALE = HEAD_DIM ** -0.5 * LOG2E

ROW_TILE = 256
Q_TILE = 256
KV_TILE = 256
LAT_KV_TILE = 256
MOVE_TILE = 256
EXPERT_BLOCK = 1024
EXPERT_SLAB = 1024
SEG_ALIGN = 8
RUN_PIECES = (256, 128, 64, 32, 16, 8)
WAIT_PIECES = (2048, 1024, 512) + RUN_PIECES
STAGE_ROWS = 2560
STAGE_CHUNK = 512
HALF = D_MODEL // 2
VMEM_LIMIT = 48 * 1024 * 1024


def _cparams(sem):
    return pltpu.CompilerParams(dimension_semantics=sem, vmem_limit_bytes=VMEM_LIMIT)


def _rms(x, g):
    ms = jnp.mean(x * x, axis=-1, keepdims=True)
    return x * lax.rsqrt(ms + EPS) * g


def _pack_halves(h):
    lo = lax.bitcast_convert_type(h[:, :HALF].astype(BF16).astype(F32), U32) >> 16
    hi = lax.bitcast_convert_type(h[:, HALF:].astype(BF16).astype(F32), U32) & jnp.uint32(0xFFFF0000)
    return hi | lo


def _unpack_halves(u):
    lo = lax.bitcast_convert_type(u << 16, F32)
    hi = lax.bitcast_convert_type(u & jnp.uint32(0xFFFF0000), F32)
    return lo, hi


def _mod_kernel(c_ref, w_ref, b_ref, o_ref):
    c = c_ref[...]
    sc = c * jax.nn.sigmoid(c)
    o_ref[0] = jnp.dot(sc, w_ref[0], precision=HIGHEST, preferred_element_type=F32) + b_ref[0]


def _modulation(cond, ada_w, ada_b):
    depth, d, n = ada_w.shape
    nt = 1536
    return pl.pallas_call(
        _mod_kernel,
        out_shape=jax.ShapeDtypeStruct((depth, 8, n), F32),
        grid=(depth, n // nt),
        in_specs=[pl.BlockSpec((8, d), lambda l, j: (0, 0)),
                  pl.BlockSpec((1, d, nt), lambda l, j: (l, 0, j)),
                  pl.BlockSpec((1, 1, nt), lambda l, j: (l, 0, j))],
        out_specs=pl.BlockSpec((1, 8, nt), lambda l, j: (l, 0, j)),
        compiler_params=_cparams(("arbitrary", "arbitrary")),
        name="ada_mod",
    )(cond, ada_w, ada_b.reshape(depth, 1, n))


def _proj_kernel(*refs, blocks, tiles):
    if tiles:
        xc_ref, x_ref, sh_ref, sc_ref, g_ref, w_ref, cos_ref, sin_ref, hg_ref, gm_ref, o_ref = refs
        x = jnp.where(pl.program_id(0) % tiles == 0, xc_ref[...], x_ref[...])
    else:
        x_ref, sh_ref, sc_ref, g_ref, w_ref, cos_ref, sin_ref, hg_ref, gm_ref, o_ref = refs
        x = x_ref[...]
    h = _rms(x, g_ref[...]) * (1.0 + sc_ref[...]) + sh_ref[...]
    y = jnp.dot(h.astype(BF16), w_ref[...], preferred_element_type=F32)
    lane = lax.broadcasted_iota(I32, (x.shape[0], LANES), 1)
    first_half = (lane % HEAD_DIM) < (HEAD_DIM // 2)
    for jb, (norm, rope, scale) in enumerate(blocks):
        cols = slice(jb * LANES, (jb + 1) * LANES)
        yb = y[:, cols]
        if norm:
            ms = jnp.dot(yb * yb, gm_ref[...], precision=HIGHEST, preferred_element_type=F32)
            yb = yb * lax.rsqrt(ms + EPS) * hg_ref[:, cols]
        if rope:
            swapped = jnp.where(first_half, pltpu.roll(yb, LANES - HEAD_DIM // 2, 1),
                                pltpu.roll(yb, HEAD_DIM // 2, 1))
            yb = yb * cos_ref[...] + swapped * sin_ref[...]
        if scale != 1.0:
            yb = yb * scale
        o_ref[:, cols] = yb.astype(BF16)


def _pair_specs(pair, tiles):
    ctx_rows, lat_rows = pair
    return [pl.BlockSpec((ROW_TILE, ctx_rows.shape[1]), lambda i: (i // tiles, 0)),
            pl.BlockSpec((ROW_TILE, lat_rows.shape[1]),
                         lambda i: ((i // tiles) * (tiles - 1) + jnp.maximum(i % tiles - 1, 0), 0))]


def _project(x, modr, mod_base, row_of_tile, g, w, cos_t, sin_t, head_gain, blocks, tiles_per_batch):
    paired = isinstance(x, tuple)
    xs = list(x) if paired else [x]
    t, d = sum(a.shape[0] for a in xs), xs[0].shape[1]
    x_specs = _pair_specs(x, tiles_per_batch) if paired else [pl.BlockSpec((ROW_TILE, d), lambda i: (i, 0))]
    n = w.shape[1]
    group_mean = jnp.asarray(np.kron(np.eye(LANES // HEAD_DIM), np.full((HEAD_DIM, HEAD_DIM), 1.0 / HEAD_DIM)), F32)
    return pl.pallas_call(
        functools.partial(_proj_kernel, blocks=blocks, tiles=tiles_per_batch if paired else None),
        out_shape=jax.ShapeDtypeStruct((t, n), BF16),
        grid=(t // ROW_TILE,),
        in_specs=x_specs + [
                  pl.BlockSpec((None, 1, d), lambda i: (mod_base + row_of_tile(i) * 6 + 0, 0, 0)),
                  pl.BlockSpec((None, 1, d), lambda i: (mod_base + row_of_tile(i) * 6 + 1, 0, 0)),
                  pl.BlockSpec((1, d), lambda i: (0, 0)),
                  pl.BlockSpec((d, n), lambda i: (0, 0)),
                  pl.BlockSpec((ROW_TILE, LANES), lambda i: (i % tiles_per_batch, 0)),
                  pl.BlockSpec((ROW_TILE, LANES), lambda i: (i % tiles_per_batch, 0)),
                  pl.BlockSpec((1, n), lambda i: (0, 0)),
                  pl.BlockSpec((LANES, LANES), lambda i: (0, 0))],
        out_specs=pl.BlockSpec((ROW_TILE, n), lambda i: (i, 0)),
        compiler_params=_cparams(("arbitrary",)),
        name="prenorm_proj",
    )(*xs, modr, modr, g, w, cos_t, sin_t, head_gain, group_mean)


def _attn_kernel(*refs, mode, queries, seq, ctx_len, lambda_init, q_parts):
    q_refs, refs = refs[:q_parts], refs[q_parts:]
    if mode == "window":
        k_ref, v_ref, sink_ref, o_ref = refs
    elif mode == "diff":
        k_ref, v_ref, lam_ref, sg_ref, o_ref = refs
    else:
        k_ref, v_ref, o_ref = refs
    tq = q_parts * q_refs[0].shape[0]
    j = pl.program_id(1)
    qi = pl.program_id(2)

    q = q_refs[0][...] if q_parts == 1 else jnp.concatenate([r[...] for r in q_refs], axis=0)
    lane = lax.broadcasted_iota(I32, (tq, LANES), 1)
    low = lane < HEAD_DIM
    zero = jnp.zeros_like(q)
    q2 = jnp.concatenate([jnp.where(low, q, zero), jnp.where(low, zero, q)], axis=0)

    if mode == "window":
        s0 = sink_ref[pl.ds(2 * j, 1), :]
        s1 = sink_ref[pl.ds(2 * j + 1, 1), :]
        row = lax.broadcasted_iota(I32, (2 * tq, LANES), 0)
        sink = jnp.where(row < tq, s0, s1) * LOG2E
        m0 = sink
    else:
        m0 = jnp.full((2 * tq, LANES), NEG_INF, F32)
    state = (m0, jnp.zeros((2 * tq, LANES), F32), jnp.zeros((2 * tq, LANES), F32))

    def chunk(state, start, valid, size=KV_TILE):
        m_prev, l_prev, acc_prev = state
        k = k_ref[pl.ds(start, size), :]
        v = v_ref[pl.ds(start, size), :]
        s = lax.dot_general(q2, k, (((1,), (1,)), ((), ())), preferred_element_type=F32)
        if valid is not None:
            s = jnp.where(valid, s, NEG_INF)
        m_new = jnp.maximum(m_prev, jnp.max(s, axis=1, keepdims=True))
        alpha = jnp.exp2(m_prev - m_new)
        p = jnp.exp2(s - jnp.concatenate([m_new] * (size // LANES), axis=1))
        part = p[:, :LANES]
        for c in range(1, size // LANES):
            part = part + p[:, c * LANES:(c + 1) * LANES]
        return (m_new, alpha * l_prev + part,
                alpha * acc_prev + jnp.dot(p.astype(BF16), v, preferred_element_type=F32))

    state = chunk(state, 0, None)
    if queries == "latent" and mode == "window":
        n_win = (tq + 2 * WINDOW) // KV_TILE
        q0 = qi * tq
        kstart = jnp.clip(q0 - WINDOW, 0, seq - n_win * KV_TILE)
        r = lax.broadcasted_iota(I32, (2 * tq, KV_TILE), 0)
        qpos = q0 + jnp.where(r >= tq, r - tq, r)
        col = lax.broadcasted_iota(I32, (2 * tq, KV_TILE), 1)
        for w in range(n_win):
            kpos = kstart + w * KV_TILE + col
            state = chunk(state, pl.multiple_of(ctx_len + kstart + w * KV_TILE, WINDOW),
                          jnp.abs(qpos - kpos) <= WINDOW)
    elif queries == "latent":
        for c in range(seq // LAT_KV_TILE):
            state = chunk(state, ctx_len + c * LAT_KV_TILE, None, LAT_KV_TILE)

    m_fin, l_part, acc = state
    l = jnp.sum(l_part, axis=1, keepdims=True)
    if mode == "window":
        l = l + jnp.exp2(sink - m_fin)[:, :1]
    o2 = acc / l
    if mode == "diff":
        lp = lam_ref[...]
        lam = (jnp.exp(jnp.sum(lp[0:1] * lp[1:2], axis=1, keepdims=True))
               - jnp.exp(jnp.sum(lp[2:3] * lp[3:4], axis=1, keepdims=True)) + lambda_init)
        o = o2[:tq] - lam * o2[tq:]
        o = _rms(o, sg_ref[...]) * (1.0 - lambda_init)
    else:
        o = jnp.where(low, o2[:tq], o2[tq:])
    o_ref[...] = o.astype(BF16)


def _attention(p, extra, *, mode, queries, batch, seq, ctx_len, n_qblocks, q_col0, k_col, v_col,
               lambda_init=0.0, q_parts=1):
    n_p = ctx_len + seq
    tiles = n_p // Q_TILE
    tq = q_parts * Q_TILE
    n_rows = seq if queries == "latent" else ctx_len
    qt = n_rows // tq
    q_off = ctx_len // Q_TILE if queries == "latent" else 0
    in_specs = [pl.BlockSpec((Q_TILE, LANES),
                             lambda b, j, qi, part=part: (b * tiles + q_off + qi * q_parts + part, q_col0 + j))
                for part in range(q_parts)]
    in_specs += [pl.BlockSpec((n_p, LANES), lambda b, j, qi: (b, k_col(j))),
                 pl.BlockSpec((n_p, LANES), lambda b, j, qi: (b, v_col(j)))]
    args = [p] * (q_parts + 2)
    for e in extra:
        in_specs.append(pl.BlockSpec(e.shape, lambda b, j, qi: (0, 0)))
        args.append(e)
    kern = functools.partial(_attn_kernel, mode=mode, queries=queries, seq=seq, ctx_len=ctx_len,
                             lambda_init=lambda_init, q_parts=q_parts)
    return pl.pallas_call(
        kern,
        out_shape=jax.ShapeDtypeStruct((batch * n_rows, n_qblocks * LANES), BF16),
        grid=(batch, n_qblocks, qt),
        in_specs=in_specs,
        out_specs=pl.BlockSpec((tq, LANES), lambda b, j, qi: (b * qt + qi, j)),
        compiler_params=_cparams(("arbitrary", "arbitrary", "arbitrary")),
        name="attn_" + mode + "_" + queries,
    )(*args)


def _out_kernel(*refs, n_o, tiles):
    n_in = 2 * n_o if tiles else n_o
    o_refs = refs[:n_in]
    w_refs = refs[n_in:n_in + n_o]
    rest = refs[n_in + n_o:]
    x_refs, rest = (rest[:2], rest[2:]) if tiles else (rest[:1], rest[1:])
    g1_ref, gate_ref, g2_ref, sh_ref, sc_ref, rw_ref, xn_ref, hp_ref, lg_ref = rest
    is_ctx = (pl.program_id(0) % tiles == 0) if tiles else None

    def residual_rows(rows):
        if not tiles:
            return x_refs[0][rows, :]
        return jnp.where(is_ctx, x_refs[0][rows, :], x_refs[1][rows, :])

    def mixer_rows(a, rows):
        if not tiles:
            return o_refs[a][rows, :]
        return jnp.where(is_ctx, o_refs[2 * a][rows, :], o_refs[2 * a + 1][rows, :])

    slab = LANES
    for r0 in range(0, xn_ref.shape[0], slab):
        rows = slice(r0, r0 + slab)
        y = jnp.dot(mixer_rows(0, rows), w_refs[0][...], preferred_element_type=F32)
        for a in range(1, n_o):
            y = y + jnp.dot(mixer_rows(a, rows), w_refs[a][...], preferred_element_type=F32)
        xn = residual_rows(rows) + gate_ref[...] * _rms(y, g1_ref[...])
        xn_ref[rows, :] = xn
        h = _rms(xn, g2_ref[...]) * (1.0 + sc_ref[...]) + sh_ref[...]
        hp_ref[rows, :] = _pack_halves(h)
        lg_ref[:, rows] = lax.dot_general(rw_ref[...], h, (((1,), (1,)), ((), ())), precision=HIGHEST,
                                          preferred_element_type=F32)


def _out_project(os_, ws, x, x_tile, modr, mod_base, row_of_tile, g1, g2, rw_t, n_tiles, tiles=None):
    d = x[0].shape[1] if tiles else x.shape[1]
    n_o = len(os_)
    tm = n_tiles * ROW_TILE
    mspec = lambda which: pl.BlockSpec((None, 1, d), lambda i: (mod_base + row_of_tile(i) * 6 + which, 0, 0))
    if tiles:
        o_specs, o_args = [], []
        for pair in os_:
            o_specs += _pair_specs(pair, tiles)
            o_args += list(pair)
        x_specs, x_args = _pair_specs(x, tiles), list(x)
    else:
        o_specs = [pl.BlockSpec((ROW_TILE, o.shape[1]), lambda i: (i, 0)) for o in os_]
        o_args = list(os_)
        x_specs, x_args = [pl.BlockSpec((ROW_TILE, d), lambda i: (x_tile(i), 0))], [x]
    in_specs = (o_specs
                + [pl.BlockSpec(w.shape, lambda i: (0, 0)) for w in ws]
                + x_specs
                + [pl.BlockSpec((1, d), lambda i: (0, 0)), mspec(2),
                   pl.BlockSpec((1, d), lambda i: (0, 0)), mspec(3), mspec(4),
                   pl.BlockSpec(rw_t.shape, lambda i: (0, 0))])
    return pl.pallas_call(
        functools.partial(_out_kernel, n_o=n_o, tiles=tiles),
        out_shape=(jax.ShapeDtypeStruct((tm, d), F32), jax.ShapeDtypeStruct((tm, HALF), U32),
                   jax.ShapeDtypeStruct((N_EXPERTS, tm), F32)),
        grid=(n_tiles,),
        in_specs=in_specs,
        out_specs=(pl.BlockSpec((ROW_TILE, d), lambda i: (i, 0)),
                   pl.BlockSpec((ROW_TILE, HALF), lambda i: (i, 0)),
                   pl.BlockSpec((N_EXPERTS, ROW_TILE), lambda i: (0, i))),
        compiler_params=_cparams(("arbitrary",)),
        name="out_proj",
    )(*o_args, *ws, *x_args, g1, modr, g2, modr, modr, rw_t)


def _route_kernel(lg_ref, bias_ref, tri_ref, ltri_ref, ls_ref, w_ref, segn_ref, segc_ref, cnt_ref, carry_ref):
    i = pl.program_id(0)

    @pl.when(i == 0)
    def _():
        carry_ref[...] = jnp.zeros(carry_ref.shape, F32)

    tr = lg_ref.shape[1]
    score = jax.nn.sigmoid(lg_ref[...])
    sel = score + bias_ref[...][:, :1]
    sel_j = [sel[j * GROUP_SIZE:(j + 1) * GROUP_SIZE] for j in range(GROUP_SIZE)]
    sc_j = [score[j * GROUP_SIZE:(j + 1) * GROUP_SIZE] for j in range(GROUP_SIZE)]
    gi = lax.broadcasted_iota(I32, (N_GROUPS, tr), 0)

    m1 = sel_j[0]
    m2 = jnp.full_like(m1, -jnp.inf)
    for j in range(1, GROUP_SIZE):
        m2 = jnp.maximum(m2, jnp.minimum(m1, sel_j[j]))
        m1 = jnp.maximum(m1, sel_j[j])
    gs = m1 + m2

    grank = jnp.zeros((N_GROUPS, tr), I32)
    for gp in range(N_GROUPS):
        rowv = gs[gp:gp + 1, :]
        grank = grank + jnp.where(gi > gp, jnp.where(rowv >= gs, 1, 0), jnp.where(rowv > gs, 1, 0))
    gmask = grank < TOPK_GROUPS
    val_j = [jnp.where(gmask, sel_j[j], NEG_INF) for j in range(GROUP_SIZE)]

    rank_j = [jnp.zeros((N_GROUPS, tr), I32) for _ in range(GROUP_SIZE)]
    for gp in range(N_GROUPS):
        after = gi > gp
        not_before = gi >= gp
        for jp in range(GROUP_SIZE):
            rowv = val_j[jp][gp:gp + 1, :]
            for j in range(GROUP_SIZE):
                tie = after if jp >= j else not_before
                rank_j[j] = rank_j[j] + jnp.where(tie, jnp.where(rowv >= val_j[j], 1, 0),
                                                  jnp.where(rowv > val_j[j], 1, 0))

    chosen = jnp.concatenate([jnp.where(rank_j[j] < TOP_K, 1.0, 0.0) for j in range(GROUP_SIZE)], axis=0)
    n_run = jnp.floor((jnp.sum(chosen, axis=1, keepdims=True) + (SEG_ALIGN - 1.0)) * (1.0 / SEG_ALIGN)) * SEG_ALIGN
    n_run = jnp.broadcast_to(n_run, (N_EXPERTS, LANES))
    run_start = jnp.dot(ltri_ref[...], n_run.astype(BF16), preferred_element_type=F32)
    local = jnp.dot(chosen.astype(BF16), tri_ref[...], preferred_element_type=F32) + run_start[:, :1]
    segn_ref[0] = n_run.astype(I32)
    segc_ref[0] = carry_ref[...].astype(I32)
    carry_ref[...] = carry_ref[...] + n_run
    cnt_ref[...] = carry_ref[...].astype(I32)

    w_rows, ls_rows = [], []
    for k in range(TOP_K):
        w_acc = jnp.zeros((N_GROUPS, tr), F32)
        p_acc = jnp.zeros((N_GROUPS, tr), F32)
        for j in range(GROUP_SIZE):
            hit = rank_j[j] == k
            w_acc = w_acc + jnp.where(hit, sc_j[j], 0.0)
            p_acc = p_acc + jnp.where(hit, local[j * GROUP_SIZE:(j + 1) * GROUP_SIZE], 0.0)
        w_rows.append(jnp.sum(w_acc, axis=0, keepdims=True))
        ls_rows.append(jnp.sum(p_acc, axis=0, keepdims=True))
    w_all = jnp.concatenate(w_rows, axis=0)
    w_ref[...] = w_all / jnp.sum(w_all, axis=0, keepdims=True) * ROUTED_SCALE
    ls_ref[...] = jnp.concatenate(ls_rows, axis=0).astype(I32)


def _route(lg_t, bias_rep):
    tm = lg_t.shape[1]
    n_tiles = tm // MOVE_TILE
    tri = jnp.asarray(np.triu(np.ones((MOVE_TILE, MOVE_TILE), np.float32), 1), BF16)
    ltri = jnp.asarray(np.tril(np.ones((N_EXPERTS, N_EXPERTS), np.float32), -1), BF16)
    tok = lambda i: (0, i)
    per_tile = pl.BlockSpec((1, N_EXPERTS, LANES), lambda i: (i, 0, 0))
    return pl.pallas_call(
        _route_kernel,
        out_shape=(jax.ShapeDtypeStruct((TOP_K, tm), I32), jax.ShapeDtypeStruct((TOP_K, tm), F32),
                   jax.ShapeDtypeStruct((n_tiles, N_EXPERTS, LANES), I32),
                   jax.ShapeDtypeStruct((n_tiles, N_EXPERTS, LANES), I32),
                   jax.ShapeDtypeStruct((N_EXPERTS, LANES), I32)),
        grid=(n_tiles,),
        in_specs=[pl.BlockSpec((N_EXPERTS, MOVE_TILE), tok),
                  pl.BlockSpec((N_EXPERTS, LANES), lambda i: (0, 0)),
                  pl.BlockSpec((MOVE_TILE, MOVE_TILE), lambda i: (0, 0)),
                  pl.BlockSpec((N_EXPERTS, N_EXPERTS), lambda i: (0, 0))],
        out_specs=(pl.BlockSpec((TOP_K, MOVE_TILE), tok), pl.BlockSpec((TOP_K, MOVE_TILE), tok),
                   per_tile, per_tile, pl.BlockSpec((N_EXPERTS, LANES), lambda i: (0, 0))),
        scratch_shapes=[pltpu.VMEM((N_EXPERTS, LANES), F32)],
        compiler_params=_cparams(("arbitrary",)),
        name="route",
    )(lg_t, bias_rep, tri, ltri)


def _start_pieces(n, src, dst, make, pieces):
    for size in pieces:
        above = n & (-2 * size)

        @pl.when((n & size) != 0)
        def _():
            make(pl.multiple_of(src + above, SEG_ALIGN), pl.multiple_of(dst + above, SEG_ALIGN), size).start()


def _start_run_copies(i, segn_ref, segd_ref, make, straight_line=False):
    split = RUN_PIECES.index(64)

    def per_run(r, src):
        n = segn_ref[i * N_EXPERTS + r]
        dst = segd_ref[i * N_EXPERTS + r]

        @pl.when(n >= RUN_PIECES[split - 1])
        def _():
            _start_pieces(n, src, dst, make, RUN_PIECES[:split])

        _start_pieces(n, src, dst, make, RUN_PIECES[split:])
        return src + n

    if not straight_line:
        return lax.fori_loop(0, N_EXPERTS, per_run, 0, unroll=2)
    src = 0
    for r in range(N_EXPERTS):
        src = per_run(r, src)
    return src


def _wait_rows(total, make):
    for size in WAIT_PIECES:
        @pl.when((total & size) != 0)
        def _():
            make(0, 0, size).wait()


def _fill_copies(zstart_ref, zlen_ref, make):
    pieces = tuple(p for p in WAIT_PIECES if p < EXPERT_BLOCK)

    def per_expert(r, carry):
        n = zlen_ref[r]
        _start_pieces(n, 0, zstart_ref[r], make, pieces)
        _wait_rows(n, make)
        return carry

    lax.fori_loop(0, N_EXPERTS, per_expert, 0)


def _dispatch_kernel(segn_ref, segd_ref, zstart_ref, zlen_ref, ls_ref, hp_ref, xs_hbm, stage_ref, zero_ref,
                     rows_ref, sem, zsem):
    i = pl.program_id(0)
    buf = i % 2
    lo, hi = _unpack_halves(hp_ref[...])
    lo = lo.astype(BF16)
    hi = hi.astype(BF16)
    ls = ls_ref[...]
    pack_rows = 16
    ls16 = [jnp.tile(jnp.broadcast_to(ls[k:k + 1, :], (pack_rows, MOVE_TILE)).astype(jnp.int16),
                     (STAGE_CHUNK // pack_rows, 1)) for k in range(TOP_K)]
    slot0 = lax.broadcasted_iota(I32, (STAGE_CHUNK, MOVE_TILE), 0).astype(jnp.int16)
    one = jnp.ones((STAGE_CHUNK, MOVE_TILE), BF16)
    for c in range(STAGE_ROWS // STAGE_CHUNK):
        slot = slot0 + jnp.int16(c * STAGE_CHUNK)
        onehot = jnp.zeros((STAGE_CHUNK, MOVE_TILE), BF16)
        for k in range(TOP_K):
            onehot = jnp.where(ls16[k] == slot, one, onehot)
        a = lax.bitcast_convert_type(jnp.dot(onehot, lo, preferred_element_type=F32), U32) >> 16
        b = lax.bitcast_convert_type(jnp.dot(onehot, hi, preferred_element_type=F32), U32)
        stage_ref[buf, c * STAGE_CHUNK:(c + 1) * STAGE_CHUNK, :] = b | a

    def to_slots(which):
        def make(src, dst, size):
            return pltpu.make_async_copy(stage_ref.at[which, pl.ds(src, size), :],
                                         xs_hbm.at[pl.ds(dst, size), :], sem.at[which])
        return make

    rows_ref[buf] = _start_run_copies(i, segn_ref, segd_ref, to_slots(buf))

    @pl.when(i > 0)
    def _():
        _wait_rows(rows_ref[1 - buf], to_slots(1 - buf))

    @pl.when(i == pl.num_programs(0) - 1)
    def _():
        _wait_rows(rows_ref[buf], to_slots(buf))

    @pl.when(i == 0)
    def _():
        zero_ref[...] = jnp.zeros(zero_ref.shape, U32)

        def zeros_to_slots(src, dst, size):
            return pltpu.make_async_copy(zero_ref.at[pl.ds(0, size), :], xs_hbm.at[pl.ds(dst, size), :], zsem)

        _fill_copies(zstart_ref, zlen_ref, zeros_to_slots)


def _dispatch(segn, segd, zstart, zlen, ls, hp, n_slots):
    tm = hp.shape[0]
    tile = lambda i, *_: (i, 0)
    return pl.pallas_call(
        _dispatch_kernel,
        out_shape=jax.ShapeDtypeStruct((n_slots, HALF), U32),
        grid_spec=pltpu.PrefetchScalarGridSpec(
            num_scalar_prefetch=4,
            grid=(tm // MOVE_TILE,),
            in_specs=[pl.BlockSpec((TOP_K, MOVE_TILE), lambda i, *_: (0, i)),
                      pl.BlockSpec((MOVE_TILE, HALF), tile)],
            out_specs=pl.BlockSpec(memory_space=pl.ANY),
            scratch_shapes=[pltpu.VMEM((2, STAGE_ROWS, HALF), U32), pltpu.VMEM((EXPERT_BLOCK // 2, HALF), U32),
                            pltpu.SMEM((2,), I32), pltpu.SemaphoreType.DMA((2,)), pltpu.SemaphoreType.DMA]),
        compiler_params=_cparams(("arbitrary",)),
        name="dispatch",
    )(segn, segd, zstart, zlen, ls, hp)


def _expert_kernel(be_ref, nu_ref, nxt_ref, xs_ref, w1_hbm, w3_hbm, w2_hbm, ys_ref, w1f_ref, w3f_ref, w2f_ref,
                   w1b_ref, w3b_ref, w2b_ref, run_ref, sem, *, layer):
    i = pl.program_id(0)

    def fetch(e, slot):
        return [pltpu.make_async_copy(src.at[layer, e], dst.at[slot], sem.at[slot])
                for src, dst in ((w1_hbm, w1f_ref), (w3_hbm, w3f_ref), (w2_hbm, w2f_ref))]

    @pl.when(i == 0)
    def _():
        run_ref[0] = 0
        for cp in fetch(be_ref[0], 0):
            cp.start()

    first = jnp.logical_or(i == 0, be_ref[i] != be_ref[jnp.maximum(i - 1, 0)])

    @pl.when(jnp.logical_and(first, i < nu_ref[0]))
    def _():
        slot = run_ref[0] % 2
        for cp in fetch(be_ref[i], slot):
            cp.wait()
        w1b_ref[...] = w1f_ref[slot].astype(BF16)
        w3b_ref[...] = w3f_ref[slot].astype(BF16)
        w2b_ref[...] = w2f_ref[slot].astype(BF16)
        nxt = nxt_ref[i]

        @pl.when(nxt >= 0)
        def _():
            for cp in fetch(nxt, 1 - slot):
                cp.start()

        run_ref[0] = run_ref[0] + 1

    @pl.when(i < nu_ref[0])
    def _():
        for rows in (slice(r0, r0 + EXPERT_SLAB) for r0 in range(0, EXPERT_BLOCK, EXPERT_SLAB)):
            lo, hi = _unpack_halves(xs_ref[rows, :])
            lo = lo.astype(BF16)
            hi = hi.astype(BF16)
            h1 = (jnp.dot(lo, w1b_ref[:HALF, :], preferred_element_type=F32)
                  + jnp.dot(hi, w1b_ref[HALF:, :], preferred_element_type=F32))
            h3 = (jnp.dot(lo, w3b_ref[:HALF, :], preferred_element_type=F32)
                  + jnp.dot(hi, w3b_ref[HALF:, :], preferred_element_type=F32))
            g = (h1 * jax.nn.sigmoid(h1) * h3).astype(BF16)
            y = jnp.dot(g, w2b_ref[...], preferred_element_type=F32)
            ys_ref[rows, :] = _pack_halves(y)


def _experts(block_e, n_used, next_e, xs, w1, w3, w2, layer):
    n_slots = xs.shape[0]
    n_blocks = n_slots // EXPERT_BLOCK
    d, eh = w1.shape[2], w1.shape[3]
    rows = lambda i, be, nu, nx: (jnp.minimum(i, nu[0] - 1), 0)
    anyspace = pl.BlockSpec(memory_space=pl.ANY)
    return pl.pallas_call(
        functools.partial(_expert_kernel, layer=layer),
        out_shape=jax.ShapeDtypeStruct((n_slots, HALF), U32),
        grid_spec=pltpu.PrefetchScalarGridSpec(
            num_scalar_prefetch=3,
            grid=(n_blocks,),
            in_specs=[pl.BlockSpec((EXPERT_BLOCK, HALF), rows), anyspace, anyspace, anyspace],
            out_specs=pl.BlockSpec((EXPERT_BLOCK, HALF), rows),
            scratch_shapes=[pltpu.VMEM((2, d, eh), F32), pltpu.VMEM((2, d, eh), F32), pltpu.VMEM((2, eh, d), F32),
                            pltpu.VMEM((d, eh), BF16), pltpu.VMEM((d, eh), BF16), pltpu.VMEM((eh, d), BF16),
                            pltpu.SMEM((1,), I32), pltpu.SemaphoreType.DMA((2,))]),
        compiler_params=_cparams(("arbitrary",)),
        name="experts",
    )(block_e, n_used, next_e, xs, w1, w3, w2)


def _combine_kernel(segn_ref, segd_ref, ys_hbm, lst_ref, wt_ref, hp_ref, x_ref, s1_ref, s3_ref, s2_ref, g_ref,
                    gate_ref, o_ref, stage_ref, rows_ref, sem):
    i = pl.program_id(0)
    buf = i % 2

    def from_slots(which):
        def make(src, dst, size):
            return pltpu.make_async_copy(ys_hbm.at[pl.ds(dst, size), :],
                                         stage_ref.at[which, pl.ds(src, size), :], sem.at[which])
        return make

    @pl.when(i == 0)
    def _():
        stage_ref[...] = jnp.zeros(stage_ref.shape, U32)
        rows_ref[0] = _start_run_copies(0, segn_ref, segd_ref, from_slots(0))

    last = pl.num_programs(0) - 1
    rows_ref[1 - buf] = _start_run_copies(jnp.minimum(i + 1, last), segn_ref, segd_ref, from_slots(1 - buf),
                                          straight_line=True)

    lo, hi = _unpack_halves(hp_ref[...])
    lo = lo.astype(BF16)
    hi = hi.astype(BF16)
    h1 = (jnp.dot(lo, s1_ref[:HALF, :], preferred_element_type=F32)
          + jnp.dot(hi, s1_ref[HALF:, :], preferred_element_type=F32))
    h3 = (jnp.dot(lo, s3_ref[:HALF, :], preferred_element_type=F32)
          + jnp.dot(hi, s3_ref[HALF:, :], preferred_element_type=F32))
    y = jnp.dot((h1 * jax.nn.sigmoid(h1) * h3).astype(BF16), s2_ref[...], preferred_element_type=F32)

    _wait_rows(rows_ref[buf], from_slots(buf))

    @pl.when(i == last)
    def _():
        _wait_rows(rows_ref[1 - buf], from_slots(1 - buf))

    lst = lst_ref[...]
    wt = wt_ref[...]
    lst16 = [jnp.broadcast_to(lst[:, k:k + 1], (MOVE_TILE, STAGE_CHUNK)).astype(jnp.int16) for k in range(TOP_K)]
    wt16 = [jnp.broadcast_to(wt[:, k:k + 1], (MOVE_TILE, STAGE_CHUNK)).astype(BF16) for k in range(TOP_K)]
    slot0 = lax.broadcasted_iota(I32, (MOVE_TILE, STAGE_CHUNK), 1).astype(jnp.int16)
    r_lo = jnp.zeros((MOVE_TILE, HALF), F32)
    r_hi = jnp.zeros((MOVE_TILE, HALF), F32)
    for c in range(STAGE_ROWS // STAGE_CHUNK):
        slot = slot0 + jnp.int16(c * STAGE_CHUNK)
        wsel = jnp.zeros((MOVE_TILE, STAGE_CHUNK), BF16)
        for k in range(TOP_K):
            wsel = jnp.where(lst16[k] == slot, wt16[k], wsel)
        a, b = _unpack_halves(stage_ref[buf, c * STAGE_CHUNK:(c + 1) * STAGE_CHUNK, :])
        r_lo = r_lo + jnp.dot(wsel, a.astype(BF16), preferred_element_type=F32)
        r_hi = r_hi + jnp.dot(wsel, b.astype(BF16), preferred_element_type=F32)
    y = y + jnp.concatenate([r_lo, r_hi], axis=1)
    o_ref[...] = x_ref[...] + gate_ref[...] * _rms(y, g_ref[...])


def _combine(segn, segd, ys, lst, wt, hp, x, s1, s3, s2, g, modr, mod_base, row_of_tile):
    tm, d = x.shape
    full = lambda a: pl.BlockSpec(a.shape, lambda i, *_: (0, 0))
    tile = lambda i, *_: (i, 0)
    return pl.pallas_call(
        _combine_kernel,
        out_shape=jax.ShapeDtypeStruct((tm, d), F32),
        grid_spec=pltpu.PrefetchScalarGridSpec(
            num_scalar_prefetch=2,
            grid=(tm // MOVE_TILE,),
            in_specs=[pl.BlockSpec(memory_space=pl.ANY),
                      pl.BlockSpec((MOVE_TILE, TOP_K), tile), pl.BlockSpec((MOVE_TILE, TOP_K), tile),
                      pl.BlockSpec((MOVE_TILE, HALF), tile), pl.BlockSpec((MOVE_TILE, d), tile),
                      full(s1), full(s3), full(s2), full(g),
                      pl.BlockSpec((None, 1, d), lambda i, *_: (mod_base + row_of_tile(i) * 6 + 5, 0, 0))],
            out_specs=pl.BlockSpec((MOVE_TILE, d), tile),
            scratch_shapes=[pltpu.VMEM((2, STAGE_ROWS, HALF), U32), pltpu.SMEM((2,), I32),
                            pltpu.SemaphoreType.DMA((2,))]),
        compiler_params=_cparams(("arbitrary",)),
        name="combine",
    )(segn, segd, ys, lst, wt, hp, x, s1, s3, s2, g, modr)


def _moe(xn, hp, lg_t, bias, expert_weights, s1, s3, s2, g, modr, mod_base, row_of_tile):
    tm = xn.shape[0]
    n_tiles = tm // MOVE_TILE
    perm = np.array([(r % GROUP_SIZE) * GROUP_SIZE + r // GROUP_SIZE for r in range(N_EXPERTS)])
    bias_rep = jnp.broadcast_to(bias.astype(F32)[perm][:, None], (N_EXPERTS, LANES))
    ls, w, segn, segc, cnt = _route(lg_t, bias_rep)

    total = cnt[:, 0]
    padded = (total + EXPERT_BLOCK - 1) // EXPERT_BLOCK * EXPERT_BLOCK
    pad_end = jnp.cumsum(padded)
    pad_start = pad_end - padded
    n_slots = tm * TOP_K + (SEG_ALIGN - 1) * N_EXPERTS * n_tiles + N_EXPERTS * EXPERT_BLOCK
    n_blocks = -(-n_slots // EXPERT_BLOCK)
    starts = jnp.arange(n_blocks, dtype=I32) * EXPERT_BLOCK
    region = jnp.minimum(jnp.sum((pad_end[None, :] <= starts[:, None]).astype(I32), axis=1), N_EXPERTS - 1)
    block_e = jnp.asarray(perm, I32)[region]
    n_used = (pad_end[-1:] // EXPERT_BLOCK).astype(I32)
    after = pad_end[region] // EXPERT_BLOCK
    next_e = jnp.where(after < n_used[0], block_e[jnp.minimum(after, n_blocks - 1)], -1).astype(I32)
    segn = segn[:, :, 0].reshape(-1)
    segd = (segc[:, :, 0] + pad_start[None, :]).reshape(-1).astype(I32)

    xs = _dispatch(segn, segd, (pad_start + total).astype(I32), (padded - total).astype(I32), ls, hp,
                   n_blocks * EXPERT_BLOCK)
    ys = _experts(block_e, n_used, next_e, xs, *expert_weights)
    return _combine(segn, segd, ys, ls.T, w.T, hp, xn, s1.astype(BF16), s3.astype(BF16), s2.astype(BF16), g, modr,
                    mod_base, row_of_tile)


_DEINTERLEAVE = np.concatenate([np.arange(0, HEAD_DIM, 2), np.arange(1, HEAD_DIM, 2)])


def _rope_tables(seq, ctx_len):
    t = np.arange(seq)
    n_pair = HEAD_DIM // 4
    inv = jnp.asarray(ROPE_THETA, F32) ** (-jnp.arange(n_pair, dtype=F32) / n_pair)
    r = jnp.asarray(t // GRID_W, F32)
    c = jnp.asarray(t % GRID_W, F32)
    ang = jnp.concatenate([r[:, None] * inv, c[:, None] * inv], axis=-1)
    cos, sin = jnp.cos(ang), jnp.sin(ang)
    cos_t = jnp.tile(jnp.concatenate([cos, cos], axis=-1), (1, LANES // HEAD_DIM))
    sin_t = jnp.tile(jnp.concatenate([-sin, sin], axis=-1), (1, LANES // HEAD_DIM))
    cos_t = jnp.concatenate([jnp.ones((ctx_len, LANES), F32), cos_t], axis=0)
    sin_t = jnp.concatenate([jnp.zeros((ctx_len, LANES), F32), sin_t], axis=0)
    return cos_t, sin_t


def _ab_layout():
    cols, blocks, gain_kind = [], [], []
    for base_q, base_k, base_v, normed in ((0, 512, 640, True), (768, 1280, 1408, False)):
        for jb in range(4):
            cols += [base_q + h * HEAD_DIM + _DEINTERLEAVE for h in (2 * jb, 2 * jb + 1)]
            blocks.append((normed, True, Q_SCALE))
            gain_kind.append("q" if normed else None)
        for kvh in range(2):
            cols += [base_k + kvh * HEAD_DIM + _DEINTERLEAVE] * 2
            blocks.append((normed, True, 1.0))
            gain_kind.append("k" if normed else None)
        for kvh in range(2):
            cols += [base_v + kvh * HEAD_DIM + np.arange(HEAD_DIM)] * 2
            blocks.append((False, False, 1.0))
            gain_kind.append(None)
    return np.concatenate(cols), tuple(blocks), gain_kind


def _c_layout():
    cols, blocks = [], []
    for base, rope, scale in ((0, True, Q_SCALE), (1024, True, 1.0)):
        for h in range(8):
            cols += [base + (2 * h + m) * HEAD_DIM + _DEINTERLEAVE for m in range(2)]
            blocks.append((False, rope, scale))
    for h in range(8):
        cols.append(2048 + h * LANES + np.arange(LANES))
        blocks.append((False, False, 1.0))
    return np.concatenate(cols), tuple(blocks)


def kernel(x, c, ctx, c_ctx, ada_w, ada_b, norm_g, ab_w_in, ab_w_out, a_q_norm, a_k_norm, b_sink,
           c_w_in, c_w_out, c_lambda, c_subln_g, router_w, router_bias, exp_w1, exp_w3, exp_w2,
           sh_w1, sh_w3, sh_w2):
    batch, seq, d = x.shape
    ctx_len = ctx.shape[1]
    depth = ada_w.shape[0]
    assert d == D_MODEL and ctx_len == ROW_TILE and depth == 2
    assert seq % (4 * Q_TILE) == 0 and batch <= 4
    n_p = ctx_len + seq
    tiles = n_p // ROW_TILE
    lat_tiles = seq // ROW_TILE
    t_all = batch * n_p

    cond = jnp.zeros((8, d), F32).at[:batch].set(c).at[4].set(c_ctx)
    modr = _modulation(cond, ada_w, ada_b).reshape(depth * 8 * 6, 1, d)
    row_all = lambda i: jnp.where(i % tiles == 0, 4, i // tiles)
    cos_t, sin_t = _rope_tables(seq, ctx_len)
    rperm = np.array([(r % GROUP_SIZE) * GROUP_SIZE + r // GROUP_SIZE for r in range(N_EXPERTS)])
    xt = (ctx.reshape(batch * ctx_len, d), x.reshape(batch * seq, d))

    cols, blocks, gain_kind = _ab_layout()
    w0 = ab_w_in[0][:, cols].astype(BF16)
    gq = jnp.tile(a_q_norm[0][_DEINTERLEAVE], 2)
    gk = jnp.tile(a_k_norm[0][_DEINTERLEAVE], 2)
    ones = jnp.ones((LANES,), F32)
    head_gain = jnp.concatenate([{"q": gq, "k": gk, None: ones}[kind] for kind in gain_kind])[None, :]
    p0 = _project(xt, modr, 0, row_all, norm_g[0, 0][None, :], w0, cos_t, sin_t, head_gain, blocks, tiles)
    common = dict(batch=batch, seq=seq, ctx_len=ctx_len, n_qblocks=4)
    dense = dict(mode="dense", q_col0=0, k_col=lambda j: 4 + j // 2, v_col=lambda j: 6 + j // 2, **common)
    window = dict(mode="window", q_col0=8, k_col=lambda j: 12 + j // 2, v_col=lambda j: 14 + j // 2, **common)
    sink = jnp.broadcast_to(b_sink[0].astype(F32)[:, None], (8, LANES))
    oa = (_attention(p0, [], queries="context", **dense), _attention(p0, [], queries="latent", q_parts=4, **dense))
    ob = (_attention(p0, [sink], queries="context", **window),
          _attention(p0, [sink], queries="latent", **window))
    w_out = ab_w_out[0].astype(BF16)
    x1, hp, lg = _out_project([oa, ob], [w_out[:512], w_out[512:]], xt, lambda i: i, modr, 0, row_all,
                              norm_g[0, 1][None, :], norm_g[0, 2][None, :], router_w[0].T[rperm],
                              t_all // ROW_TILE, tiles=tiles)
    x2 = _moe(x1, hp, lg, router_bias[0], (exp_w1, exp_w3, exp_w2, 0), sh_w1[0], sh_w3[0], sh_w2[0],
              norm_g[0, 3][None, :], modr, 0, lambda i: row_all(i // (ROW_TILE // MOVE_TILE)))

    base1 = 8 * 6
    lambda_init = 0.8 - 0.6 * math.exp(-0.3 * 1)
    cols1, blocks1 = _c_layout()
    w1p = c_w_in[0][:, cols1].astype(BF16)
    p1 = _project(x2, modr, base1, row_all, norm_g[1, 0][None, :], w1p, cos_t, sin_t,
                  jnp.ones((1, w1p.shape[1]), F32), blocks1, tiles)
    lam = jnp.zeros((8, LANES), F32).at[:4, :HEAD_DIM].set(c_lambda[0].astype(F32))
    oc = _attention(p1, [lam, c_subln_g[0][None, :]], mode="diff", queries="latent", batch=batch, seq=seq,
                    ctx_len=ctx_len, n_qblocks=8, q_col0=0, k_col=lambda j: 8 + j, v_col=lambda j: 16 + j,
                    lambda_init=lambda_init, q_parts=4)
    row_lat = lambda i: i // lat_tiles
    x3, hp1, lg1 = _out_project([oc], [c_w_out[0].astype(BF16)], x2,
                                lambda i: (i // lat_tiles) * tiles + 1 + i % lat_tiles, modr, base1, row_lat,
                                norm_g[1, 1][None, :], norm_g[1, 2][None, :], router_w[1].T[rperm],
                                batch * lat_tiles)
    out = _moe(x3, hp1, lg1, router_bias[1], (exp_w1, exp_w3, exp_w2, 1), sh_w1[1], sh_w3[1], sh_w2[1],
               norm_g[1, 3][None, :], modr, base1, lambda i: row_lat(i // (ROW_TILE // MOVE_TILE)))
    return out.reshape(batch, seq, d)
```

```python
import functools
import math

import numpy as np
import jax
import jax.numpy as jnp
from jax import lax
from jax.experimental import pallas as pl
from jax.experimental.pallas import tpu as pltpu

F32 = jnp.float32
BF16 = jnp.bfloat16
U32 = jnp.uint32
I32 = jnp.int32
HIGHEST = lax.Precision.HIGHEST

D_MODEL = 1024
HEAD_DIM = 64
LANES = 128
SUBLANES = 8
GRID_W = 64
ROPE_THETA = 10000.0
EPS = 1e-6
NEG_INF = -1e30
WINDOW = 128
N_EXPERTS = 64
TOP_K = 8
N_GROUPS = 8
TOPK_GROUPS = 4
GROUP_SIZE = N_EXPERTS // N_GROUPS
ROUTED_SCALE = 2.5
LOG2E = 1.4426950408889634
Q_SCALE = HEAD_DIM ** -0.5 * LOG2E

ROW_TILE = 256
Q_TILE = 256
KV_TILE = 256
LAT_KV_TILE = 256
MOVE_TILE = 256
EXPERT_BLOCK = 1024
EXPERT_SLAB = 1024
SEG_ALIGN = 8
RUN_PIECES = (256, 128, 64, 32, 16, 8)
WAIT_PIECES = (2048, 1024, 512) + RUN_PIECES
STAGE_ROWS = 2560
STAGE_CHUNK = 512
HALF = D_MODEL // 2
VMEM_LIMIT = 48 * 1024 * 1024


def _cparams(sem):
    return pltpu.CompilerParams(dimension_semantics=sem, vmem_limit_bytes=VMEM_LIMIT)


def _rms(x, g):
    ms = jnp.mean(x * x, axis=-1, keepdims=True)
    return x * lax.rsqrt(ms + EPS) * g


def _pack_halves(h):
    lo = lax.bitcast_convert_type(h[:, :HALF].astype(BF16).astype(F32), U32) >> 16
    hi = lax.bitcast_convert_type(h[:, HALF:].astype(BF16).astype(F32), U32) & jnp.uint32(0xFFFF0000)
    return hi | lo


def _unpack_halves(u):
    lo = lax.bitcast_convert_type(u << 16, F32)
    hi = lax.bitcast_convert_type(u & jnp.uint32(0xFFFF0000), F32)
    return lo, hi


def _mod_kernel(c_ref, w_ref, b_ref, o_ref):
    c = c_ref[...]
    sc = c * jax.nn.sigmoid(c)
    o_ref[0] = jnp.dot(sc, w_ref[0], precision=HIGHEST, preferred_element_type=F32) + b_ref[0]


def _modulation(cond, ada_w, ada_b):
    depth, d, n = ada_w.shape
    nt = 1536
    return pl.pallas_call(
        _mod_kernel,
        out_shape=jax.ShapeDtypeStruct((depth, 8, n), F32),
        grid=(depth, n // nt),
        in_specs=[pl.BlockSpec((8, d), lambda l, j: (0, 0)),
                  pl.BlockSpec((1, d, nt), lambda l, j: (l, 0, j)),
                  pl.BlockSpec((1, 1, nt), lambda l, j: (l, 0, j))],
        out_specs=pl.BlockSpec((1, 8, nt), lambda l, j: (l, 0, j)),
        compiler_params=_cparams(("arbitrary", "arbitrary")),
        name="ada_mod",
    )(cond, ada_w, ada_b.reshape(depth, 1, n))


def _proj_kernel(*refs, blocks, tiles):
    if tiles:
        xc_ref, x_ref, sh_ref, sc_ref, g_ref, w_ref, cos_ref, sin_ref, hg_ref, gm_ref, o_ref = refs
        x = jnp.where(pl.program_id(0) % tiles == 0, xc_ref[...], x_ref[...])
    else:
        x_ref, sh_ref, sc_ref, g_ref, w_ref, cos_ref, sin_ref, hg_ref, gm_ref, o_ref = refs
        x = x_ref[...]
    h = _rms(x, g_ref[...]) * (1.0 + sc_ref[...]) + sh_ref[...]
    y = jnp.dot(h.astype(BF16), w_ref[...], preferred_element_type=F32)
    lane = lax.broadcasted_iota(I32, (x.shape[0], LANES), 1)
    first_half = (lane % HEAD_DIM) < (HEAD_DIM // 2)
    for jb, (norm, rope, scale) in enumerate(blocks):
        cols = slice(jb * LANES, (jb + 1) * LANES)
        yb = y[:, cols]
        if norm:
            ms = jnp.dot(yb * yb, gm_ref[...], precision=HIGHEST, preferred_element_type=F32)
            yb = yb * lax.rsqrt(ms + EPS) * hg_ref[:, cols]
        if rope:
            swapped = jnp.where(first_half, pltpu.roll(yb, LANES - HEAD_DIM // 2, 1),
                                pltpu.roll(yb, HEAD_DIM // 2, 1))
            yb = yb * cos_ref[...] + swapped * sin_ref[...]
        if scale != 1.0:
            yb = yb * scale
        o_ref[:, cols] = yb.astype(BF16)


def _pair_specs(pair, tiles):
    ctx_rows, lat_rows = pair
    return [pl.BlockSpec((ROW_TILE, ctx_rows.shape[1]), lambda i: (i // tiles, 0)),
            pl.BlockSpec((ROW_TILE, lat_rows.shape[1]),
                         lambda i: ((i // tiles) * (tiles - 1) + jnp.maximum(i % tiles - 1, 0), 0))]


def _project(x, modr, mod_base, row_of_tile, g, w, cos_t, sin_t, head_gain, blocks, tiles_per_batch):
    paired = isinstance(x, tuple)
    xs = list(x) if paired else [x]
    t, d = sum(a.shape[0] for a in xs), xs[0].shape[1]
    x_specs = _pair_specs(x, tiles_per_batch) if paired else [pl.BlockSpec((ROW_TILE, d), lambda i: (i, 0))]
    n = w.shape[1]
    group_mean = jnp.asarray(np.kron(np.eye(LANES // HEAD_DIM), np.full((HEAD_DIM, HEAD_DIM), 1.0 / HEAD_DIM)), F32)
    return pl.pallas_call(
        functools.partial(_proj_kernel, blocks=blocks, tiles=tiles_per_batch if paired else None),
        out_shape=jax.ShapeDtypeStruct((t, n), BF16),
        grid=(t // ROW_TILE,),
        in_specs=x_specs + [
                  pl.BlockSpec((None, 1, d), lambda i: (mod_base + row_of_tile(i) * 6 + 0, 0, 0)),
                  pl.BlockSpec((None, 1, d), lambda i: (mod_base + row_of_tile(i) * 6 + 1, 0, 0)),
                  pl.BlockSpec((1, d), lambda i: (0, 0)),
                  pl.BlockSpec((d, n), lambda i: (0, 0)),
                  pl.BlockSpec((ROW_TILE, LANES), lambda i: (i % tiles_per_batch, 0)),
                  pl.BlockSpec((ROW_TILE, LANES), lambda i: (i % tiles_per_batch, 0)),
                  pl.BlockSpec((1, n), lambda i: (0, 0)),
                  pl.BlockSpec((LANES, LANES), lambda i: (0, 0))],
        out_specs=pl.BlockSpec((ROW_TILE, n), lambda i: (i, 0)),
        compiler_params=_cparams(("arbitrary",)),
        name="prenorm_proj",
    )(*xs, modr, modr, g, w, cos_t, sin_t, head_gain, group_mean)


def _attn_kernel(*refs, mode, queries, seq, ctx_len, lambda_init, q_parts):
    q_refs, refs = refs[:q_parts], refs[q_parts:]
    if mode == "window":
        k_ref, v_ref, sink_ref, o_ref = refs
    elif mode == "diff":
        k_ref, v_ref, lam_ref, sg_ref, o_ref = refs
    else:
        k_ref, v_ref, o_ref = refs
    tq = q_parts * q_refs[0].shape[0]
    j = pl.program_id(1)
    qi = pl.program_id(2)

    q = q_refs[0][...] if q_parts == 1 else jnp.concatenate([r[...] for r in q_refs], axis=0)
    lane = lax.broadcasted_iota(I32, (tq, LANES), 1)
    low = lane < HEAD_DIM
    zero = jnp.zeros_like(q)
    q2 = jnp.concatenate([jnp.where(low, q, zero), jnp.where(low, zero, q)], axis=0)

    if mode == "window":
        s0 = sink_ref[pl.ds(2 * j, 1), :]
        s1 = sink_ref[pl.ds(2 * j + 1, 1), :]
        row = lax.broadcasted_iota(I32, (2 * tq, LANES), 0)
        sink = jnp.where(row < tq, s0, s1) * LOG2E
        m0 = sink
    else:
        m0 = jnp.full((2 * tq, LANES), NEG_INF, F32)
    state = (m0, jnp.zeros((2 * tq, LANES), F32), jnp.zeros((2 * tq, LANES), F32))

    def chunk(state, start, valid, size=KV_TILE):
        m_prev, l_prev, acc_prev = state
        k = k_ref[pl.ds(start, size), :]
        v = v_ref[pl.ds(start, size), :]
        s = lax.dot_general(q2, k, (((1,), (1,)), ((), ())), preferred_element_type=F32)
        if valid is not None:
            s = jnp.where(valid, s, NEG_INF)
        m_new = jnp.maximum(m_prev, jnp.max(s, axis=1, keepdims=True))
        alpha = jnp.exp2(m_prev - m_new)
        p = jnp.exp2(s - jnp.concatenate([m_new] * (size // LANES), axis=1))
        part = p[:, :LANES]
        for c in range(1, size // LANES):
            part = part + p[:, c * LANES:(c + 1) * LANES]
        return (m_new, alpha * l_prev + part,
                alpha * acc_prev + jnp.dot(p.astype(BF16), v, preferred_element_type=F32))

    state = chunk(state, 0, None)
    if queries == "latent" and mode == "window":
        n_win = (tq + 2 * WINDOW) // KV_TILE
        q0 = qi * tq
        kstart = jnp.clip(q0 - WINDOW, 0, seq - n_win * KV_TILE)
        r = lax.broadcasted_iota(I32, (2 * tq, KV_TILE), 0)
        qpos = q0 + jnp.where(r >= tq, r - tq, r)
        col = lax.broadcasted_iota(I32, (2 * tq, KV_TILE), 1)
        for w in range(n_win):
            kpos = kstart + w * KV_TILE + col
            state = chunk(state, pl.multiple_of(ctx_len + kstart + w * KV_TILE, WINDOW),
                          jnp.abs(qpos - kpos) <= WINDOW)
    elif queries == "latent":
        for c in range(seq // LAT_KV_TILE):
            state = chunk(state, ctx_len + c * LAT_KV_TILE, None, LAT_KV_TILE)

    m_fin, l_part, acc = state
    l = jnp.sum(l_part, axis=1, keepdims=True)
    if mode == "window":
        l = l + jnp.exp2(sink - m_fin)[:, :1]
    o2 = acc / l
    if mode == "diff":
        lp = lam_ref[...]
        lam = (jnp.exp(jnp.sum(lp[0:1] * lp[1:2], axis=1, keepdims=True))
               - jnp.exp(jnp.sum(lp[2:3] * lp[3:4], axis=1, keepdims=True)) + lambda_init)
        o = o2[:tq] - lam * o2[tq:]
        o = _rms(o, sg_ref[...]) * (1.0 - lambda_init)
    else:
        o = jnp.where(low, o2[:tq], o2[tq:])
    o_ref[...] = o.astype(BF16)


def _attention(p, extra, *, mode, queries, batch, seq, ctx_len, n_qblocks, q_col0, k_col, v_col,
               lambda_init=0.0, q_parts=1):
    n_p = ctx_len + seq
    tiles = n_p // Q_TILE
    tq = q_parts * Q_TILE
    n_rows = seq if queries == "latent" else ctx_len
    qt = n_rows // tq
    q_off = ctx_len // Q_TILE if queries == "latent" else 0
    in_specs = [pl.BlockSpec((Q_TILE, LANES),
                             lambda b, j, qi, part=part: (b * tiles + q_off + qi * q_parts + part, q_col0 + j))
                for part in range(q_parts)]
    in_specs += [pl.BlockSpec((n_p, LANES), lambda b, j, qi: (b, k_col(j))),
                 pl.BlockSpec((n_p, LANES), lambda b, j, qi: (b, v_col(j)))]
    args = [p] * (q_parts + 2)
    for e in extra:
        in_specs.append(pl.BlockSpec(e.shape, lambda b, j, qi: (0, 0)))
        args.append(e)
    kern = functools.partial(_attn_kernel, mode=mode, queries=queries, seq=seq, ctx_len=ctx_len,
                             lambda_init=lambda_init, q_parts=q_parts)
    return pl.pallas_call(
        kern,
        out_shape=jax.ShapeDtypeStruct((batch * n_rows, n_qblocks * LANES), BF16),
        grid=(batch, n_qblocks, qt),
        in_specs=in_specs,
        out_specs=pl.BlockSpec((tq, LANES), lambda b, j, qi: (b * qt + qi, j)),
        compiler_params=_cparams(("arbitrary", "arbitrary", "arbitrary")),
        name="attn_" + mode + "_" + queries,
    )(*args)


def _out_kernel(*refs, n_o, tiles):
    n_in = 2 * n_o if tiles else n_o
    o_refs = refs[:n_in]
    w_refs = refs[n_in:n_in + n_o]
    rest = refs[n_in + n_o:]
    x_refs, rest = (rest[:2], rest[2:]) if tiles else (rest[:1], rest[1:])
    g1_ref, gate_ref, g2_ref, sh_ref, sc_ref, rw_ref, xn_ref, hp_ref, lg_ref = rest
    is_ctx = (pl.program_id(0) % tiles == 0) if tiles else None

    def residual_rows(rows):
        if not tiles:
            return x_refs[0][rows, :]
        return jnp.where(is_ctx, x_refs[0][rows, :], x_refs[1][rows, :])

    def mixer_rows(a, rows):
        if not tiles:
            return o_refs[a][rows, :]
        return jnp.where(is_ctx, o_refs[2 * a][rows, :], o_refs[2 * a + 1][rows, :])

    slab = LANES
    for r0 in range(0, xn_ref.shape[0], slab):
        rows = slice(r0, r0 + slab)
        y = jnp.dot(mixer_rows(0, rows), w_refs[0][...], preferred_element_type=F32)
        for a in range(1, n_o):
            y = y + jnp.dot(mixer_rows(a, rows), w_refs[a][...], preferred_element_type=F32)
        xn = residual_rows(rows) + gate_ref[...] * _rms(y, g1_ref[...])
        xn_ref[rows, :] = xn
        h = _rms(xn, g2_ref[...]) * (1.0 + sc_ref[...]) + sh_ref[...]
        hp_ref[rows, :] = _pack_halves(h)
        lg_ref[:, rows] = lax.dot_general(rw_ref[...], h, (((1,), (1,)), ((), ())), precision=HIGHEST,
                                          preferred_element_type=F32)


def _out_project(os_, ws, x, x_tile, modr, mod_base, row_of_tile, g1, g2, rw_t, n_tiles, tiles=None):
    d = x[0].shape[1] if tiles else x.shape[1]
    n_o = len(os_)
    tm = n_tiles * ROW_TILE
    mspec = lambda which: pl.BlockSpec((None, 1, d), lambda i: (mod_base + row_of_tile(i) * 6 + which, 0, 0))
    if tiles:
        o_specs, o_args = [], []
        for pair in os_:
            o_specs += _pair_specs(pair, tiles)
            o_args += list(pair)
        x_specs, x_args = _pair_specs(x, tiles), list(x)
    else:
        o_specs = [pl.BlockSpec((ROW_TILE, o.shape[1]), lambda i: (i, 0)) for o in os_]
        o_args = list(os_)
        x_specs, x_args = [pl.BlockSpec((ROW_TILE, d), lambda i: (x_tile(i), 0))], [x]
    in_specs = (o_specs
                + [pl.BlockSpec(w.shape, lambda i: (0, 0)) for w in ws]
                + x_specs
                + [pl.BlockSpec((1, d), lambda i: (0, 0)), mspec(2),
                   pl.BlockSpec((1, d), lambda i: (0, 0)), mspec(3), mspec(4),
                   pl.BlockSpec(rw_t.shape, lambda i: (0, 0))])
    return pl.pallas_call(
        functools.partial(_out_kernel, n_o=n_o, tiles=tiles),
        out_shape=(jax.ShapeDtypeStruct((tm, d), F32), jax.ShapeDtypeStruct((tm, HALF), U32),
                   jax.ShapeDtypeStruct((N_EXPERTS, tm), F32)),
        grid=(n_tiles,),
        in_specs=in_specs,
        out_specs=(pl.BlockSpec((ROW_TILE, d), lambda i: (i, 0)),
                   pl.BlockSpec((ROW_TILE, HALF), lambda i: (i, 0)),
                   pl.BlockSpec((N_EXPERTS, ROW_TILE), lambda i: (0, i))),
        compiler_params=_cparams(("arbitrary",)),
        name="out_proj",
    )(*o_args, *ws, *x_args, g1, modr, g2, modr, modr, rw_t)


def _route_kernel(lg_ref, bias_ref, tri_ref, ltri_ref, ls_ref, w_ref, segn_ref, segc_ref, cnt_ref, carry_ref):
    i = pl.program_id(0)

    @pl.when(i == 0)
    def _():
        carry_ref[...] = jnp.zeros(carry_ref.shape, F32)

    tr = lg_ref.shape[1]
    score = jax.nn.sigmoid(lg_ref[...])
    sel = score + bias_ref[...][:, :1]
    sel_j = [sel[j * GROUP_SIZE:(j + 1) * GROUP_SIZE] for j in range(GROUP_SIZE)]
    sc_j = [score[j * GROUP_SIZE:(j + 1) * GROUP_SIZE] for j in range(GROUP_SIZE)]
    gi = lax.broadcasted_iota(I32, (N_GROUPS, tr), 0)

    m1 = sel_j[0]
    m2 = jnp.full_like(m1, -jnp.inf)
    for j in range(1, GROUP_SIZE):
        m2 = jnp.maximum(m2, jnp.minimum(m1, sel_j[j]))
        m1 = jnp.maximum(m1, sel_j[j])
    gs = m1 + m2

    grank = jnp.zeros((N_GROUPS, tr), I32)
    for gp in range(N_GROUPS):
        rowv = gs[gp:gp + 1, :]
        grank = grank + jnp.where(gi > gp, jnp.where(rowv >= gs, 1, 0), jnp.where(rowv > gs, 1, 0))
    gmask = grank < TOPK_GROUPS
    val_j = [jnp.where(gmask, sel_j[j], NEG_INF) for j in range(GROUP_SIZE)]

    e_j = [(gi * GROUP_SIZE + j).astype(F32) for j in range(GROUP_SIZE)]
    work_j = list(val_j)
    picked_j = [jnp.zeros((N_GROUPS, tr), F32) for _ in range(GROUP_SIZE)]
    idx_k = []
    for k in range(TOP_K):
        best = work_j[0]
        for j in range(1, GROUP_SIZE):
            best = jnp.maximum(best, work_j[j])
        best = jnp.max(best, axis=0, keepdims=True)
        first = jnp.where(work_j[0] == best, e_j[0], float(N_EXPERTS))
        for j in range(1, GROUP_SIZE):
            first = jnp.minimum(first, jnp.where(work_j[j] == best, e_j[j], float(N_EXPERTS)))
        first = jnp.min(first, axis=0, keepdims=True)
        idx_k.append(first)
        for j in range(GROUP_SIZE):
            hit = e_j[j] == first
            work_j[j] = jnp.where(hit, -jnp.inf, work_j[j])
            picked_j[j] = jnp.where(hit, 1.0, picked_j[j])

    chosen = jnp.concatenate(picked_j, axis=0)
    n_run = jnp.floor((jnp.sum(chosen, axis=1, keepdims=True) + (SEG_ALIGN - 1.0)) * (1.0 / SEG_ALIGN)) * SEG_ALIGN
    n_run = jnp.broadcast_to(n_run, (N_EXPERTS, LANES))
    run_start = jnp.dot(ltri_ref[...], n_run.astype(BF16), preferred_element_type=F32)
    local = jnp.dot(chosen.astype(BF16), tri_ref[...], preferred_element_type=F32) + run_start[:, :1]
    segn_ref[0] = n_run.astype(I32)
    segc_ref[0] = carry_ref[...].astype(I32)
    carry_ref[...] = carry_ref[...] + n_run
    cnt_ref[...] = carry_ref[...].astype(I32)

    w_rows, ls_rows = [], []
    for k in range(TOP_K):
        w_acc = jnp.zeros((N_GROUPS, tr), F32)
        p_acc = jnp.zeros((N_GROUPS, tr), F32)
        for j in range(GROUP_SIZE):
            hit = e_j[j] == idx_k[k]
            w_acc = w_acc + jnp.where(hit, sc_j[j], 0.0)
            p_acc = p_acc + jnp.where(hit, local[j * GROUP_SIZE:(j + 1) * GROUP_SIZE], 0.0)
        w_rows.append(jnp.sum(w_acc, axis=0, keepdims=True))
        ls_rows.append(jnp.sum(p_acc, axis=0, keepdims=True))
    w_all = jnp.concatenate(w_rows, axis=0)
    w_ref[...] = w_all / jnp.sum(w_all, axis=0, keepdims=True) * ROUTED_SCALE
    ls_ref[...] = jnp.concatenate(ls_rows, axis=0).astype(I32)


def _route(lg_t, bias_rep):
    tm = lg_t.shape[1]
    n_tiles = tm // MOVE_TILE
    tri = jnp.asarray(np.triu(np.ones((MOVE_TILE, MOVE_TILE), np.float32), 1), BF16)
    ltri = jnp.asarray(np.tril(np.ones((N_EXPERTS, N_EXPERTS), np.float32), -1), BF16)
    tok = lambda i: (0, i)
    per_tile = pl.BlockSpec((1, N_EXPERTS, LANES), lambda i: (i, 0, 0))
    return pl.pallas_call(
        _route_kernel,
        out_shape=(jax.ShapeDtypeStruct((TOP_K, tm), I32), jax.ShapeDtypeStruct((TOP_K, tm), F32),
                   jax.ShapeDtypeStruct((n_tiles, N_EXPERTS, LANES), I32),
                   jax.ShapeDtypeStruct((n_tiles, N_EXPERTS, LANES), I32),
                   jax.ShapeDtypeStruct((N_EXPERTS, LANES), I32)),
        grid=(n_tiles,),
        in_specs=[pl.BlockSpec((N_EXPERTS, MOVE_TILE), tok),
                  pl.BlockSpec((N_EXPERTS, LANES), lambda i: (0, 0)),
                  pl.BlockSpec((MOVE_TILE, MOVE_TILE), lambda i: (0, 0)),
                  pl.BlockSpec((N_EXPERTS, N_EXPERTS), lambda i: (0, 0))],
        out_specs=(pl.BlockSpec((TOP_K, MOVE_TILE), tok), pl.BlockSpec((TOP_K, MOVE_TILE), tok),
                   per_tile, per_tile, pl.BlockSpec((N_EXPERTS, LANES), lambda i: (0, 0))),
        scratch_shapes=[pltpu.VMEM((N_EXPERTS, LANES), F32)],
        compiler_params=_cparams(("arbitrary",)),
        name="route",
    )(lg_t, bias_rep, tri, ltri)


def _start_pieces(n, src, dst, make, pieces):
    for size in pieces:
        above = n & (-2 * size)

        @pl.when((n & size) != 0)
        def _():
            make(pl.multiple_of(src + above, SEG_ALIGN), pl.multiple_of(dst + above, SEG_ALIGN), size).start()


def _start_run_copies(i, segn_ref, segd_ref, make, straight_line=False):
    split = RUN_PIECES.index(64)

    def per_run(r, src):
        n = segn_ref[i * N_EXPERTS + r]
        dst = segd_ref[i * N_EXPERTS + r]

        @pl.when(n >= RUN_PIECES[split - 1])
        def _():
            _start_pieces(n, src, dst, make, RUN_PIECES[:split])

        _start_pieces(n, src, dst, make, RUN_PIECES[split:])
        return src + n

    if not straight_line:
        return lax.fori_loop(0, N_EXPERTS, per_run, 0, unroll=2)
    src = 0
    for r in range(N_EXPERTS):
        src = per_run(r, src)
    return src


def _wait_rows(total, make):
    for size in WAIT_PIECES:
        @pl.when((total & size) != 0)
        def _():
            make(0, 0, size).wait()


def _fill_copies(zstart_ref, zlen_ref, make):
    pieces = tuple(p for p in WAIT_PIECES if p < EXPERT_BLOCK)

    def per_expert(r, carry):
        n = zlen_ref[r]
        _start_pieces(n, 0, zstart_ref[r], make, pieces)
        _wait_rows(n, make)
        return carry

    lax.fori_loop(0, N_EXPERTS, per_expert, 0)


def _dispatch_kernel(segn_ref, segd_ref, zstart_ref, zlen_ref, ls_ref, hp_ref, xs_hbm, stage_ref, zero_ref,
                     rows_ref, sem, zsem):
    i = pl.program_id(0)
    buf = i % 2
    lo, hi = _unpack_halves(hp_ref[...])
    lo = lo.astype(BF16)
    hi = hi.astype(BF16)
    ls = ls_ref[...]
    pack_rows = 16
    ls16 = [jnp.tile(jnp.broadcast_to(ls[k:k + 1, :], (pack_rows, MOVE_TILE)).astype(jnp.int16),
                     (STAGE_CHUNK // pack_rows, 1)) for k in range(TOP_K)]
    slot0 = lax.broadcasted_iota(I32, (STAGE_CHUNK, MOVE_TILE), 0).astype(jnp.int16)
    one = jnp.ones((STAGE_CHUNK, MOVE_TILE), BF16)
    for c in range(STAGE_ROWS // STAGE_CHUNK):
        slot = slot0 + jnp.int16(c * STAGE_CHUNK)
        onehot = jnp.zeros((STAGE_CHUNK, MOVE_TILE), BF16)
        for k in range(TOP_K):
            onehot = jnp.where(ls16[k] == slot, one, onehot)
        a = lax.bitcast_convert_type(jnp.dot(onehot, lo, preferred_element_type=F32), U32) >> 16
        b = lax.bitcast_convert_type(jnp.dot(onehot, hi, preferred_element_type=F32), U32)
        stage_ref[buf, c * STAGE_CHUNK:(c + 1) * STAGE_CHUNK, :] = b | a

    def to_slots(which):
        def make(src, dst, size):
            return pltpu.make_async_copy(stage_ref.at[which, pl.ds(src, size), :],
                                         xs_hbm.at[pl.ds(dst, size), :], sem.at[which])
        return make

    rows_ref[buf] = _start_run_copies(i, segn_ref, segd_ref, to_slots(buf))

    @pl.when(i > 0)
    def _():
        _wait_rows(rows_ref[1 - buf], to_slots(1 - buf))

    @pl.when(i == pl.num_programs(0) - 1)
    def _():
        _wait_rows(rows_ref[buf], to_slots(buf))

    @pl.when(i == 0)
    def _():
        zero_ref[...] = jnp.zeros(zero_ref.shape, U32)

        def zeros_to_slots(src, dst, size):
            return pltpu.make_async_copy(zero_ref.at[pl.ds(0, size), :], xs_hbm.at[pl.ds(dst, size), :], zsem)

        _fill_copies(zstart_ref, zlen_ref, zeros_to_slots)


def _dispatch(segn, segd, zstart, zlen, ls, hp, n_slots):
    tm = hp.shape[0]
    tile = lambda i, *_: (i, 0)
    return pl.pallas_call(
        _dispatch_kernel,
        out_shape=jax.ShapeDtypeStruct((n_slots, HALF), U32),
        grid_spec=pltpu.PrefetchScalarGridSpec(
            num_scalar_prefetch=4,
            grid=(tm // MOVE_TILE,),
            in_specs=[pl.BlockSpec((TOP_K, MOVE_TILE), lambda i, *_: (0, i)),
                      pl.BlockSpec((MOVE_TILE, HALF), tile)],
            out_specs=pl.BlockSpec(memory_space=pl.ANY),
            scratch_shapes=[pltpu.VMEM((2, STAGE_ROWS, HALF), U32), pltpu.VMEM((EXPERT_BLOCK // 2, HALF), U32),
                            pltpu.SMEM((2,), I32), pltpu.SemaphoreType.DMA((2,)), pltpu.SemaphoreType.DMA]),
        compiler_params=_cparams(("arbitrary",)),
        name="dispatch",
    )(segn, segd, zstart, zlen, ls, hp)


def _expert_kernel(be_ref, nu_ref, nxt_ref, xs_ref, w1_hbm, w3_hbm, w2_hbm, ys_ref, w1f_ref, w3f_ref, w2f_ref,
                   w1b_ref, w3b_ref, w2b_ref, run_ref, sem, *, layer):
    i = pl.program_id(0)

    def fetch(e, slot):
        return [pltpu.make_async_copy(src.at[layer, e], dst.at[slot], sem.at[slot])
                for src, dst in ((w1_hbm, w1f_ref), (w3_hbm, w3f_ref), (w2_hbm, w2f_ref))]

    @pl.when(i == 0)
    def _():
        run_ref[0] = 0
        for cp in fetch(be_ref[0], 0):
            cp.start()

    first = jnp.logical_or(i == 0, be_ref[i] != be_ref[jnp.maximum(i - 1, 0)])

    @pl.when(jnp.logical_and(first, i < nu_ref[0]))
    def _():
        slot = run_ref[0] % 2
        for cp in fetch(be_ref[i], slot):
            cp.wait()
        w1b_ref[...] = w1f_ref[slot].astype(BF16)
        w3b_ref[...] = w3f_ref[slot].astype(BF16)
        w2b_ref[...] = w2f_ref[slot].astype(BF16)
        nxt = nxt_ref[i]

        @pl.when(nxt >= 0)
        def _():
            for cp in fetch(nxt, 1 - slot):
                cp.start()

        run_ref[0] = run_ref[0] + 1

    @pl.when(i < nu_ref[0])
    def _():
        for rows in (slice(r0, r0 + EXPERT_SLAB) for r0 in range(0, EXPERT_BLOCK, EXPERT_SLAB)):
            lo, hi = _unpack_halves(xs_ref[rows, :])
            lo = lo.astype(BF16)
            hi = hi.astype(BF16)
            h1 = (jnp.dot(lo, w1b_ref[:HALF, :], preferred_element_type=F32)
                  + jnp.dot(hi, w1b_ref[HALF:, :], preferred_element_type=F32))
            h3 = (jnp.dot(lo, w3b_ref[:HALF, :], preferred_element_type=F32)
                  + jnp.dot(hi, w3b_ref[HALF:, :], preferred_element_type=F32))
            g = (h1 * jax.nn.sigmoid(h1) * h3).astype(BF16)
            y = jnp.dot(g, w2b_ref[...], preferred_element_type=F32)
            ys_ref[rows, :] = _pack_halves(y)


def _experts(block_e, n_used, next_e, xs, w1, w3, w2, layer):
    n_slots = xs.shape[0]
    n_blocks = n_slots // EXPERT_BLOCK
    d, eh = w1.shape[2], w1.shape[3]
    rows = lambda i, be, nu, nx: (jnp.minimum(i, nu[0] - 1), 0)
    anyspace = pl.BlockSpec(memory_space=pl.ANY)
    return pl.pallas_call(
        functools.partial(_expert_kernel, layer=layer),
        out_shape=jax.ShapeDtypeStruct((n_slots, HALF), U32),
        grid_spec=pltpu.PrefetchScalarGridSpec(
            num_scalar_prefetch=3,
            grid=(n_blocks,),
            in_specs=[pl.BlockSpec((EXPERT_BLOCK, HALF), rows), anyspace, anyspace, anyspace],
            out_specs=pl.BlockSpec((EXPERT_BLOCK, HALF), rows),
            scratch_shapes=[pltpu.VMEM((2, d, eh), F32), pltpu.VMEM((2, d, eh), F32), pltpu.VMEM((2, eh, d), F32),
                            pltpu.VMEM((d, eh), BF16), pltpu.VMEM((d, eh), BF16), pltpu.VMEM((eh, d), BF16),
                            pltpu.SMEM((1,), I32), pltpu.SemaphoreType.DMA((2,))]),
        compiler_params=_cparams(("arbitrary",)),
        name="experts",
    )(block_e, n_used, next_e, xs, w1, w3, w2)


def _combine_kernel(segn_ref, segd_ref, ys_hbm, lst_ref, wt_ref, hp_ref, x_ref, s1_ref, s3_ref, s2_ref, g_ref,
                    gate_ref, o_ref, stage_ref, rows_ref, sem):
    i = pl.program_id(0)
    buf = i % 2

    def from_slots(which):
        def make(src, dst, size):
            return pltpu.make_async_copy(ys_hbm.at[pl.ds(dst, size), :],
                                         stage_ref.at[which, pl.ds(src, size), :], sem.at[which])
        return make

    @pl.when(i == 0)
    def _():
        stage_ref[...] = jnp.zeros(stage_ref.shape, U32)
        rows_ref[0] = _start_run_copies(0, segn_ref, segd_ref, from_slots(0))

    last = pl.num_programs(0) - 1
    rows_ref[1 - buf] = _start_run_copies(jnp.minimum(i + 1, last), segn_ref, segd_ref, from_slots(1 - buf),
                                          straight_line=True)

    lo, hi = _unpack_halves(hp_ref[...])
    lo = lo.astype(BF16)
    hi = hi.astype(BF16)
    h1 = (jnp.dot(lo, s1_ref[:HALF, :], preferred_element_type=F32)
          + jnp.dot(hi, s1_ref[HALF:, :], preferred_element_type=F32))
    h3 = (jnp.dot(lo, s3_ref[:HALF, :], preferred_element_type=F32)
          + jnp.dot(hi, s3_ref[HALF:, :], preferred_element_type=F32))
    y = jnp.dot((h1 * jax.nn.sigmoid(h1) * h3).astype(BF16), s2_ref[...], preferred_element_type=F32)

    _wait_rows(rows_ref[buf], from_slots(buf))

    @pl.when(i == last)
    def _():
        _wait_rows(rows_ref[1 - buf], from_slots(1 - buf))

    lst = lst_ref[...]
    wt = wt_ref[...]
    lst16 = [jnp.broadcast_to(lst[:, k:k + 1], (MOVE_TILE, STAGE_CHUNK)).astype(jnp.int16) for k in range(TOP_K)]
    wt16 = [jnp.broadcast_to(wt[:, k:k + 1], (MOVE_TILE, STAGE_CHUNK)).astype(BF16) for k in range(TOP_K)]
    slot0 = lax.broadcasted_iota(I32, (MOVE_TILE, STAGE_CHUNK), 1).astype(jnp.int16)
    r_lo = jnp.zeros((MOVE_TILE, HALF), F32)
    r_hi = jnp.zeros((MOVE_TILE, HALF), F32)
    for c in range(STAGE_ROWS // STAGE_CHUNK):
        slot = slot0 + jnp.int16(c * STAGE_CHUNK)
        wsel = jnp.zeros((MOVE_TILE, STAGE_CHUNK), BF16)
        for k in range(TOP_K):
            wsel = jnp.where(lst16[k] == slot, wt16[k], wsel)
        a, b = _unpack_halves(stage_ref[buf, c * STAGE_CHUNK:(c + 1) * STAGE_CHUNK, :])
        r_lo = r_lo + jnp.dot(wsel, a.astype(BF16), preferred_element_type=F32)
        r_hi = r_hi + jnp.dot(wsel, b.astype(BF16), preferred_element_type=F32)
    y = y + jnp.concatenate([r_lo, r_hi], axis=1)
    o_ref[...] = x_ref[...] + gate_ref[...] * _rms(y, g_ref[...])


def _combine(segn, segd, ys, lst, wt, hp, x, s1, s3, s2, g, modr, mod_base, row_of_tile):
    tm, d = x.shape
    full = lambda a: pl.BlockSpec(a.shape, lambda i, *_: (0, 0))
    tile = lambda i, *_: (i, 0)
    return pl.pallas_call(
        _combine_kernel,
        out_shape=jax.ShapeDtypeStruct((tm, d), F32),
        grid_spec=pltpu.PrefetchScalarGridSpec(
            num_scalar_prefetch=2,
            grid=(tm // MOVE_TILE,),
            in_specs=[pl.BlockSpec(memory_space=pl.ANY),
                      pl.BlockSpec((MOVE_TILE, TOP_K), tile), pl.BlockSpec((MOVE_TILE, TOP_K), tile),
                      pl.BlockSpec((MOVE_TILE, HALF), tile), pl.BlockSpec((MOVE_TILE, d), tile),
                      full(s1), full(s3), full(s2), full(g),
                      pl.BlockSpec((None, 1, d), lambda i, *_: (mod_base + row_of_tile(i) * 6 + 5, 0, 0))],
            out_specs=pl.BlockSpec((MOVE_TILE, d), tile),
            scratch_shapes=[pltpu.VMEM((2, STAGE_ROWS, HALF), U32), pltpu.SMEM((2,), I32),
                            pltpu.SemaphoreType.DMA((2,))]),
        compiler_params=_cparams(("arbitrary",)),
        name="combine",
    )(segn, segd, ys, lst, wt, hp, x, s1, s3, s2, g, modr)


def _moe(xn, hp, lg_t, bias, expert_weights, s1, s3, s2, g, modr, mod_base, row_of_tile):
    tm = xn.shape[0]
    n_tiles = tm // MOVE_TILE
    perm = np.array([(r % GROUP_SIZE) * GROUP_SIZE + r // GROUP_SIZE for r in range(N_EXPERTS)])
    bias_rep = jnp.broadcast_to(bias.astype(F32)[perm][:, None], (N_EXPERTS, LANES))
    ls, w, segn, segc, cnt = _route(lg_t, bias_rep)

    total = cnt[:, 0]
    padded = (total + EXPERT_BLOCK - 1) // EXPERT_BLOCK * EXPERT_BLOCK
    pad_end = jnp.cumsum(padded)
    pad_start = pad_end - padded
    n_slots = tm * TOP_K + (SEG_ALIGN - 1) * N_EXPERTS * n_tiles + N_EXPERTS * EXPERT_BLOCK
    n_blocks = -(-n_slots // EXPERT_BLOCK)
    starts = jnp.arange(n_blocks, dtype=I32) * EXPERT_BLOCK
    region = jnp.minimum(jnp.sum((pad_end[None, :] <= starts[:, None]).astype(I32), axis=1), N_EXPERTS - 1)
    block_e = jnp.asarray(perm, I32)[region]
    n_used = (pad_end[-1:] // EXPERT_BLOCK).astype(I32)
    after = pad_end[region] // EXPERT_BLOCK
    next_e = jnp.where(after < n_used[0], block_e[jnp.minimum(after, n_blocks - 1)], -1).astype(I32)
    segn = segn[:, :, 0].reshape(-1)
    segd = (segc[:, :, 0] + pad_start[None, :]).reshape(-1).astype(I32)

    xs = _dispatch(segn, segd, (pad_start + total).astype(I32), (padded - total).astype(I32), ls, hp,
                   n_blocks * EXPERT_BLOCK)
    ys = _experts(block_e, n_used, next_e, xs, *expert_weights)
    return _combine(segn, segd, ys, ls.T, w.T, hp, xn, s1.astype(BF16), s3.astype(BF16), s2.astype(BF16), g, modr,
                    mod_base, row_of_tile)


_DEINTERLEAVE = np.concatenate([np.arange(0, HEAD_DIM, 2), np.arange(1, HEAD_DIM, 2)])


def _rope_tables(seq, ctx_len):
    t = np.arange(seq)
    n_pair = HEAD_DIM // 4
    inv = jnp.asarray(ROPE_THETA, F32) ** (-jnp.arange(n_pair, dtype=F32) / n_pair)
    r = jnp.asarray(t // GRID_W, F32)
    c = jnp.asarray(t % GRID_W, F32)
    ang = jnp.concatenate([r[:, None] * inv, c[:, None] * inv], axis=-1)
    cos, sin = jnp.cos(ang), jnp.sin(ang)
    cos_t = jnp.tile(jnp.concatenate([cos, cos], axis=-1), (1, LANES // HEAD_DIM))
    sin_t = jnp.tile(jnp.concatenate([-sin, sin], axis=-1), (1, LANES // HEAD_DIM))
    cos_t = jnp.concatenate([jnp.ones((ctx_len, LANES), F32), cos_t], axis=0)
    sin_t = jnp.concatenate([jnp.zeros((ctx_len, LANES), F32), sin_t], axis=0)
    return cos_t, sin_t


def _ab_layout():
    cols, blocks, gain_kind = [], [], []
    for base_q, base_k, base_v, normed in ((0, 512, 640, True), (768, 1280, 1408, False)):
        for jb in range(4):
            cols += [base_q + h * HEAD_DIM + _DEINTERLEAVE for h in (2 * jb, 2 * jb + 1)]
            blocks.append((normed, True, Q_SCALE))
            gain_kind.append("q" if normed else None)
        for kvh in range(2):
            cols += [base_k + kvh * HEAD_DIM + _DEINTERLEAVE] * 2
            blocks.append((normed, True, 1.0))
            gain_kind.append("k" if normed else None)
        for kvh in range(2):
            cols += [base_v + kvh * HEAD_DIM + np.arange(HEAD_DIM)] * 2
            blocks.append((False, False, 1.0))
            gain_kind.append(None)
    return np.concatenate(cols), tuple(blocks), gain_kind


def _c_layout():
    cols, blocks = [], []
    for base, rope, scale in ((0, True, Q_SCALE), (1024, True, 1.0)):
        for h in range(8):
            cols += [base + (2 * h + m) * HEAD_DIM + _DEINTERLEAVE for m in range(2)]
            blocks.append((False, rope, scale))
    for h in range(8):
        cols.append(2048 + h * LANES + np.arange(LANES))
        blocks.append((False, False, 1.0))
    return np.concatenate(cols), tuple(blocks)


def kernel(x, c, ctx, c_ctx, ada_w, ada_b, norm_g, ab_w_in, ab_w_out, a_q_norm, a_k_norm, b_sink,
           c_w_in, c_w_out, c_lambda, c_subln_g, router_w, router_bias, exp_w1, exp_w3, exp_w2,
           sh_w1, sh_w3, sh_w2):
    batch, seq, d = x.shape
    ctx_len = ctx.shape[1]
    depth = ada_w.shape[0]
    assert d == D_MODEL and ctx_len == ROW_TILE and depth == 2
    assert seq % (4 * Q_TILE) == 0 and batch <= 4
    n_p = ctx_len + seq
    tiles = n_p // ROW_TILE
    lat_tiles = seq // ROW_TILE
    t_all = batch * n_p

    cond = jnp.zeros((8, d), F32).at[:batch].set(c).at[4].set(c_ctx)
    modr = _modulation(cond, ada_w, ada_b).reshape(depth * 8 * 6, 1, d)
    row_all = lambda i: jnp.where(i % tiles == 0, 4, i // tiles)
    cos_t, sin_t = _rope_tables(seq, ctx_len)
    rperm = np.array([(r % GROUP_SIZE) * GROUP_SIZE + r // GROUP_SIZE for r in range(N_EXPERTS)])
    xt = (ctx.reshape(batch * ctx_len, d), x.reshape(batch * seq, d))

    cols, blocks, gain_kind = _ab_layout()
    w0 = ab_w_in[0][:, cols].astype(BF16)
    gq = jnp.tile(a_q_norm[0][_DEINTERLEAVE], 2)
    gk = jnp.tile(a_k_norm[0][_DEINTERLEAVE], 2)
    ones = jnp.ones((LANES,), F32)
    head_gain = jnp.concatenate([{"q": gq, "k": gk, None: ones}[kind] for kind in gain_kind])[None, :]
    p0 = _project(xt, modr, 0, row_all, norm_g[0, 0][None, :], w0, cos_t, sin_t, head_gain, blocks, tiles)
    common = dict(batch=batch, seq=seq, ctx_len=ctx_len, n_qblocks=4)
    dense = dict(mode="dense", q_col0=0, k_col=lambda j: 4 + j // 2, v_col=lambda j: 6 + j // 2, **common)
    window = dict(mode="window", q_col0=8, k_col=lambda j: 12 + j // 2, v_col=lambda j: 14 + j // 2, **common)
    sink = jnp.broadcast_to(b_sink[0].astype(F32)[:, None], (8, LANES))
    oa = (_attention(p0, [], queries="context", **dense), _attention(p0, [], queries="latent", q_parts=4, **dense))
    ob = (_attention(p0, [sink], queries="context", **window),
          _attention(p0, [sink], queries="latent", **window))
    w_out = ab_w_out[0].astype(BF16)
    x1, hp, lg = _out_project([oa, ob], [w_out[:512], w_out[512:]], xt, lambda i: i, modr, 0, row_all,
                              norm_g[0, 1][None, :], norm_g[0, 2][None, :], router_w[0].T[rperm],
                              t_all // ROW_TILE, tiles=tiles)
    x2 = _moe(x1, hp, lg, router_bias[0], (exp_w1, exp_w3, exp_w2, 0), sh_w1[0], sh_w3[0], sh_w2[0],
              norm_g[0, 3][None, :], modr, 0, lambda i: row_all(i // (ROW_TILE // MOVE_TILE)))

    base1 = 8 * 6
    lambda_init = 0.8 - 0.6 * math.exp(-0.3 * 1)
    cols1, blocks1 = _c_layout()
    w1p = c_w_in[0][:, cols1].astype(BF16)
    p1 = _project(x2, modr, base1, row_all, norm_g[1, 0][None, :], w1p, cos_t, sin_t,
                  jnp.ones((1, w1p.shape[1]), F32), blocks1, tiles)
    lam = jnp.zeros((8, LANES), F32).at[:4, :HEAD_DIM].set(c_lambda[0].astype(F32))
    oc = _attention(p1, [lam, c_subln_g[0][None, :]], mode="diff", queries="latent", batch=batch, seq=seq,
                    ctx_len=ctx_len, n_qblocks=8, q_col0=0, k_col=lambda j: 8 + j, v_col=lambda j: 16 + j,
                    lambda_init=lambda_init, q_parts=4)
    row_lat = lambda i: i // lat_tiles
    x3, hp1, lg1 = _out_project([oc], [c_w_out[0].astype(BF16)], x2,
                                lambda i: (i // lat_tiles) * tiles + 1 + i % lat_tiles, modr, base1, row_lat,
                                norm_g[1, 1][None, :], norm_g[1, 2][None, :], router_w[1].T[rperm],
                                batch * lat_tiles)
    out = _moe(x3, hp1, lg1, router_bias[1], (exp_w1, exp_w3, exp_w2, 1), sh_w1[1], sh_w3[1], sh_w2[1],
               norm_g[1, 3][None, :], modr, base1, lambda i: row_lat(i // (ROW_TILE // MOVE_TILE)))
    return out.reshape(batch, seq, d)
```

```python
import functools
import math

import numpy as np
import jax
import jax.numpy as jnp
from jax import lax
from jax.experimental import pallas as pl
from jax.experimental.pallas import tpu as pltpu

F32 = jnp.float32
BF16 = jnp.bfloat16
U32 = jnp.uint32
I32 = jnp.int32
HIGHEST = lax.Precision.HIGHEST

D_MODEL = 1024
HEAD_DIM = 64
LANES = 128
SUBLANES = 8
GRID_W = 64
ROPE_THETA = 10000.0
EPS = 1e-6
NEG_INF = -1e30
WINDOW = 128
N_EXPERTS = 64
TOP_K = 8
N_GROUPS = 8
TOPK_GROUPS = 4
GROUP_SIZE = N_EXPERTS // N_GROUPS
ROUTED_SCALE = 2.5
LOG2E = 1.4426950408889634
Q_SCALE = HEAD_DIM ** -0.5 * LOG2E

ROW_TILE = 256
PROJ_SUB = 2
Q_TILE = 256
KV_TILE = 256
LAT_KV_TILE = 256
MOVE_TILE = 256
EXPERT_BLOCK = 1024
EXPERT_SLAB = 1024
SEG_ALIGN = 8
RUN_PIECES = (256, 128, 64, 32, 16, 8)
WAIT_PIECES = (2048, 1024, 512) + RUN_PIECES
STAGE_ROWS = 2560
STAGE_CHUNK = 512
HALF = D_MODEL // 2
VMEM_LIMIT = 48 * 1024 * 1024


def _cparams(sem):
    return pltpu.CompilerParams(dimension_semantics=sem, vmem_limit_bytes=VMEM_LIMIT)


def _rms(x, g):
    ms = jnp.mean(x * x, axis=-1, keepdims=True)
    return x * lax.rsqrt(ms + EPS) * g


def _pack_halves(h):
    lo = lax.bitcast_convert_type(h[:, :HALF].astype(BF16).astype(F32), U32) >> 16
    hi = lax.bitcast_convert_type(h[:, HALF:].astype(BF16).astype(F32), U32) & jnp.uint32(0xFFFF0000)
    return hi | lo


def _unpack_halves(u):
    lo = lax.bitcast_convert_type(u << 16, F32)
    hi = lax.bitcast_convert_type(u & jnp.uint32(0xFFFF0000), F32)
    return lo, hi


def _mod_kernel(c_ref, w_ref, b_ref, o_ref):
    c = c_ref[...]
    sc = c * jax.nn.sigmoid(c)
    o_ref[0] = jnp.dot(sc, w_ref[0], precision=HIGHEST, preferred_element_type=F32) + b_ref[0]


def _modulation(cond, ada_w, ada_b):
    depth, d, n = ada_w.shape
    nt = 1536
    return pl.pallas_call(
        _mod_kernel,
        out_shape=jax.ShapeDtypeStruct((depth, 8, n), F32),
        grid=(depth, n // nt),
        in_specs=[pl.BlockSpec((8, d), lambda l, j: (0, 0)),
                  pl.BlockSpec((1, d, nt), lambda l, j: (l, 0, j)),
                  pl.BlockSpec((1, 1, nt), lambda l, j: (l, 0, j))],
        out_specs=pl.BlockSpec((1, 8, nt), lambda l, j: (l, 0, j)),
        compiler_params=_cparams(("arbitrary", "arbitrary")),
        name="ada_mod",
    )(cond, ada_w, ada_b.reshape(depth, 1, n))


def _proj_kernel(*refs, blocks, tiles):
    per_tile = (2 if tiles else 1) + 4
    g_ref, w_ref, hg_ref, gm_ref, o_ref = refs[PROJ_SUB * per_tile:]
    hs, cos, sin = [], [], []
    for s in range(PROJ_SUB):
        sub = refs[s * per_tile:(s + 1) * per_tile]
        if tiles:
            xc_ref, x_ref, sh_ref, sc_ref, cos_ref, sin_ref = sub
            x = jnp.where((pl.program_id(0) * PROJ_SUB + s) % tiles == 0, xc_ref[...], x_ref[...])
        else:
            x_ref, sh_ref, sc_ref, cos_ref, sin_ref = sub
            x = x_ref[...]
        hs.append((_rms(x, g_ref[...]) * (1.0 + sc_ref[...]) + sh_ref[...]).astype(BF16))
        cos.append(cos_ref[...])
        sin.append(sin_ref[...])
    h = jnp.concatenate(hs, axis=0)
    cos = jnp.concatenate(cos, axis=0)
    sin = jnp.concatenate(sin, axis=0)
    y = jnp.dot(h, w_ref[...], preferred_element_type=F32)
    lane = lax.broadcasted_iota(I32, (h.shape[0], LANES), 1)
    first_half = (lane % HEAD_DIM) < (HEAD_DIM // 2)
    for jb, (norm, rope, scale) in enumerate(blocks):
        cols = slice(jb * LANES, (jb + 1) * LANES)
        yb = y[:, cols]
        if norm:
            ms = jnp.dot(yb * yb, gm_ref[...], precision=HIGHEST, preferred_element_type=F32)
            yb = yb * lax.rsqrt(ms + EPS) * hg_ref[:, cols]
        if rope:
            swapped = jnp.where(first_half, pltpu.roll(yb, LANES - HEAD_DIM // 2, 1),
                                pltpu.roll(yb, HEAD_DIM // 2, 1))
            yb = yb * cos + swapped * sin
        if scale != 1.0:
            yb = yb * scale
        o_ref[:, cols] = yb.astype(BF16)


def _pair_specs(pair, tiles):
    ctx_rows, lat_rows = pair
    return [pl.BlockSpec((ROW_TILE, ctx_rows.shape[1]), lambda i: (i // tiles, 0)),
            pl.BlockSpec((ROW_TILE, lat_rows.shape[1]),
                         lambda i: ((i // tiles) * (tiles - 1) + jnp.maximum(i % tiles - 1, 0), 0))]


def _project(x, modr, mod_base, row_of_tile, g, w, cos_t, sin_t, head_gain, blocks, tiles_per_batch):
    paired = isinstance(x, tuple)
    xs = list(x) if paired else [x]
    t, d = sum(a.shape[0] for a in xs), xs[0].shape[1]
    n = w.shape[1]
    assert (t // ROW_TILE) % PROJ_SUB == 0
    group_mean = jnp.asarray(np.kron(np.eye(LANES // HEAD_DIM), np.full((HEAD_DIM, HEAD_DIM), 1.0 / HEAD_DIM)), F32)

    def at_tile(spec, s):
        return pl.BlockSpec(spec.block_shape, lambda i: spec.index_map(i * PROJ_SUB + s))

    tile_specs = (_pair_specs(x, tiles_per_batch) if paired else [pl.BlockSpec((ROW_TILE, d), lambda i: (i, 0))])
    tile_specs += [pl.BlockSpec((None, 1, d), lambda i: (mod_base + row_of_tile(i) * 6 + 0, 0, 0)),
                   pl.BlockSpec((None, 1, d), lambda i: (mod_base + row_of_tile(i) * 6 + 1, 0, 0)),
                   pl.BlockSpec((ROW_TILE, LANES), lambda i: (i % tiles_per_batch, 0)),
                   pl.BlockSpec((ROW_TILE, LANES), lambda i: (i % tiles_per_batch, 0))]
    tile_args = xs + [modr, modr, cos_t, sin_t]
    in_specs, args = [], []
    for s in range(PROJ_SUB):
        in_specs += [at_tile(spec, s) for spec in tile_specs]
        args += tile_args
    in_specs += [pl.BlockSpec((1, d), lambda i: (0, 0)), pl.BlockSpec((d, n), lambda i: (0, 0)),
                 pl.BlockSpec((1, n), lambda i: (0, 0)), pl.BlockSpec((LANES, LANES), lambda i: (0, 0))]
    args += [g, w, head_gain, group_mean]
    return pl.pallas_call(
        functools.partial(_proj_kernel, blocks=blocks, tiles=tiles_per_batch if paired else None),
        out_shape=jax.ShapeDtypeStruct((t, n), BF16),
        grid=(t // (ROW_TILE * PROJ_SUB),),
        in_specs=in_specs,
        out_specs=pl.BlockSpec((ROW_TILE * PROJ_SUB, n), lambda i: (i, 0)),
        compiler_params=_cparams(("arbitrary",)),
        name="prenorm_proj",
    )(*args)


def _attn_kernel(*refs, mode, queries, seq, ctx_len, lambda_init, q_parts):
    q_refs, refs = refs[:q_parts], refs[q_parts:]
    if mode == "window":
        k_ref, v_ref, sink_ref, o_ref = refs
    elif mode == "diff":
        k_ref, v_ref, lam_ref, sg_ref, o_ref = refs
    else:
        k_ref, v_ref, o_ref = refs
    tq = q_parts * q_refs[0].shape[0]
    j = pl.program_id(1)
    qi = pl.program_id(2)

    q = q_refs[0][...] if q_parts == 1 else jnp.concatenate([r[...] for r in q_refs], axis=0)
    lane = lax.broadcasted_iota(I32, (tq, LANES), 1)
    low = lane < HEAD_DIM
    zero = jnp.zeros_like(q)
    q2 = jnp.concatenate([jnp.where(low, q, zero), jnp.where(low, zero, q)], axis=0)

    if mode == "window":
        s0 = sink_ref[pl.ds(2 * j, 1), :]
        s1 = sink_ref[pl.ds(2 * j + 1, 1), :]
        row = lax.broadcasted_iota(I32, (2 * tq, LANES), 0)
        sink = jnp.where(row < tq, s0, s1) * LOG2E
        m0 = sink
    else:
        m0 = jnp.full((2 * tq, LANES), NEG_INF, F32)
    state = (m0, jnp.zeros((2 * tq, LANES), F32), jnp.zeros((2 * tq, LANES), F32))

    def chunk(state, start, valid, size=KV_TILE):
        m_prev, l_prev, acc_prev = state
        k = k_ref[pl.ds(start, size), :]
        v = v_ref[pl.ds(start, size), :]
        s = lax.dot_general(q2, k, (((1,), (1,)), ((), ())), preferred_element_type=F32)
        if valid is not None:
            s = jnp.where(valid, s, NEG_INF)
        m_new = jnp.maximum(m_prev, jnp.max(s, axis=1, keepdims=True))
        alpha = jnp.exp2(m_prev - m_new)
        p = jnp.exp2(s - jnp.concatenate([m_new] * (size // LANES), axis=1))
        part = p[:, :LANES]
        for c in range(1, size // LANES):
            part = part + p[:, c * LANES:(c + 1) * LANES]
        return (m_new, alpha * l_prev + part,
                alpha * acc_prev + jnp.dot(p.astype(BF16), v, preferred_element_type=F32))

    state = chunk(state, 0, None)
    if queries == "latent" and mode == "window":
        n_win = (tq + 2 * WINDOW) // KV_TILE
        q0 = qi * tq
        kstart = jnp.clip(q0 - WINDOW, 0, seq - n_win * KV_TILE)
        r = lax.broadcasted_iota(I32, (2 * tq, KV_TILE), 0)
        qpos = q0 + jnp.where(r >= tq, r - tq, r)
        col = lax.broadcasted_iota(I32, (2 * tq, KV_TILE), 1)
        for w in range(n_win):
            kpos = kstart + w * KV_TILE + col
            state = chunk(state, pl.multiple_of(ctx_len + kstart + w * KV_TILE, WINDOW),
                          jnp.abs(qpos - kpos) <= WINDOW)
    elif queries == "latent":
        for c in range(seq // LAT_KV_TILE):
            state = chunk(state, ctx_len + c * LAT_KV_TILE, None, LAT_KV_TILE)

    m_fin, l_part, acc = state
    l = jnp.sum(l_part, axis=1, keepdims=True)
    if mode == "window":
        l = l + jnp.exp2(sink - m_fin)[:, :1]
    o2 = acc / l
    if mode == "diff":
        lp = lam_ref[...]
        lam = (jnp.exp(jnp.sum(lp[0:1] * lp[1:2], axis=1, keepdims=True))
               - jnp.exp(jnp.sum(lp[2:3] * lp[3:4], axis=1, keepdims=True)) + lambda_init)
        o = o2[:tq] - lam * o2[tq:]
        o = _rms(o, sg_ref[...]) * (1.0 - lambda_init)
    else:
        o = jnp.where(low, o2[:tq], o2[tq:])
    o_ref[...] = o.astype(BF16)


def _attention(p, extra, *, mode, queries, batch, seq, ctx_len, n_qblocks, q_col0, k_col, v_col,
               lambda_init=0.0, q_parts=1):
    n_p = ctx_len + seq
    tiles = n_p // Q_TILE
    tq = q_parts * Q_TILE
    n_rows = seq if queries == "latent" else ctx_len
    qt = n_rows // tq
    q_off = ctx_len // Q_TILE if queries == "latent" else 0
    in_specs = [pl.BlockSpec((Q_TILE, LANES),
                             lambda b, j, qi, part=part: (b * tiles + q_off + qi * q_parts + part, q_col0 + j))
                for part in range(q_parts)]
    in_specs += [pl.BlockSpec((n_p, LANES), lambda b, j, qi: (b, k_col(j))),
                 pl.BlockSpec((n_p, LANES), lambda b, j, qi: (b, v_col(j)))]
    args = [p] * (q_parts + 2)
    for e in extra:
        in_specs.append(pl.BlockSpec(e.shape, lambda b, j, qi: (0, 0)))
        args.append(e)
    kern = functools.partial(_attn_kernel, mode=mode, queries=queries, seq=seq, ctx_len=ctx_len,
                             lambda_init=lambda_init, q_parts=q_parts)
    return pl.pallas_call(
        kern,
        out_shape=jax.ShapeDtypeStruct((batch * n_rows, n_qblocks * LANES), BF16),
        grid=(batch, n_qblocks, qt),
        in_specs=in_specs,
        out_specs=pl.BlockSpec((tq, LANES), lambda b, j, qi: (b * qt + qi, j)),
        compiler_params=_cparams(("arbitrary", "arbitrary", "arbitrary")),
        name="attn_" + mode + "_" + queries,
    )(*args)


def _out_kernel(*refs, n_o, tiles):
    n_in = 2 * n_o if tiles else n_o
    o_refs = refs[:n_in]
    w_refs = refs[n_in:n_in + n_o]
    rest = refs[n_in + n_o:]
    x_refs, rest = (rest[:2], rest[2:]) if tiles else (rest[:1], rest[1:])
    g1_ref, gate_ref, g2_ref, sh_ref, sc_ref, rw_ref, xn_ref, hp_ref, lg_ref = rest
    is_ctx = (pl.program_id(0) % tiles == 0) if tiles else None

    def residual_rows(rows):
        if not tiles:
            return x_refs[0][rows, :]
        return jnp.where(is_ctx, x_refs[0][rows, :], x_refs[1][rows, :])

    def mixer_rows(a, rows):
        if not tiles:
            return o_refs[a][rows, :]
        return jnp.where(is_ctx, o_refs[2 * a][rows, :], o_refs[2 * a + 1][rows, :])

    slab = LANES
    for r0 in range(0, xn_ref.shape[0], slab):
        rows = slice(r0, r0 + slab)
        y = jnp.dot(mixer_rows(0, rows), w_refs[0][...], preferred_element_type=F32)
        for a in range(1, n_o):
            y = y + jnp.dot(mixer_rows(a, rows), w_refs[a][...], preferred_element_type=F32)
        xn = residual_rows(rows) + gate_ref[...] * _rms(y, g1_ref[...])
        xn_ref[rows, :] = xn
        h = _rms(xn, g2_ref[...]) * (1.0 + sc_ref[...]) + sh_ref[...]
        hp_ref[rows, :] = _pack_halves(h)
        lg_ref[:, rows] = lax.dot_general(rw_ref[...], h, (((1,), (1,)), ((), ())), precision=HIGHEST,
                                          preferred_element_type=F32)


def _out_project(os_, ws, x, x_tile, modr, mod_base, row_of_tile, g1, g2, rw_t, n_tiles, tiles=None):
    d = x[0].shape[1] if tiles else x.shape[1]
    n_o = len(os_)
    tm = n_tiles * ROW_TILE
    mspec = lambda which: pl.BlockSpec((None, 1, d), lambda i: (mod_base + row_of_tile(i) * 6 + which, 0, 0))
    if tiles:
        o_specs, o_args = [], []
        for pair in os_:
            o_specs += _pair_specs(pair, tiles)
            o_args += list(pair)
        x_specs, x_args = _pair_specs(x, tiles), list(x)
    else:
        o_specs = [pl.BlockSpec((ROW_TILE, o.shape[1]), lambda i: (i, 0)) for o in os_]
        o_args = list(os_)
        x_specs, x_args = [pl.BlockSpec((ROW_TILE, d), lambda i: (x_tile(i), 0))], [x]
    in_specs = (o_specs
                + [pl.BlockSpec(w.shape, lambda i: (0, 0)) for w in ws]
                + x_specs
                + [pl.BlockSpec((1, d), lambda i: (0, 0)), mspec(2),
                   pl.BlockSpec((1, d), lambda i: (0, 0)), mspec(3), mspec(4),
                   pl.BlockSpec(rw_t.shape, lambda i: (0, 0))])
    return pl.pallas_call(
        functools.partial(_out_kernel, n_o=n_o, tiles=tiles),
        out_shape=(jax.ShapeDtypeStruct((tm, d), F32), jax.ShapeDtypeStruct((tm, HALF), U32),
                   jax.ShapeDtypeStruct((N_EXPERTS, tm), F32)),
        grid=(n_tiles,),
        in_specs=in_specs,
        out_specs=(pl.BlockSpec((ROW_TILE, d), lambda i: (i, 0)),
                   pl.BlockSpec((ROW_TILE, HALF), lambda i: (i, 0)),
                   pl.BlockSpec((N_EXPERTS, ROW_TILE), lambda i: (0, i))),
        compiler_params=_cparams(("arbitrary",)),
        name="out_proj",
    )(*o_args, *ws, *x_args, g1, modr, g2, modr, modr, rw_t)


def _route_kernel(lg_ref, bias_ref, tri_ref, ltri_ref, ls_ref, w_ref, segn_ref, segc_ref, cnt_ref, carry_ref):
    i = pl.program_id(0)

    @pl.when(i == 0)
    def _():
        carry_ref[...] = jnp.zeros(carry_ref.shape, F32)

    tr = lg_ref.shape[1]
    score = jax.nn.sigmoid(lg_ref[...])
    sel = score + bias_ref[...][:, :1]
    sel_j = [sel[j * GROUP_SIZE:(j + 1) * GROUP_SIZE] for j in range(GROUP_SIZE)]
    sc_j = [score[j * GROUP_SIZE:(j + 1) * GROUP_SIZE] for j in range(GROUP_SIZE)]
    gi = lax.broadcasted_iota(I32, (N_GROUPS, tr), 0)

    m1 = sel_j[0]
    m2 = jnp.full_like(m1, -jnp.inf)
    for j in range(1, GROUP_SIZE):
        m2 = jnp.maximum(m2, jnp.minimum(m1, sel_j[j]))
        m1 = jnp.maximum(m1, sel_j[j])
    gs = m1 + m2

    grank = jnp.zeros((N_GROUPS, tr), I32)
    for gp in range(N_GROUPS):
        rowv = gs[gp:gp + 1, :]
        grank = grank + jnp.where(gi > gp, jnp.where(rowv >= gs, 1, 0), jnp.where(rowv > gs, 1, 0))
    gmask = grank < TOPK_GROUPS
    val_j = [jnp.where(gmask, sel_j[j], NEG_INF) for j in range(GROUP_SIZE)]

    e_j = [(gi * GROUP_SIZE + j).astype(F32) for j in range(GROUP_SIZE)]
    work_j = list(val_j)
    picked_j = [jnp.zeros((N_GROUPS, tr), F32) for _ in range(GROUP_SIZE)]
    idx_k = []
    for k in range(TOP_K):
        best = work_j[0]
        for j in range(1, GROUP_SIZE):
            best = jnp.maximum(best, work_j[j])
        best = jnp.max(best, axis=0, keepdims=True)
        first = jnp.where(work_j[0] == best, e_j[0], float(N_EXPERTS))
        for j in range(1, GROUP_SIZE):
            first = jnp.minimum(first, jnp.where(work_j[j] == best, e_j[j], float(N_EXPERTS)))
        first = jnp.min(first, axis=0, keepdims=True)
        idx_k.append(first)
        for j in range(GROUP_SIZE):
            hit = e_j[j] == first
            work_j[j] = jnp.where(hit, -jnp.inf, work_j[j])
            picked_j[j] = jnp.where(hit, 1.0, picked_j[j])

    chosen = jnp.concatenate(picked_j, axis=0)
    n_run = jnp.floor((jnp.sum(chosen, axis=1, keepdims=True) + (SEG_ALIGN - 1.0)) * (1.0 / SEG_ALIGN)) * SEG_ALIGN
    n_run = jnp.broadcast_to(n_run, (N_EXPERTS, LANES))
    run_start = jnp.dot(ltri_ref[...], n_run.astype(BF16), preferred_element_type=F32)
    local = jnp.dot(chosen.astype(BF16), tri_ref[...], preferred_element_type=F32) + run_start[:, :1]
    segn_ref[0] = n_run.astype(I32)
    segc_ref[0] = carry_ref[...].astype(I32)
    carry_ref[...] = carry_ref[...] + n_run
    cnt_ref[...] = carry_ref[...].astype(I32)

    w_rows, ls_rows = [], []
    for k in range(TOP_K):
        w_acc = jnp.zeros((N_GROUPS, tr), F32)
        p_acc = jnp.zeros((N_GROUPS, tr), F32)
        for j in range(GROUP_SIZE):
            hit = e_j[j] == idx_k[k]
            w_acc = w_acc + jnp.where(hit, sc_j[j], 0.0)
            p_acc = p_acc + jnp.where(hit, local[j * GROUP_SIZE:(j + 1) * GROUP_SIZE], 0.0)
        w_rows.append(jnp.sum(w_acc, axis=0, keepdims=True))
        ls_rows.append(jnp.sum(p_acc, axis=0, keepdims=True))
    w_all = jnp.concatenate(w_rows, axis=0)
    w_ref[...] = w_all / jnp.sum(w_all, axis=0, keepdims=True) * ROUTED_SCALE
    ls_ref[...] = jnp.concatenate(ls_rows, axis=0).astype(I32)


def _route(lg_t, bias_rep):
    tm = lg_t.shape[1]
    n_tiles = tm // MOVE_TILE
    tri = jnp.asarray(np.triu(np.ones((MOVE_TILE, MOVE_TILE), np.float32), 1), BF16)
    ltri = jnp.asarray(np.tril(np.ones((N_EXPERTS, N_EXPERTS), np.float32), -1), BF16)
    tok = lambda i: (0, i)
    per_tile = pl.BlockSpec((1, N_EXPERTS, LANES), lambda i: (i, 0, 0))
    return pl.pallas_call(
        _route_kernel,
        out_shape=(jax.ShapeDtypeStruct((TOP_K, tm), I32), jax.ShapeDtypeStruct((TOP_K, tm), F32),
                   jax.ShapeDtypeStruct((n_tiles, N_EXPERTS, LANES), I32),
                   jax.ShapeDtypeStruct((n_tiles, N_EXPERTS, LANES), I32),
                   jax.ShapeDtypeStruct((N_EXPERTS, LANES), I32)),
        grid=(n_tiles,),
        in_specs=[pl.BlockSpec((N_EXPERTS, MOVE_TILE), tok),
                  pl.BlockSpec((N_EXPERTS, LANES), lambda i: (0, 0)),
                  pl.BlockSpec((MOVE_TILE, MOVE_TILE), lambda i: (0, 0)),
                  pl.BlockSpec((N_EXPERTS, N_EXPERTS), lambda i: (0, 0))],
        out_specs=(pl.BlockSpec((TOP_K, MOVE_TILE), tok), pl.BlockSpec((TOP_K, MOVE_TILE), tok),
                   per_tile, per_tile, pl.BlockSpec((N_EXPERTS, LANES), lambda i: (0, 0))),
        scratch_shapes=[pltpu.VMEM((N_EXPERTS, LANES), F32)],
        compiler_params=_cparams(("arbitrary",)),
        name="route",
    )(lg_t, bias_rep, tri, ltri)


def _start_pieces(n, src, dst, make, pieces):
    for size in pieces:
        above = n & (-2 * size)

        @pl.when((n & size) != 0)
        def _():
            make(pl.multiple_of(src + above, SEG_ALIGN), pl.multiple_of(dst + above, SEG_ALIGN), size).start()


def _start_run_copies(i, segn_ref, segd_ref, make, straight_line=False):
    split = RUN_PIECES.index(64)

    def per_run(r, src):
        n = segn_ref[i * N_EXPERTS + r]
        dst = segd_ref[i * N_EXPERTS + r]

        @pl.when(n >= RUN_PIECES[split - 1])
        def _():
            _start_pieces(n, src, dst, make, RUN_PIECES[:split])

        _start_pieces(n, src, dst, make, RUN_PIECES[split:])
        return src + n

    if not straight_line:
        return lax.fori_loop(0, N_EXPERTS, per_run, 0, unroll=2)
    src = 0
    for r in range(N_EXPERTS):
        src = per_run(r, src)
    return src


def _wait_rows(total, make):
    for size in WAIT_PIECES:
        @pl.when((total & size) != 0)
        def _():
            make(0, 0, size).wait()


def _fill_copies(zstart_ref, zlen_ref, make):
    pieces = tuple(p for p in WAIT_PIECES if p < EXPERT_BLOCK)

    def per_expert(r, carry):
        n = zlen_ref[r]
        _start_pieces(n, 0, zstart_ref[r], make, pieces)
        _wait_rows(n, make)
        return carry

    lax.fori_loop(0, N_EXPERTS, per_expert, 0)


def _dispatch_kernel(segn_ref, segd_ref, zstart_ref, zlen_ref, ls_ref, hp_ref, xs_hbm, stage_ref, zero_ref,
                     rows_ref, sem, zsem):
    i = pl.program_id(0)
    buf = i % 2
    lo, hi = _unpack_halves(hp_ref[...])
    lo = lo.astype(BF16)
    hi = hi.astype(BF16)
    ls = ls_ref[...]
    pack_rows = 16
    ls16 = [jnp.tile(jnp.broadcast_to(ls[k:k + 1, :], (pack_rows, MOVE_TILE)).astype(jnp.int16),
                     (STAGE_CHUNK // pack_rows, 1)) for k in range(TOP_K)]
    slot0 = lax.broadcasted_iota(I32, (STAGE_CHUNK, MOVE_TILE), 0).astype(jnp.int16)
    one = jnp.ones((STAGE_CHUNK, MOVE_TILE), BF16)
    for c in range(STAGE_ROWS // STAGE_CHUNK):
        slot = slot0 + jnp.int16(c * STAGE_CHUNK)
        onehot = jnp.zeros((STAGE_CHUNK, MOVE_TILE), BF16)
        for k in range(TOP_K):
            onehot = jnp.where(ls16[k] == slot, one, onehot)
        a = lax.bitcast_convert_type(jnp.dot(onehot, lo, preferred_element_type=F32), U32) >> 16
        b = lax.bitcast_convert_type(jnp.dot(onehot, hi, preferred_element_type=F32), U32)
        stage_ref[buf, c * STAGE_CHUNK:(c + 1) * STAGE_CHUNK, :] = b | a

    def to_slots(which):
        def make(src, dst, size):
            return pltpu.make_async_copy(stage_ref.at[which, pl.ds(src, size), :],
                                         xs_hbm.at[pl.ds(dst, size), :], sem.at[which])
        return make

    rows_ref[buf] = _start_run_copies(i, segn_ref, segd_ref, to_slots(buf))

    @pl.when(i > 0)
    def _():
        _wait_rows(rows_ref[1 - buf], to_slots(1 - buf))

    @pl.when(i == pl.num_programs(0) - 1)
    def _():
        _wait_rows(rows_ref[buf], to_slots(buf))

    @pl.when(i == 0)
    def _():
        zero_ref[...] = jnp.zeros(zero_ref.shape, U32)

        def zeros_to_slots(src, dst, size):
            return pltpu.make_async_copy(zero_ref.at[pl.ds(0, size), :], xs_hbm.at[pl.ds(dst, size), :], zsem)

        _fill_copies(zstart_ref, zlen_ref, zeros_to_slots)


def _dispatch(segn, segd, zstart, zlen, ls, hp, n_slots):
    tm = hp.shape[0]
    tile = lambda i, *_: (i, 0)
    return pl.pallas_call(
        _dispatch_kernel,
        out_shape=jax.ShapeDtypeStruct((n_slots, HALF), U32),
        grid_spec=pltpu.PrefetchScalarGridSpec(
            num_scalar_prefetch=4,
            grid=(tm // MOVE_TILE,),
            in_specs=[pl.BlockSpec((TOP_K, MOVE_TILE), lambda i, *_: (0, i)),
                      pl.BlockSpec((MOVE_TILE, HALF), tile)],
            out_specs=pl.BlockSpec(memory_space=pl.ANY),
            scratch_shapes=[pltpu.VMEM((2, STAGE_ROWS, HALF), U32), pltpu.VMEM((EXPERT_BLOCK // 2, HALF), U32),
                            pltpu.SMEM((2,), I32), pltpu.SemaphoreType.DMA((2,)), pltpu.SemaphoreType.DMA]),
        compiler_params=_cparams(("arbitrary",)),
        name="dispatch",
    )(segn, segd, zstart, zlen, ls, hp)


def _expert_kernel(be_ref, nu_ref, nxt_ref, xs_ref, w1_hbm, w3_hbm, w2_hbm, ys_ref, w1f_ref, w3f_ref, w2f_ref,
                   w1b_ref, w3b_ref, w2b_ref, run_ref, sem, *, layer):
    i = pl.program_id(0)

    def fetch(e, slot):
        return [pltpu.make_async_copy(src.at[layer, e], dst.at[slot], sem.at[slot])
                for src, dst in ((w1_hbm, w1f_ref), (w3_hbm, w3f_ref), (w2_hbm, w2f_ref))]

    @pl.when(i == 0)
    def _():
        run_ref[0] = 0
        for cp in fetch(be_ref[0], 0):
            cp.start()

    first = jnp.logical_or(i == 0, be_ref[i] != be_ref[jnp.maximum(i - 1, 0)])

    @pl.when(jnp.logical_and(first, i < nu_ref[0]))
    def _():
        slot = run_ref[0] % 2
        for cp in fetch(be_ref[i], slot):
            cp.wait()
        w1b_ref[...] = w1f_ref[slot].astype(BF16)
        w3b_ref[...] = w3f_ref[slot].astype(BF16)
        w2b_ref[...] = w2f_ref[slot].astype(BF16)
        nxt = nxt_ref[i]

        @pl.when(nxt >= 0)
        def _():
            for cp in fetch(nxt, 1 - slot):
                cp.start()

        run_ref[0] = run_ref[0] + 1

    @pl.when(i < nu_ref[0])
    def _():
        for rows in (slice(r0, r0 + EXPERT_SLAB) for r0 in range(0, EXPERT_BLOCK, EXPERT_SLAB)):
            lo, hi = _unpack_halves(xs_ref[rows, :])
            lo = lo.astype(BF16)
            hi = hi.astype(BF16)
            h1 = (jnp.dot(lo, w1b_ref[:HALF, :], preferred_element_type=F32)
                  + jnp.dot(hi, w1b_ref[HALF:, :], preferred_element_type=F32))
            h3 = (jnp.dot(lo, w3b_ref[:HALF, :], preferred_element_type=F32)
                  + jnp.dot(hi, w3b_ref[HALF:, :], preferred_element_type=F32))
            g = (h1 * jax.nn.sigmoid(h1) * h3).astype(BF16)
            y = jnp.dot(g, w2b_ref[...], preferred_element_type=F32)
            ys_ref[rows, :] = _pack_halves(y)


def _experts(block_e, n_used, next_e, xs, w1, w3, w2, layer):
    n_slots = xs.shape[0]
    n_blocks = n_slots // EXPERT_BLOCK
    d, eh = w1.shape[2], w1.shape[3]
    rows = lambda i, be, nu, nx: (jnp.minimum(i, nu[0] - 1), 0)
    anyspace = pl.BlockSpec(memory_space=pl.ANY)
    return pl.pallas_call(
        functools.partial(_expert_kernel, layer=layer),
        out_shape=jax.ShapeDtypeStruct((n_slots, HALF), U32),
        grid_spec=pltpu.PrefetchScalarGridSpec(
            num_scalar_prefetch=3,
            grid=(n_blocks,),
            in_specs=[pl.BlockSpec((EXPERT_BLOCK, HALF), rows), anyspace, anyspace, anyspace],
            out_specs=pl.BlockSpec((EXPERT_BLOCK, HALF), rows),
            scratch_shapes=[pltpu.VMEM((2, d, eh), F32), pltpu.VMEM((2, d, eh), F32), pltpu.VMEM((2, eh, d), F32),
                            pltpu.VMEM((d, eh), BF16), pltpu.VMEM((d, eh), BF16), pltpu.VMEM((eh, d), BF16),
                            pltpu.SMEM((1,), I32), pltpu.SemaphoreType.DMA((2,))]),
        compiler_params=_cparams(("arbitrary",)),
        name="experts",
    )(block_e, n_used, next_e, xs, w1, w3, w2)


def _combine_kernel(segn_ref, segd_ref, ys_hbm, lst_ref, wt_ref, hp_ref, x_ref, s1_ref, s3_ref, s2_ref, g_ref,
                    gate_ref, o_ref, stage_ref, rows_ref, sem):
    i = pl.program_id(0)
    buf = i % 2

    def from_slots(which):
        def make(src, dst, size):
            return pltpu.make_async_copy(ys_hbm.at[pl.ds(dst, size), :],
                                         stage_ref.at[which, pl.ds(src, size), :], sem.at[which])
        return make

    @pl.when(i == 0)
    def _():
        stage_ref[...] = jnp.zeros(stage_ref.shape, U32)
        rows_ref[0] = _start_run_copies(0, segn_ref, segd_ref, from_slots(0))

    last = pl.num_programs(0) - 1
    rows_ref[1 - buf] = _start_run_copies(jnp.minimum(i + 1, last), segn_ref, segd_ref, from_slots(1 - buf),
                                          straight_line=True)

    lo, hi = _unpack_halves(hp_ref[...])
    lo = lo.astype(BF16)
    hi = hi.astype(BF16)
    h1 = (jnp.dot(lo, s1_ref[:HALF, :], preferred_element_type=F32)
          + jnp.dot(hi, s1_ref[HALF:, :], preferred_element_type=F32))
    h3 = (jnp.dot(lo, s3_ref[:HALF, :], preferred_element_type=F32)
          + jnp.dot(hi, s3_ref[HALF:, :], preferred_element_type=F32))
    y = jnp.dot((h1 * jax.nn.sigmoid(h1) * h3).astype(BF16), s2_ref[...], preferred_element_type=F32)

    _wait_rows(rows_ref[buf], from_slots(buf))

    @pl.when(i == last)
    def _():
        _wait_rows(rows_ref[1 - buf], from_slots(1 - buf))

    lst = lst_ref[...]
    wt = wt_ref[...]
    lst16 = [jnp.broadcast_to(lst[:, k:k + 1], (MOVE_TILE, STAGE_CHUNK)).astype(jnp.int16) for k in range(TOP_K)]
    wt16 = [jnp.broadcast_to(wt[:, k:k + 1], (MOVE_TILE, STAGE_CHUNK)).astype(BF16) for k in range(TOP_K)]
    slot0 = lax.broadcasted_iota(I32, (MOVE_TILE, STAGE_CHUNK), 1).astype(jnp.int16)
    r_lo = jnp.zeros((MOVE_TILE, HALF), F32)
    r_hi = jnp.zeros((MOVE_TILE, HALF), F32)
    for c in range(STAGE_ROWS // STAGE_CHUNK):
        slot = slot0 + jnp.int16(c * STAGE_CHUNK)
        wsel = jnp.zeros((MOVE_TILE, STAGE_CHUNK), BF16)
        for k in range(TOP_K):
            wsel = jnp.where(lst16[k] == slot, wt16[k], wsel)
        a, b = _unpack_halves(stage_ref[buf, c * STAGE_CHUNK:(c + 1) * STAGE_CHUNK, :])
        r_lo = r_lo + jnp.dot(wsel, a.astype(BF16), preferred_element_type=F32)
        r_hi = r_hi + jnp.dot(wsel, b.astype(BF16), preferred_element_type=F32)
    y = y + jnp.concatenate([r_lo, r_hi], axis=1)
    o_ref[...] = x_ref[...] + gate_ref[...] * _rms(y, g_ref[...])


def _combine(segn, segd, ys, lst, wt, hp, x, s1, s3, s2, g, modr, mod_base, row_of_tile):
    tm, d = x.shape
    full = lambda a: pl.BlockSpec(a.shape, lambda i, *_: (0, 0))
    tile = lambda i, *_: (i, 0)
    return pl.pallas_call(
        _combine_kernel,
        out_shape=jax.ShapeDtypeStruct((tm, d), F32),
        grid_spec=pltpu.PrefetchScalarGridSpec(
            num_scalar_prefetch=2,
            grid=(tm // MOVE_TILE,),
            in_specs=[pl.BlockSpec(memory_space=pl.ANY),
                      pl.BlockSpec((MOVE_TILE, TOP_K), tile), pl.BlockSpec((MOVE_TILE, TOP_K), tile),
                      pl.BlockSpec((MOVE_TILE, HALF), tile), pl.BlockSpec((MOVE_TILE, d), tile),
                      full(s1), full(s3), full(s2), full(g),
                      pl.BlockSpec((None, 1, d), lambda i, *_: (mod_base + row_of_tile(i) * 6 + 5, 0, 0))],
            out_specs=pl.BlockSpec((MOVE_TILE, d), tile),
            scratch_shapes=[pltpu.VMEM((2, STAGE_ROWS, HALF), U32), pltpu.SMEM((2,), I32),
                            pltpu.SemaphoreType.DMA((2,))]),
        compiler_params=_cparams(("arbitrary",)),
        name="combine",
    )(segn, segd, ys, lst, wt, hp, x, s1, s3, s2, g, modr)


def _moe(xn, hp, lg_t, bias, expert_weights, s1, s3, s2, g, modr, mod_base, row_of_tile):
    tm = xn.shape[0]
    n_tiles = tm // MOVE_TILE
    perm = np.array([(r % GROUP_SIZE) * GROUP_SIZE + r // GROUP_SIZE for r in range(N_EXPERTS)])
    bias_rep = jnp.broadcast_to(bias.astype(F32)[perm][:, None], (N_EXPERTS, LANES))
    ls, w, segn, segc, cnt = _route(lg_t, bias_rep)

    total = cnt[:, 0]
    padded = (total + EXPERT_BLOCK - 1) // EXPERT_BLOCK * EXPERT_BLOCK
    pad_end = jnp.cumsum(padded)
    pad_start = pad_end - padded
    n_slots = tm * TOP_K + (SEG_ALIGN - 1) * N_EXPERTS * n_tiles + N_EXPERTS * EXPERT_BLOCK
    n_blocks = -(-n_slots // EXPERT_BLOCK)
    starts = jnp.arange(n_blocks, dtype=I32) * EXPERT_BLOCK
    region = jnp.minimum(jnp.sum((pad_end[None, :] <= starts[:, None]).astype(I32), axis=1), N_EXPERTS - 1)
    block_e = jnp.asarray(perm, I32)[region]
    n_used = (pad_end[-1:] // EXPERT_BLOCK).astype(I32)
    after = pad_end[region] // EXPERT_BLOCK
    next_e = jnp.where(after < n_used[0], block_e[jnp.minimum(after, n_blocks - 1)], -1).astype(I32)
    segn = segn[:, :, 0].reshape(-1)
    segd = (segc[:, :, 0] + pad_start[None, :]).reshape(-1).astype(I32)

    xs = _dispatch(segn, segd, (pad_start + total).astype(I32), (padded - total).astype(I32), ls, hp,
                   n_blocks * EXPERT_BLOCK)
    ys = _experts(block_e, n_used, next_e, xs, *expert_weights)
    return _combine(segn, segd, ys, ls.T, w.T, hp, xn, s1.astype(BF16), s3.astype(BF16), s2.astype(BF16), g, modr,
                    mod_base, row_of_tile)


_DEINTERLEAVE = np.concatenate([np.arange(0, HEAD_DIM, 2), np.arange(1, HEAD_DIM, 2)])


def _rope_tables(seq, ctx_len):
    t = np.arange(seq)
    n_pair = HEAD_DIM // 4
    inv = jnp.asarray(ROPE_THETA, F32) ** (-jnp.arange(n_pair, dtype=F32) / n_pair)
    r = jnp.asarray(t // GRID_W, F32)
    c = jnp.asarray(t % GRID_W, F32)
    ang = jnp.concatenate([r[:, None] * inv, c[:, None] * inv], axis=-1)
    cos, sin = jnp.cos(ang), jnp.sin(ang)
    cos_t = jnp.tile(jnp.concatenate([cos, cos], axis=-1), (1, LANES // HEAD_DIM))
    sin_t = jnp.tile(jnp.concatenate([-sin, sin], axis=-1), (1, LANES // HEAD_DIM))
    cos_t = jnp.concatenate([jnp.ones((ctx_len, LANES), F32), cos_t], axis=0)
    sin_t = jnp.concatenate([jnp.zeros((ctx_len, LANES), F32), sin_t], axis=0)
    return cos_t, sin_t


def _ab_layout():
    cols, blocks, gain_kind = [], [], []
    for base_q, base_k, base_v, normed in ((0, 512, 640, True), (768, 1280, 1408, False)):
        for jb in range(4):
            cols += [base_q + h * HEAD_DIM + _DEINTERLEAVE for h in (2 * jb, 2 * jb + 1)]
            blocks.append((normed, True, Q_SCALE))
            gain_kind.append("q" if normed else None)
        for kvh in range(2):
            cols += [base_k + kvh * HEAD_DIM + _DEINTERLEAVE] * 2
            blocks.append((normed, True, 1.0))
            gain_kind.append("k" if normed else None)
        for kvh in range(2):
            cols += [base_v + kvh * HEAD_DIM + np.arange(HEAD_DIM)] * 2
            blocks.append((False, False, 1.0))
            gain_kind.append(None)
    return np.concatenate(cols), tuple(blocks), gain_kind


def _c_layout():
    cols, blocks = [], []
    for base, rope, scale in ((0, True, Q_SCALE), (1024, True, 1.0)):
        for h in range(8):
            cols += [base + (2 * h + m) * HEAD_DIM + _DEINTERLEAVE for m in range(2)]
            blocks.append((False, rope, scale))
    for h in range(8):
        cols.append(2048 + h * LANES + np.arange(LANES))
        blocks.append((False, False, 1.0))
    return np.concatenate(cols), tuple(blocks)


def kernel(x, c, ctx, c_ctx, ada_w, ada_b, norm_g, ab_w_in, ab_w_out, a_q_norm, a_k_norm, b_sink,
           c_w_in, c_w_out, c_lambda, c_subln_g, router_w, router_bias, exp_w1, exp_w3, exp_w2,
           sh_w1, sh_w3, sh_w2):
    batch, seq, d = x.shape
    ctx_len = ctx.shape[1]
    depth = ada_w.shape[0]
    assert d == D_MODEL and ctx_len == ROW_TILE and depth == 2
    assert seq % (4 * Q_TILE) == 0 and batch <= 4
    n_p = ctx_len + seq
    tiles = n_p // ROW_TILE
    lat_tiles = seq // ROW_TILE
    t_all = batch * n_p

    cond = jnp.zeros((8, d), F32).at[:batch].set(c).at[4].set(c_ctx)
    modr = _modulation(cond, ada_w, ada_b).reshape(depth * 8 * 6, 1, d)
    row_all = lambda i: jnp.where(i % tiles == 0, 4, i // tiles)
    cos_t, sin_t = _rope_tables(seq, ctx_len)
    rperm = np.array([(r % GROUP_SIZE) * GROUP_SIZE + r // GROUP_SIZE for r in range(N_EXPERTS)])
    xt = (ctx.reshape(batch * ctx_len, d), x.reshape(batch * seq, d))

    cols, blocks, gain_kind = _ab_layout()
    w0 = ab_w_in[0][:, cols].astype(BF16)
    gq = jnp.tile(a_q_norm[0][_DEINTERLEAVE], 2)
    gk = jnp.tile(a_k_norm[0][_DEINTERLEAVE], 2)
    ones = jnp.ones((LANES,), F32)
    head_gain = jnp.concatenate([{"q": gq, "k": gk, None: ones}[kind] for kind in gain_kind])[None, :]
    p0 = _project(xt, modr, 0, row_all, norm_g[0, 0][None, :], w0, cos_t, sin_t, head_gain, blocks, tiles)
    common = dict(batch=batch, seq=seq, ctx_len=ctx_len, n_qblocks=4)
    dense = dict(mode="dense", q_col0=0, k_col=lambda j: 4 + j // 2, v_col=lambda j: 6 + j // 2, **common)
    window = dict(mode="window", q_col0=8, k_col=lambda j: 12 + j // 2, v_col=lambda j: 14 + j // 2, **common)
    sink = jnp.broadcast_to(b_sink[0].astype(F32)[:, None], (8, LANES))
    oa = (_attention(p0, [], queries="context", **dense), _attention(p0, [], queries="latent", q_parts=4, **dense))
    ob = (_attention(p0, [sink], queries="context", **window),
          _attention(p0, [sink], queries="latent", **window))
    w_out = ab_w_out[0].astype(BF16)
    x1, hp, lg = _out_project([oa, ob], [w_out[:512], w_out[512:]], xt, lambda i: i, modr, 0, row_all,
                              norm_g[0, 1][None, :], norm_g[0, 2][None, :], router_w[0].T[rperm],
                              t_all // ROW_TILE, tiles=tiles)
    x2 = _moe(x1, hp, lg, router_bias[0], (exp_w1, exp_w3, exp_w2, 0), sh_w1[0], sh_w3[0], sh_w2[0],
              norm_g[0, 3][None, :], modr, 0, lambda i: row_all(i // (ROW_TILE // MOVE_TILE)))

    base1 = 8 * 6
    lambda_init = 0.8 - 0.6 * math.exp(-0.3 * 1)
    cols1, blocks1 = _c_layout()
    w1p = c_w_in[0][:, cols1].astype(BF16)
    p1 = _project(x2, modr, base1, row_all, norm_g[1, 0][None, :], w1p, cos_t, sin_t,
                  jnp.ones((1, w1p.shape[1]), F32), blocks1, tiles)
    lam = jnp.zeros((8, LANES), F32).at[:4, :HEAD_DIM].set(c_lambda[0].astype(F32))
    oc = _attention(p1, [lam, c_subln_g[0][None, :]], mode="diff", queries="latent", batch=batch, seq=seq,
                    ctx_len=ctx_len, n_qblocks=8, q_col0=0, k_col=lambda j: 8 + j, v_col=lambda j: 16 + j,
                    lambda_init=lambda_init, q_parts=4)
    row_lat = lambda i: i // lat_tiles
    x3, hp1, lg1 = _out_project([oc], [c_w_out[0].astype(BF16)], x2,
                                lambda i: (i // lat_tiles) * tiles + 1 + i % lat_tiles, modr, base1, row_lat,
                                norm_g[1, 1][None, :], norm_g[1, 2][None, :], router_w[1].T[rperm],
                                batch * lat_tiles)
    out = _moe(x3, hp1, lg1, router_bias[1], (exp_w1, exp_w3, exp_w2, 1), sh_w1[1], sh_w3[1], sh_w2[1],
               norm_g[1, 3][None, :], modr, base1, lambda i: row_lat(i // (ROW_TILE // MOVE_TILE)))
    return out.reshape(batch, seq, d)
```

```python
import functools
import math

import numpy as np
import jax
import jax.numpy as jnp
from jax import lax
from jax.experimental import pallas as pl
from jax.experimental.pallas import tpu as pltpu

F32 = jnp.float32
BF16 = jnp.bfloat16
U32 = jnp.uint32
I32 = jnp.int32
HIGHEST = lax.Precision.HIGHEST

D_MODEL = 1024
HEAD_DIM = 64
LANES = 128
SUBLANES = 8
GRID_W = 64
ROPE_THETA = 10000.0
EPS = 1e-6
NEG_INF = -1e30
WINDOW = 128
N_EXPERTS = 64
TOP_K = 8
N_GROUPS = 8
TOPK_GROUPS = 4
GROUP_SIZE = N_EXPERTS // N_GROUPS
ROUTED_SCALE = 2.5
LOG2E = 1.4426950408889634
Q_SCALE = HEAD_DIM ** -0.5 * LOG2E

ROW_TILE = 256
PROJ_SUB = 2
Q_TILE = 256
KV_TILE = 256
LAT_KV_TILE = 256
MOVE_TILE = 256
EXPERT_BLOCK = 1024
EXPERT_SLAB = 1024
SEG_ALIGN = 8
RUN_PIECES = (256, 128, 64, 32, 16, 8)
WAIT_PIECES = (2048, 1024, 512) + RUN_PIECES
STAGE_ROWS = 2560
STAGE_CHUNK = 512
HALF = D_MODEL // 2
VMEM_LIMIT = 48 * 1024 * 1024


def _cparams(sem):
    return pltpu.CompilerParams(dimension_semantics=sem, vmem_limit_bytes=VMEM_LIMIT)


def _rms(x, g):
    ms = jnp.mean(x * x, axis=-1, keepdims=True)
    return x * lax.rsqrt(ms + EPS) * g


def _pack_halves(h):
    lo = lax.bitcast_convert_type(h[:, :HALF].astype(BF16).astype(F32), U32) >> 16
    hi = lax.bitcast_convert_type(h[:, HALF:].astype(BF16).astype(F32), U32) & jnp.uint32(0xFFFF0000)
    return hi | lo


def _unpack_halves(u):
    lo = lax.bitcast_convert_type(u << 16, F32)
    hi = lax.bitcast_convert_type(u & jnp.uint32(0xFFFF0000), F32)
    return lo, hi


def _mod_kernel(c_ref, w_ref, b_ref, o_ref):
    c = c_ref[...]
    sc = c * jax.nn.sigmoid(c)
    o_ref[0] = jnp.dot(sc, w_ref[0], precision=HIGHEST, preferred_element_type=F32) + b_ref[0]


def _modulation(cond, ada_w, ada_b):
    depth, d, n = ada_w.shape
    nt = 1536
    return pl.pallas_call(
        _mod_kernel,
        out_shape=jax.ShapeDtypeStruct((depth, 8, n), F32),
        grid=(depth, n // nt),
        in_specs=[pl.BlockSpec((8, d), lambda l, j: (0, 0)),
                  pl.BlockSpec((1, d, nt), lambda l, j: (l, 0, j)),
                  pl.BlockSpec((1, 1, nt), lambda l, j: (l, 0, j))],
        out_specs=pl.BlockSpec((1, 8, nt), lambda l, j: (l, 0, j)),
        compiler_params=_cparams(("arbitrary", "arbitrary")),
        name="ada_mod",
    )(cond, ada_w, ada_b.reshape(depth, 1, n))


def _proj_kernel(*refs, blocks, tiles):
    per_tile = (2 if tiles else 1) + 4
    g_ref, w_ref, hg_ref, gm_ref, o_ref = refs[PROJ_SUB * per_tile:]
    hs, cos, sin = [], [], []
    for s in range(PROJ_SUB):
        sub = refs[s * per_tile:(s + 1) * per_tile]
        if tiles:
            xc_ref, x_ref, sh_ref, sc_ref, cos_ref, sin_ref = sub
            x = jnp.where((pl.program_id(0) * PROJ_SUB + s) % tiles == 0, xc_ref[...], x_ref[...])
        else:
            x_ref, sh_ref, sc_ref, cos_ref, sin_ref = sub
            x = x_ref[...]
        hs.append((_rms(x, g_ref[...]) * (1.0 + sc_ref[...]) + sh_ref[...]).astype(BF16))
        cos.append(cos_ref[...])
        sin.append(sin_ref[...])
    h = jnp.concatenate(hs, axis=0)
    cos = jnp.concatenate(cos, axis=0)
    sin = jnp.concatenate(sin, axis=0)
    y = jnp.dot(h, w_ref[...], preferred_element_type=F32)
    lane = lax.broadcasted_iota(I32, (h.shape[0], LANES), 1)
    first_half = (lane % HEAD_DIM) < (HEAD_DIM // 2)
    for jb, (norm, rope, scale) in enumerate(blocks):
        cols = slice(jb * LANES, (jb + 1) * LANES)
        yb = y[:, cols]
        if norm:
            ms = jnp.dot(yb * yb, gm_ref[...], precision=HIGHEST, preferred_element_type=F32)
            yb = yb * lax.rsqrt(ms + EPS) * hg_ref[:, cols]
        if rope:
            swapped = jnp.where(first_half, pltpu.roll(yb, LANES - HEAD_DIM // 2, 1),
                                pltpu.roll(yb, HEAD_DIM // 2, 1))
            yb = yb * cos + swapped * sin
        if scale != 1.0:
            yb = yb * scale
        o_ref[:, cols] = yb.astype(BF16)


def _pair_specs(pair, tiles):
    ctx_rows, lat_rows = pair
    return [pl.BlockSpec((ROW_TILE, ctx_rows.shape[1]), lambda i: (i // tiles, 0)),
            pl.BlockSpec((ROW_TILE, lat_rows.shape[1]),
                         lambda i: ((i // tiles) * (tiles - 1) + jnp.maximum(i % tiles - 1, 0), 0))]


def _project(x, modr, mod_base, row_of_tile, g, w, cos_t, sin_t, head_gain, blocks, tiles_per_batch):
    paired = isinstance(x, tuple)
    xs = list(x) if paired else [x]
    t, d = sum(a.shape[0] for a in xs), xs[0].shape[1]
    n = w.shape[1]
    assert (t // ROW_TILE) % PROJ_SUB == 0
    group_mean = jnp.asarray(np.kron(np.eye(LANES // HEAD_DIM), np.full((HEAD_DIM, HEAD_DIM), 1.0 / HEAD_DIM)), F32)

    def at_tile(spec, s):
        return pl.BlockSpec(spec.block_shape, lambda i: spec.index_map(i * PROJ_SUB + s))

    tile_specs = (_pair_specs(x, tiles_per_batch) if paired else [pl.BlockSpec((ROW_TILE, d), lambda i: (i, 0))])
    tile_specs += [pl.BlockSpec((None, 1, d), lambda i: (mod_base + row_of_tile(i) * 6 + 0, 0, 0)),
                   pl.BlockSpec((None, 1, d), lambda i: (mod_base + row_of_tile(i) * 6 + 1, 0, 0)),
                   pl.BlockSpec((ROW_TILE, LANES), lambda i: (i % tiles_per_batch, 0)),
                   pl.BlockSpec((ROW_TILE, LANES), lambda i: (i % tiles_per_batch, 0))]
    tile_args = xs + [modr, modr, cos_t, sin_t]
    in_specs, args = [], []
    for s in range(PROJ_SUB):
        in_specs += [at_tile(spec, s) for spec in tile_specs]
        args += tile_args
    in_specs += [pl.BlockSpec((1, d), lambda i: (0, 0)), pl.BlockSpec((d, n), lambda i: (0, 0)),
                 pl.BlockSpec((1, n), lambda i: (0, 0)), pl.BlockSpec((LANES, LANES), lambda i: (0, 0))]
    args += [g, w, head_gain, group_mean]
    return pl.pallas_call(
        functools.partial(_proj_kernel, blocks=blocks, tiles=tiles_per_batch if paired else None),
        out_shape=jax.ShapeDtypeStruct((t, n), BF16),
        grid=(t // (ROW_TILE * PROJ_SUB),),
        in_specs=in_specs,
        out_specs=pl.BlockSpec((ROW_TILE * PROJ_SUB, n), lambda i: (i, 0)),
        compiler_params=_cparams(("arbitrary",)),
        name="prenorm_proj",
    )(*args)


def _attn_kernel(*refs, mode, queries, seq, ctx_len, lambda_init, q_parts):
    q_refs, refs = refs[:q_parts], refs[q_parts:]
    if mode == "window":
        k_ref, v_ref, sink_ref, o_ref = refs
    elif mode == "diff":
        k_ref, v_ref, lam_ref, sg_ref, o_ref = refs
    else:
        k_ref, v_ref, o_ref = refs
    tq = q_parts * q_refs[0].shape[0]
    j = pl.program_id(1)
    qi = pl.program_id(2)

    q = q_refs[0][...] if q_parts == 1 else jnp.concatenate([r[...] for r in q_refs], axis=0)
    lane = lax.broadcasted_iota(I32, (tq, LANES), 1)
    low = lane < HEAD_DIM
    zero = jnp.zeros_like(q)
    q2 = jnp.concatenate([jnp.where(low, q, zero), jnp.where(low, zero, q)], axis=0)

    if mode == "window":
        s0 = sink_ref[pl.ds(2 * j, 1), :]
        s1 = sink_ref[pl.ds(2 * j + 1, 1), :]
        row = lax.broadcasted_iota(I32, (2 * tq, LANES), 0)
        sink = jnp.where(row < tq, s0, s1) * LOG2E
        m0 = sink
    else:
        m0 = jnp.full((2 * tq, LANES), NEG_INF, F32)
    state = (m0, jnp.zeros((2 * tq, LANES), F32), jnp.zeros((2 * tq, LANES), F32))

    def chunk(state, start, valid, size=KV_TILE):
        m_prev, l_prev, acc_prev = state
        k = k_ref[pl.ds(start, size), :]
        v = v_ref[pl.ds(start, size), :]
        s = lax.dot_general(q2, k, (((1,), (1,)), ((), ())), preferred_element_type=F32)
        if valid is not None:
            s = jnp.where(valid, s, NEG_INF)
        m_new = jnp.maximum(m_prev, jnp.max(s, axis=1, keepdims=True))
        alpha = jnp.exp2(m_prev - m_new)
        p = jnp.exp2(s - jnp.concatenate([m_new] * (size // LANES), axis=1))
        part = p[:, :LANES]
        for c in range(1, size // LANES):
            part = part + p[:, c * LANES:(c + 1) * LANES]
        return (m_new, alpha * l_prev + part,
                alpha * acc_prev + jnp.dot(p.astype(BF16), v, preferred_element_type=F32))

    state = chunk(state, 0, None)
    if queries == "latent" and mode == "window":
        n_win = (tq + 2 * WINDOW) // KV_TILE
        q0 = qi * tq
        kstart = jnp.clip(q0 - WINDOW, 0, seq - n_win * KV_TILE)
        r = lax.broadcasted_iota(I32, (2 * tq, KV_TILE), 0)
        qpos = q0 + jnp.where(r >= tq, r - tq, r)
        col = lax.broadcasted_iota(I32, (2 * tq, KV_TILE), 1)
        for w in range(n_win):
            kpos = kstart + w * KV_TILE + col
            state = chunk(state, pl.multiple_of(ctx_len + kstart + w * KV_TILE, WINDOW),
                          jnp.abs(qpos - kpos) <= WINDOW)
    elif queries == "latent":
        for c in range(seq // LAT_KV_TILE):
            state = chunk(state, ctx_len + c * LAT_KV_TILE, None, LAT_KV_TILE)

    m_fin, l_part, acc = state
    l = jnp.sum(l_part, axis=1, keepdims=True)
    if mode == "window":
        l = l + jnp.exp2(sink - m_fin)[:, :1]
    o2 = acc / l
    if mode == "diff":
        lp = lam_ref[...]
        lam = (jnp.exp(jnp.sum(lp[0:1] * lp[1:2], axis=1, keepdims=True))
               - jnp.exp(jnp.sum(lp[2:3] * lp[3:4], axis=1, keepdims=True)) + lambda_init)
        o = o2[:tq] - lam * o2[tq:]
        o = _rms(o, sg_ref[...]) * (1.0 - lambda_init)
    else:
        o = jnp.where(low, o2[:tq], o2[tq:])
    o_ref[...] = o.astype(BF16)


def _attention(p, extra, *, mode, queries, batch, seq, ctx_len, n_qblocks, q_col0, k_col, v_col,
               lambda_init=0.0, q_parts=1):
    n_p = ctx_len + seq
    tiles = n_p // Q_TILE
    tq = q_parts * Q_TILE
    n_rows = seq if queries == "latent" else ctx_len
    qt = n_rows // tq
    q_off = ctx_len // Q_TILE if queries == "latent" else 0
    in_specs = [pl.BlockSpec((Q_TILE, LANES),
                             lambda b, j, qi, part=part: (b * tiles + q_off + qi * q_parts + part, q_col0 + j))
                for part in range(q_parts)]
    in_specs += [pl.BlockSpec((n_p, LANES), lambda b, j, qi: (b, k_col(j))),
                 pl.BlockSpec((n_p, LANES), lambda b, j, qi: (b, v_col(j)))]
    args = [p] * (q_parts + 2)
    for e in extra:
        in_specs.append(pl.BlockSpec(e.shape, lambda b, j, qi: (0, 0)))
        args.append(e)
    kern = functools.partial(_attn_kernel, mode=mode, queries=queries, seq=seq, ctx_len=ctx_len,
                             lambda_init=lambda_init, q_parts=q_parts)
    return pl.pallas_call(
        kern,
        out_shape=jax.ShapeDtypeStruct((batch * n_rows, n_qblocks * LANES), BF16),
        grid=(batch, n_qblocks, qt),
        in_specs=in_specs,
        out_specs=pl.BlockSpec((tq, LANES), lambda b, j, qi: (b * qt + qi, j)),
        compiler_params=_cparams(("arbitrary", "arbitrary", "arbitrary")),
        name="attn_" + mode + "_" + queries,
    )(*args)


def _out_kernel(*refs, n_o, tiles):
    n_in = 2 * n_o if tiles else n_o
    o_refs = refs[:n_in]
    w_refs = refs[n_in:n_in + n_o]
    rest = refs[n_in + n_o:]
    x_refs, rest = (rest[:2], rest[2:]) if tiles else (rest[:1], rest[1:])
    g1_ref, gate_ref, g2_ref, sh_ref, sc_ref, rw_ref, xn_ref, hp_ref, lg_ref = rest
    is_ctx = (pl.program_id(0) % tiles == 0) if tiles else None

    def residual_rows(rows):
        if not tiles:
            return x_refs[0][rows, :]
        return jnp.where(is_ctx, x_refs[0][rows, :], x_refs[1][rows, :])

    def mixer_rows(a, rows):
        if not tiles:
            return o_refs[a][rows, :]
        return jnp.where(is_ctx, o_refs[2 * a][rows, :], o_refs[2 * a + 1][rows, :])

    slab = LANES
    for r0 in range(0, xn_ref.shape[0], slab):
        rows = slice(r0, r0 + slab)
        y = jnp.dot(mixer_rows(0, rows), w_refs[0][...], preferred_element_type=F32)
        for a in range(1, n_o):
            y = y + jnp.dot(mixer_rows(a, rows), w_refs[a][...], preferred_element_type=F32)
        xn = residual_rows(rows) + gate_ref[...] * _rms(y, g1_ref[...])
        xn_ref[rows, :] = xn
        h = _rms(xn, g2_ref[...]) * (1.0 + sc_ref[...]) + sh_ref[...]
        hp_ref[rows, :] = _pack_halves(h)
        lg_ref[:, rows] = lax.dot_general(rw_ref[...], h, (((1,), (1,)), ((), ())), precision=HIGHEST,
                                          preferred_element_type=F32)


def _out_project(os_, ws, x, x_tile, modr, mod_base, row_of_tile, g1, g2, rw_t, n_tiles, tiles=None):
    d = x[0].shape[1] if tiles else x.shape[1]
    n_o = len(os_)
    tm = n_tiles * ROW_TILE
    mspec = lambda which: pl.BlockSpec((None, 1, d), lambda i: (mod_base + row_of_tile(i) * 6 + which, 0, 0))
    if tiles:
        o_specs, o_args = [], []
        for pair in os_:
            o_specs += _pair_specs(pair, tiles)
            o_args += list(pair)
        x_specs, x_args = _pair_specs(x, tiles), list(x)
    else:
        o_specs = [pl.BlockSpec((ROW_TILE, o.shape[1]), lambda i: (i, 0)) for o in os_]
        o_args = list(os_)
        x_specs, x_args = [pl.BlockSpec((ROW_TILE, d), lambda i: (x_tile(i), 0))], [x]
    in_specs = (o_specs
                + [pl.BlockSpec(w.shape, lambda i: (0, 0)) for w in ws]
                + x_specs
                + [pl.BlockSpec((1, d), lambda i: (0, 0)), mspec(2),
                   pl.BlockSpec((1, d), lambda i: (0, 0)), mspec(3), mspec(4),
                   pl.BlockSpec(rw_t.shape, lambda i: (0, 0))])
    return pl.pallas_call(
        functools.partial(_out_kernel, n_o=n_o, tiles=tiles),
        out_shape=(jax.ShapeDtypeStruct((tm, d), F32), jax.ShapeDtypeStruct((tm, HALF), U32),
                   jax.ShapeDtypeStruct((N_EXPERTS, tm), F32)),
        grid=(n_tiles,),
        in_specs=in_specs,
        out_specs=(pl.BlockSpec((ROW_TILE, d), lambda i: (i, 0)),
                   pl.BlockSpec((ROW_TILE, HALF), lambda i: (i, 0)),
                   pl.BlockSpec((N_EXPERTS, ROW_TILE), lambda i: (0, i))),
        compiler_params=_cparams(("arbitrary",)),
        name="out_proj",
    )(*o_args, *ws, *x_args, g1, modr, g2, modr, modr, rw_t)


def _route_kernel(lg_ref, bias_ref, tri_ref, ltri_ref, ls_ref, w_ref, segn_ref, segc_ref, cnt_ref, carry_ref):
    i = pl.program_id(0)

    @pl.when(i == 0)
    def _():
        carry_ref[...] = jnp.zeros(carry_ref.shape, F32)

    tr = lg_ref.shape[1]
    score = jax.nn.sigmoid(lg_ref[...])
    sel = score + bias_ref[...][:, :1]
    sel_j = [sel[j * GROUP_SIZE:(j + 1) * GROUP_SIZE] for j in range(GROUP_SIZE)]
    sc_j = [score[j * GROUP_SIZE:(j + 1) * GROUP_SIZE] for j in range(GROUP_SIZE)]
    gi = lax.broadcasted_iota(I32, (N_GROUPS, tr), 0)

    m1 = sel_j[0]
    m2 = jnp.full_like(m1, -jnp.inf)
    for j in range(1, GROUP_SIZE):
        m2 = jnp.maximum(m2, jnp.minimum(m1, sel_j[j]))
        m1 = jnp.maximum(m1, sel_j[j])
    gs = m1 + m2

    grank = jnp.zeros((N_GROUPS, tr), I32)
    for gp in range(N_GROUPS):
        rowv = gs[gp:gp + 1, :]
        grank = grank + jnp.where(gi > gp, jnp.where(rowv >= gs, 1, 0), jnp.where(rowv > gs, 1, 0))
    gmask = grank < TOPK_GROUPS
    val_j = [jnp.where(gmask, sel_j[j], NEG_INF) for j in range(GROUP_SIZE)]

    e_j = [(gi * GROUP_SIZE + j).astype(F32) for j in range(GROUP_SIZE)]
    work_j = list(val_j)
    picked_j = [jnp.zeros((N_GROUPS, tr), F32) for _ in range(GROUP_SIZE)]
    idx_k = []
    for k in range(TOP_K):
        best = work_j[0]
        for j in range(1, GROUP_SIZE):
            best = jnp.maximum(best, work_j[j])
        best = jnp.max(best, axis=0, keepdims=True)
        first = jnp.where(work_j[0] == best, e_j[0], float(N_EXPERTS))
        for j in range(1, GROUP_SIZE):
            first = jnp.minimum(first, jnp.where(work_j[j] == best, e_j[j], float(N_EXPERTS)))
        first = jnp.min(first, axis=0, keepdims=True)
        idx_k.append(first)
        for j in range(GROUP_SIZE):
            hit = e_j[j] == first
            work_j[j] = jnp.where(hit, -jnp.inf, work_j[j])
            picked_j[j] = jnp.where(hit, 1.0, picked_j[j])

    chosen = jnp.concatenate(picked_j, axis=0)
    n_run = jnp.floor((jnp.sum(chosen, axis=1, keepdims=True) + (SEG_ALIGN - 1.0)) * (1.0 / SEG_ALIGN)) * SEG_ALIGN
    n_run = jnp.broadcast_to(n_run, (N_EXPERTS, LANES))
    run_start = jnp.dot(ltri_ref[...], n_run.astype(BF16), preferred_element_type=F32)
    local = jnp.dot(chosen.astype(BF16), tri_ref[...], preferred_element_type=F32) + run_start[:, :1]
    segn_ref[0] = n_run.astype(I32)
    segc_ref[0] = carry_ref[...].astype(I32)
    carry_ref[...] = carry_ref[...] + n_run
    cnt_ref[...] = carry_ref[...].astype(I32)

    w_rows, ls_rows = [], []
    for k in range(TOP_K):
        w_acc = jnp.zeros((N_GROUPS, tr), F32)
        p_acc = jnp.zeros((N_GROUPS, tr), F32)
        for j in range(GROUP_SIZE):
            hit = e_j[j] == idx_k[k]
            w_acc = w_acc + jnp.where(hit, sc_j[j], 0.0)
            p_acc = p_acc + jnp.where(hit, local[j * GROUP_SIZE:(j + 1) * GROUP_SIZE], 0.0)
        w_rows.append(jnp.sum(w_acc, axis=0, keepdims=True))
        ls_rows.append(jnp.sum(p_acc, axis=0, keepdims=True))
    w_all = jnp.concatenate(w_rows, axis=0)
    w_ref[...] = w_all / jnp.sum(w_all, axis=0, keepdims=True) * ROUTED_SCALE
    ls_ref[...] = jnp.concatenate(ls_rows, axis=0).astype(I32)


def _route(lg_t, bias_rep):
    tm = lg_t.shape[1]
    n_tiles = tm // MOVE_TILE
    tri = jnp.asarray(np.triu(np.ones((MOVE_TILE, MOVE_TILE), np.float32), 1), BF16)
    ltri = jnp.asarray(np.tril(np.ones((N_EXPERTS, N_EXPERTS), np.float32), -1), BF16)
    tok = lambda i: (0, i)
    per_tile = pl.BlockSpec((1, N_EXPERTS, LANES), lambda i: (i, 0, 0))
    return pl.pallas_call(
        _route_kernel,
        out_shape=(jax.ShapeDtypeStruct((TOP_K, tm), I32), jax.ShapeDtypeStruct((TOP_K, tm), F32),
                   jax.ShapeDtypeStruct((n_tiles, N_EXPERTS, LANES), I32),
                   jax.ShapeDtypeStruct((n_tiles, N_EXPERTS, LANES), I32),
                   jax.ShapeDtypeStruct((N_EXPERTS, LANES), I32)),
        grid=(n_tiles,),
        in_specs=[pl.BlockSpec((N_EXPERTS, MOVE_TILE), tok),
                  pl.BlockSpec((N_EXPERTS, LANES), lambda i: (0, 0)),
                  pl.BlockSpec((MOVE_TILE, MOVE_TILE), lambda i: (0, 0)),
                  pl.BlockSpec((N_EXPERTS, N_EXPERTS), lambda i: (0, 0))],
        out_specs=(pl.BlockSpec((TOP_K, MOVE_TILE), tok), pl.BlockSpec((TOP_K, MOVE_TILE), tok),
                   per_tile, per_tile, pl.BlockSpec((N_EXPERTS, LANES), lambda i: (0, 0))),
        scratch_shapes=[pltpu.VMEM((N_EXPERTS, LANES), F32)],
        compiler_params=_cparams(("arbitrary",)),
        name="route",
    )(lg_t, bias_rep, tri, ltri)


def _start_pieces(n, src, dst, make, pieces):
    for size in pieces:
        above = n & (-2 * size)

        @pl.when((n & size) != 0)
        def _():
            make(pl.multiple_of(src + above, SEG_ALIGN), pl.multiple_of(dst + above, SEG_ALIGN), size).start()


def _start_run_copies(i, segn_ref, segd_ref, make, straight_line=False):
    split = RUN_PIECES.index(64)

    def per_run(r, src):
        n = segn_ref[i * N_EXPERTS + r]
        dst = segd_ref[i * N_EXPERTS + r]

        @pl.when(n >= RUN_PIECES[split - 1])
        def _():
            _start_pieces(n, src, dst, make, RUN_PIECES[:split])

        _start_pieces(n, src, dst, make, RUN_PIECES[split:])
        return src + n

    if not straight_line:
        return lax.fori_loop(0, N_EXPERTS, per_run, 0, unroll=2)
    src = 0
    for r in range(N_EXPERTS):
        src = per_run(r, src)
    return src


def _wait_rows(total, make):
    for size in WAIT_PIECES:
        @pl.when((total & size) != 0)
        def _():
            make(0, 0, size).wait()


def _fill_copies(zstart_ref, zlen_ref, make):
    pieces = tuple(p for p in WAIT_PIECES if p < EXPERT_BLOCK)

    def per_expert(r, carry):
        n = zlen_ref[r]
        _start_pieces(n, 0, zstart_ref[r], make, pieces)
        _wait_rows(n, make)
        return carry

    lax.fori_loop(0, N_EXPERTS, per_expert, 0)


def _dispatch_kernel(segn_ref, segd_ref, zstart_ref, zlen_ref, ls_ref, hp_ref, xs_hbm, stage_ref, zero_ref,
                     rows_ref, sem, zsem):
    i = pl.program_id(0)
    buf = i % 2
    lo, hi = _unpack_halves(hp_ref[...])
    lo = lo.astype(BF16)
    hi = hi.astype(BF16)
    ls = ls_ref[...]
    pack_rows = 16
    ls16 = [jnp.tile(jnp.broadcast_to(ls[k:k + 1, :], (pack_rows, MOVE_TILE)).astype(jnp.int16),
                     (STAGE_CHUNK // pack_rows, 1)) for k in range(TOP_K)]
    slot0 = lax.broadcasted_iota(I32, (STAGE_CHUNK, MOVE_TILE), 0).astype(jnp.int16)
    one = jnp.ones((STAGE_CHUNK, MOVE_TILE), BF16)
    for c in range(STAGE_ROWS // STAGE_CHUNK):
        slot = slot0 + jnp.int16(c * STAGE_CHUNK)
        onehot = jnp.zeros((STAGE_CHUNK, MOVE_TILE), BF16)
        for k in range(TOP_K):
            onehot = jnp.where(ls16[k] == slot, one, onehot)
        a = lax.bitcast_convert_type(jnp.dot(onehot, lo, preferred_element_type=F32), U32) >> 16
        b = lax.bitcast_convert_type(jnp.dot(onehot, hi, preferred_element_type=F32), U32)
        stage_ref[buf, c * STAGE_CHUNK:(c + 1) * STAGE_CHUNK, :] = b | a

    def to_slots(which):
        def make(src, dst, size):
            return pltpu.make_async_copy(stage_ref.at[which, pl.ds(src, size), :],
                                         xs_hbm.at[pl.ds(dst, size), :], sem.at[which])
        return make

    rows_ref[buf] = _start_run_copies(i, segn_ref, segd_ref, to_slots(buf))

    @pl.when(i > 0)
    def _():
        _wait_rows(rows_ref[1 - buf], to_slots(1 - buf))

    @pl.when(i == pl.num_programs(0) - 1)
    def _():
        _wait_rows(rows_ref[buf], to_slots(buf))

    @pl.when(i == 0)
    def _():
        zero_ref[...] = jnp.zeros(zero_ref.shape, U32)

        def zeros_to_slots(src, dst, size):
            return pltpu.make_async_copy(zero_ref.at[pl.ds(0, size), :], xs_hbm.at[pl.ds(dst, size), :], zsem)

        _fill_copies(zstart_ref, zlen_ref, zeros_to_slots)


def _dispatch(segn, segd, zstart, zlen, ls, hp, n_slots):
    tm = hp.shape[0]
    tile = lambda i, *_: (i, 0)
    return pl.pallas_call(
        _dispatch_kernel,
        out_shape=jax.ShapeDtypeStruct((n_slots, HALF), U32),
        grid_spec=pltpu.PrefetchScalarGridSpec(
            num_scalar_prefetch=4,
            grid=(tm // MOVE_TILE,),
            in_specs=[pl.BlockSpec((TOP_K, MOVE_TILE), lambda i, *_: (0, i)),
                      pl.BlockSpec((MOVE_TILE, HALF), tile)],
            out_specs=pl.BlockSpec(memory_space=pl.ANY),
            scratch_shapes=[pltpu.VMEM((2, STAGE_ROWS, HALF), U32), pltpu.VMEM((EXPERT_BLOCK // 2, HALF), U32),
                            pltpu.SMEM((2,), I32), pltpu.SemaphoreType.DMA((2,)), pltpu.SemaphoreType.DMA]),
        compiler_params=_cparams(("arbitrary",)),
        name="dispatch",
    )(segn, segd, zstart, zlen, ls, hp)


def _expert_kernel(be_ref, nu_ref, nxt_ref, half_ref, xs_ref, w1_hbm, w3_hbm, w2_hbm, ys_ref, w1f_ref, w3f_ref, w2f_ref,
                   w1b_ref, w3b_ref, w2b_ref, run_ref, sem, *, layer):
    i = pl.program_id(0)

    def fetch(e, slot):
        return [pltpu.make_async_copy(src.at[layer, e], dst.at[slot], sem.at[slot])
                for src, dst in ((w1_hbm, w1f_ref), (w3_hbm, w3f_ref), (w2_hbm, w2f_ref))]

    @pl.when(i == 0)
    def _():
        run_ref[0] = 0
        for cp in fetch(be_ref[0], 0):
            cp.start()

    first = jnp.logical_or(i == 0, be_ref[i] != be_ref[jnp.maximum(i - 1, 0)])

    @pl.when(jnp.logical_and(first, i < nu_ref[0]))
    def _():
        slot = run_ref[0] % 2
        for cp in fetch(be_ref[i], slot):
            cp.wait()
        w1b_ref[...] = w1f_ref[slot].astype(BF16)
        w3b_ref[...] = w3f_ref[slot].astype(BF16)
        w2b_ref[...] = w2f_ref[slot].astype(BF16)
        nxt = nxt_ref[i]

        @pl.when(nxt >= 0)
        def _():
            for cp in fetch(nxt, 1 - slot):
                cp.start()

        run_ref[0] = run_ref[0] + 1

    def swiglu(n_rows):
        rows = slice(0, n_rows)
        lo, hi = _unpack_halves(xs_ref[rows, :])
        lo = lo.astype(BF16)
        hi = hi.astype(BF16)
        h1 = (jnp.dot(lo, w1b_ref[:HALF, :], preferred_element_type=F32)
              + jnp.dot(hi, w1b_ref[HALF:, :], preferred_element_type=F32))
        h3 = (jnp.dot(lo, w3b_ref[:HALF, :], preferred_element_type=F32)
              + jnp.dot(hi, w3b_ref[HALF:, :], preferred_element_type=F32))
        g = (h1 * jax.nn.sigmoid(h1) * h3).astype(BF16)
        y = jnp.dot(g, w2b_ref[...], preferred_element_type=F32)
        ys_ref[rows, :] = _pack_halves(y)
        if n_rows < EXPERT_BLOCK:
            ys_ref[n_rows:, :] = jnp.zeros((EXPERT_BLOCK - n_rows, HALF), U32)

    active = i < nu_ref[0]
    small = half_ref[i] != 0

    @pl.when(jnp.logical_and(active, jnp.logical_not(small)))
    def _():
        swiglu(EXPERT_BLOCK)

    @pl.when(jnp.logical_and(active, small))
    def _():
        swiglu(EXPERT_BLOCK // 2)


def _experts(block_e, n_used, next_e, half_full, xs, w1, w3, w2, layer):
    n_slots = xs.shape[0]
    n_blocks = n_slots // EXPERT_BLOCK
    d, eh = w1.shape[2], w1.shape[3]
    rows = lambda i, be, nu, nx, hf: (jnp.minimum(i, nu[0] - 1), 0)
    anyspace = pl.BlockSpec(memory_space=pl.ANY)
    return pl.pallas_call(
        functools.partial(_expert_kernel, layer=layer),
        out_shape=jax.ShapeDtypeStruct((n_slots, HALF), U32),
        grid_spec=pltpu.PrefetchScalarGridSpec(
            num_scalar_prefetch=4,
            grid=(n_blocks,),
            in_specs=[pl.BlockSpec((EXPERT_BLOCK, HALF), rows), anyspace, anyspace, anyspace],
            out_specs=pl.BlockSpec((EXPERT_BLOCK, HALF), rows),
            scratch_shapes=[pltpu.VMEM((2, d, eh), F32), pltpu.VMEM((2, d, eh), F32), pltpu.VMEM((2, eh, d), F32),
                            pltpu.VMEM((d, eh), BF16), pltpu.VMEM((d, eh), BF16), pltpu.VMEM((eh, d), BF16),
                            pltpu.SMEM((1,), I32), pltpu.SemaphoreType.DMA((2,))]),
        compiler_params=_cparams(("arbitrary",)),
        name="experts",
    )(block_e, n_used, next_e, half_full, xs, w1, w3, w2)


def _combine_kernel(segn_ref, segd_ref, ys_hbm, lst_ref, wt_ref, hp_ref, x_ref, s1_ref, s3_ref, s2_ref, g_ref,
                    gate_ref, o_ref, stage_ref, rows_ref, sem):
    i = pl.program_id(0)
    buf = i % 2

    def from_slots(which):
        def make(src, dst, size):
            return pltpu.make_async_copy(ys_hbm.at[pl.ds(dst, size), :],
                                         stage_ref.at[which, pl.ds(src, size), :], sem.at[which])
        return make

    @pl.when(i == 0)
    def _():
        stage_ref[...] = jnp.zeros(stage_ref.shape, U32)
        rows_ref[0] = _start_run_copies(0, segn_ref, segd_ref, from_slots(0))

    last = pl.num_programs(0) - 1
    rows_ref[1 - buf] = _start_run_copies(jnp.minimum(i + 1, last), segn_ref, segd_ref, from_slots(1 - buf),
                                          straight_line=True)

    lo, hi = _unpack_halves(hp_ref[...])
    lo = lo.astype(BF16)
    hi = hi.astype(BF16)
    h1 = (jnp.dot(lo, s1_ref[:HALF, :], preferred_element_type=F32)
          + jnp.dot(hi, s1_ref[HALF:, :], preferred_element_type=F32))
    h3 = (jnp.dot(lo, s3_ref[:HALF, :], preferred_element_type=F32)
          + jnp.dot(hi, s3_ref[HALF:, :], preferred_element_type=F32))
    y = jnp.dot((h1 * jax.nn.sigmoid(h1) * h3).astype(BF16), s2_ref[...], preferred_element_type=F32)

    _wait_rows(rows_ref[buf], from_slots(buf))

    @pl.when(i == last)
    def _():
        _wait_rows(rows_ref[1 - buf], from_slots(1 - buf))

    lst = lst_ref[...]
    wt = wt_ref[...]
    lst16 = [jnp.broadcast_to(lst[:, k:k + 1], (MOVE_TILE, STAGE_CHUNK)).astype(jnp.int16) for k in range(TOP_K)]
    wt16 = [jnp.broadcast_to(wt[:, k:k + 1], (MOVE_TILE, STAGE_CHUNK)).astype(BF16) for k in range(TOP_K)]
    slot0 = lax.broadcasted_iota(I32, (MOVE_TILE, STAGE_CHUNK), 1).astype(jnp.int16)
    r_lo = jnp.zeros((MOVE_TILE, HALF), F32)
    r_hi = jnp.zeros((MOVE_TILE, HALF), F32)
    for c in range(STAGE_ROWS // STAGE_CHUNK):
        slot = slot0 + jnp.int16(c * STAGE_CHUNK)
        wsel = jnp.zeros((MOVE_TILE, STAGE_CHUNK), BF16)
        for k in range(TOP_K):
            wsel = jnp.where(lst16[k] == slot, wt16[k], wsel)
        a, b = _unpack_halves(stage_ref[buf, c * STAGE_CHUNK:(c + 1) * STAGE_CHUNK, :])
        r_lo = r_lo + jnp.dot(wsel, a.astype(BF16), preferred_element_type=F32)
        r_hi = r_hi + jnp.dot(wsel, b.astype(BF16), preferred_element_type=F32)
    y = y + jnp.concatenate([r_lo, r_hi], axis=1)
    o_ref[...] = x_ref[...] + gate_ref[...] * _rms(y, g_ref[...])


def _combine(segn, segd, ys, lst, wt, hp, x, s1, s3, s2, g, modr, mod_base, row_of_tile):
    tm, d = x.shape
    full = lambda a: pl.BlockSpec(a.shape, lambda i, *_: (0, 0))
    tile = lambda i, *_: (i, 0)
    return pl.pallas_call(
        _combine_kernel,
        out_shape=jax.ShapeDtypeStruct((tm, d), F32),
        grid_spec=pltpu.PrefetchScalarGridSpec(
            num_scalar_prefetch=2,
            grid=(tm // MOVE_TILE,),
            in_specs=[pl.BlockSpec(memory_space=pl.ANY),
                      pl.BlockSpec((MOVE_TILE, TOP_K), tile), pl.BlockSpec((MOVE_TILE, TOP_K), tile),
                      pl.BlockSpec((MOVE_TILE, HALF), tile), pl.BlockSpec((MOVE_TILE, d), tile),
                      full(s1), full(s3), full(s2), full(g),
                      pl.BlockSpec((None, 1, d), lambda i, *_: (mod_base + row_of_tile(i) * 6 + 5, 0, 0))],
            out_specs=pl.BlockSpec((MOVE_TILE, d), tile),
            scratch_shapes=[pltpu.VMEM((2, STAGE_ROWS, HALF), U32), pltpu.SMEM((2,), I32),
                            pltpu.SemaphoreType.DMA((2,))]),
        compiler_params=_cparams(("arbitrary",)),
        name="combine",
    )(segn, segd, ys, lst, wt, hp, x, s1, s3, s2, g, modr)


def _moe(xn, hp, lg_t, bias, expert_weights, s1, s3, s2, g, modr, mod_base, row_of_tile):
    tm = xn.shape[0]
    n_tiles = tm // MOVE_TILE
    perm = np.array([(r % GROUP_SIZE) * GROUP_SIZE + r // GROUP_SIZE for r in range(N_EXPERTS)])
    bias_rep = jnp.broadcast_to(bias.astype(F32)[perm][:, None], (N_EXPERTS, LANES))
    ls, w, segn, segc, cnt = _route(lg_t, bias_rep)

    total = cnt[:, 0]
    padded = (total + EXPERT_BLOCK - 1) // EXPERT_BLOCK * EXPERT_BLOCK
    pad_end = jnp.cumsum(padded)
    pad_start = pad_end - padded
    n_slots = tm * TOP_K + (SEG_ALIGN - 1) * N_EXPERTS * n_tiles + N_EXPERTS * EXPERT_BLOCK
    n_blocks = -(-n_slots // EXPERT_BLOCK)
    starts = jnp.arange(n_blocks, dtype=I32) * EXPERT_BLOCK
    region = jnp.minimum(jnp.sum((pad_end[None, :] <= starts[:, None]).astype(I32), axis=1), N_EXPERTS - 1)
    block_e = jnp.asarray(perm, I32)[region]
    n_used = (pad_end[-1:] // EXPERT_BLOCK).astype(I32)
    after = pad_end[region] // EXPERT_BLOCK
    next_e = jnp.where(after < n_used[0], block_e[jnp.minimum(after, n_blocks - 1)], -1).astype(I32)
    valid_rows = jnp.clip(pad_start[region] + total[region] - starts, 0, EXPERT_BLOCK)
    half_full = (valid_rows <= EXPERT_BLOCK // 2).astype(I32)
    segn = segn[:, :, 0].reshape(-1)
    segd = (segc[:, :, 0] + pad_start[None, :]).reshape(-1).astype(I32)

    xs = _dispatch(segn, segd, (pad_start + total).astype(I32), (padded - total).astype(I32), ls, hp,
                   n_blocks * EXPERT_BLOCK)
    ys = _experts(block_e, n_used, next_e, half_full, xs, *expert_weights)
    return _combine(segn, segd, ys, ls.T, w.T, hp, xn, s1.astype(BF16), s3.astype(BF16), s2.astype(BF16), g, modr,
                    mod_base, row_of_tile)


_DEINTERLEAVE = np.concatenate([np.arange(0, HEAD_DIM, 2), np.arange(1, HEAD_DIM, 2)])


def _rope_tables(seq, ctx_len):
    t = np.arange(seq)
    n_pair = HEAD_DIM // 4
    inv = jnp.asarray(ROPE_THETA, F32) ** (-jnp.arange(n_pair, dtype=F32) / n_pair)
    r = jnp.asarray(t // GRID_W, F32)
    c = jnp.asarray(t % GRID_W, F32)
    ang = jnp.concatenate([r[:, None] * inv, c[:, None] * inv], axis=-1)
    cos, sin = jnp.cos(ang), jnp.sin(ang)
    cos_t = jnp.tile(jnp.concatenate([cos, cos], axis=-1), (1, LANES // HEAD_DIM))
    sin_t = jnp.tile(jnp.concatenate([-sin, sin], axis=-1), (1, LANES // HEAD_DIM))
    cos_t = jnp.concatenate([jnp.ones((ctx_len, LANES), F32), cos_t], axis=0)
    sin_t = jnp.concatenate([jnp.zeros((ctx_len, LANES), F32), sin_t], axis=0)
    return cos_t, sin_t


def _ab_layout():
    cols, blocks, gain_kind = [], [], []
    for base_q, base_k, base_v, normed in ((0, 512, 640, True), (768, 1280, 1408, False)):
        for jb in range(4):
            cols += [base_q + h * HEAD_DIM + _DEINTERLEAVE for h in (2 * jb, 2 * jb + 1)]
            blocks.append((normed, True, Q_SCALE))
            gain_kind.append("q" if normed else None)
        for kvh in range(2):
            cols += [base_k + kvh * HEAD_DIM + _DEINTERLEAVE] * 2
            blocks.append((normed, True, 1.0))
            gain_kind.append("k" if normed else None)
        for kvh in range(2):
            cols += [base_v + kvh * HEAD_DIM + np.arange(HEAD_DIM)] * 2
            blocks.append((False, False, 1.0))
            gain_kind.append(None)
    return np.concatenate(cols), tuple(blocks), gain_kind


def _c_layout():
    cols, blocks = [], []
    for base, rope, scale in ((0, True, Q_SCALE), (1024, True, 1.0)):
        for h in range(8):
            cols += [base + (2 * h + m) * HEAD_DIM + _DEINTERLEAVE for m in range(2)]
            blocks.append((False, rope, scale))
    for h in range(8):
        cols.append(2048 + h * LANES + np.arange(LANES))
        blocks.append((False, False, 1.0))
    return np.concatenate(cols), tuple(blocks)


def kernel(x, c, ctx, c_ctx, ada_w, ada_b, norm_g, ab_w_in, ab_w_out, a_q_norm, a_k_norm, b_sink,
           c_w_in, c_w_out, c_lambda, c_subln_g, router_w, router_bias, exp_w1, exp_w3, exp_w2,
           sh_w1, sh_w3, sh_w2):
    batch, seq, d = x.shape
    ctx_len = ctx.shape[1]
    depth = ada_w.shape[0]
    assert d == D_MODEL and ctx_len == ROW_TILE and depth == 2
    assert seq % (4 * Q_TILE) == 0 and batch <= 4
    n_p = ctx_len + seq
    tiles = n_p // ROW_TILE
    lat_tiles = seq // ROW_TILE
    t_all = batch * n_p

    cond = jnp.zeros((8, d), F32).at[:batch].set(c).at[4].set(c_ctx)
    modr = _modulation(cond, ada_w, ada_b).reshape(depth * 8 * 6, 1, d)
    row_all = lambda i: jnp.where(i % tiles == 0, 4, i // tiles)
    cos_t, sin_t = _rope_tables(seq, ctx_len)
    rperm = np.array([(r % GROUP_SIZE) * GROUP_SIZE + r // GROUP_SIZE for r in range(N_EXPERTS)])
    xt = (ctx.reshape(batch * ctx_len, d), x.reshape(batch * seq, d))

    cols, blocks, gain_kind = _ab_layout()
    w0 = ab_w_in[0][:, cols].astype(BF16)
    gq = jnp.tile(a_q_norm[0][_DEINTERLEAVE], 2)
    gk = jnp.tile(a_k_norm[0][_DEINTERLEAVE], 2)
    ones = jnp.ones((LANES,), F32)
    head_gain = jnp.concatenate([{"q": gq, "k": gk, None: ones}[kind] for kind in gain_kind])[None, :]
    p0 = _project(xt, modr, 0, row_all, norm_g[0, 0][None, :], w0, cos_t, sin_t, head_gain, blocks, tiles)
    common = dict(batch=batch, seq=seq, ctx_len=ctx_len, n_qblocks=4)
    dense = dict(mode="dense", q_col0=0, k_col=lambda j: 4 + j // 2, v_col=lambda j: 6 + j // 2, **common)
    window = dict(mode="window", q_col0=8, k_col=lambda j: 12 + j // 2, v_col=lambda j: 14 + j // 2, **common)
    sink = jnp.broadcast_to(b_sink[0].astype(F32)[:, None], (8, LANES))
    oa = (_attention(p0, [], queries="context", **dense), _attention(p0, [], queries="latent", q_parts=4, **dense))
    ob = (_attention(p0, [sink], queries="context", **window),
          _attention(p0, [sink], queries="latent", **window))
    w_out = ab_w_out[0].astype(BF16)
    x1, hp, lg = _out_project([oa, ob], [w_out[:512], w_out[512:]], xt, lambda i: i, modr, 0, row_all,
                              norm_g[0, 1][None, :], norm_g[0, 2][None, :], router_w[0].T[rperm],
                              t_all // ROW_TILE, tiles=tiles)
    x2 = _moe(x1, hp, lg, router_bias[0], (exp_w1, exp_w3, exp_w2, 0), sh_w1[0], sh_w3[0], sh_w2[0],
              norm_g[0, 3][None, :], modr, 0, lambda i: row_all(i // (ROW_TILE // MOVE_TILE)))

    base1 = 8 * 6
    lambda_init = 0.8 - 0.6 * math.exp(-0.3 * 1)
    cols1, blocks1 = _c_layout()
    w1p = c_w_in[0][:, cols1].astype(BF16)
    p1 = _project(x2, modr, base1, row_all, norm_g[1, 0][None, :], w1p, cos_t, sin_t,
                  jnp.ones((1, w1p.shape[1]), F32), blocks1, tiles)
    lam = jnp.zeros((8, LANES), F32).at[:4, :HEAD_DIM].set(c_lambda[0].astype(F32))
    oc = _attention(p1, [lam, c_subln_g[0][None, :]], mode="diff", queries="latent", batch=batch, seq=seq,
                    ctx_len=ctx_len, n_qblocks=8, q_col0=0, k_col=lambda j: 8 + j, v_col=lambda j: 16 + j,
                    lambda_init=lambda_init, q_parts=4)
    row_lat = lambda i: i // lat_tiles
    x3, hp1, lg1 = _out_project([oc], [c_w_out[0].astype(BF16)], x2,
                                lambda i: (i // lat_tiles) * tiles + 1 + i % lat_tiles, modr, base1, row_lat,
                                norm_g[1, 1][None, :], norm_g[1, 2][None, :], router_w[1].T[rperm],
                                batch * lat_tiles)
    out = _moe(x3, hp1, lg1, router_bias[1], (exp_w1, exp_w3, exp_w2, 1), sh_w1[1], sh_w3[1], sh_w2[1],
               norm_g[1, 3][None, :], modr, base1, lambda i: row_lat(i // (ROW_TILE // MOVE_TILE)))
    return out.reshape(batch, seq, d)
```

```python
import functools
import math

import numpy as np
import jax
import jax.numpy as jnp
from jax import lax
from jax.experimental import pallas as pl
from jax.experimental.pallas import tpu as pltpu

F32 = jnp.float32
BF16 = jnp.bfloat16
U32 = jnp.uint32
I32 = jnp.int32
HIGHEST = lax.Precision.HIGHEST

D_MODEL = 1024
HEAD_DIM = 64
LANES = 128
SUBLANES = 8
GRID_W = 64
ROPE_THETA = 10000.0
EPS = 1e-6
NEG_INF = -1e30
WINDOW = 128
N_EXPERTS = 64
TOP_K = 8
N_GROUPS = 8
TOPK_GROUPS = 4
GROUP_SIZE = N_EXPERTS // N_GROUPS
ROUTED_SCALE = 2.5
LOG2E = 1.4426950408889634
Q_SCALE = HEAD_DIM ** -0.5 * LOG2E

ROW_TILE = 256
PROJ_SUB = 2
Q_TILE = 256
KV_TILE = 256
LAT_KV_TILE = 256
MOVE_TILE = 256
EXPERT_BLOCK = 1024
EXPERT_SLAB = 1024
SEG_ALIGN = 8
RUN_PIECES = (256, 128, 64, 32, 16, 8)
WAIT_PIECES = (2048, 1024, 512) + RUN_PIECES
STAGE_ROWS = 2560
STAGE_CHUNK = 512
HALF = D_MODEL // 2
VMEM_LIMIT = 48 * 1024 * 1024


def _cparams(sem):
    return pltpu.CompilerParams(dimension_semantics=sem, vmem_limit_bytes=VMEM_LIMIT)


def _rms(x, g):
    ms = jnp.mean(x * x, axis=-1, keepdims=True)
    return x * lax.rsqrt(ms + EPS) * g


def _pack_halves(h):
    lo = lax.bitcast_convert_type(h[:, :HALF].astype(BF16).astype(F32), U32) >> 16
    hi = lax.bitcast_convert_type(h[:, HALF:].astype(BF16).astype(F32), U32) & jnp.uint32(0xFFFF0000)
    return hi | lo


def _unpack_halves(u):
    lo = lax.bitcast_convert_type(u << 16, F32)
    hi = lax.bitcast_convert_type(u & jnp.uint32(0xFFFF0000), F32)
    return lo, hi


def _mod_kernel(c_ref, w_ref, b_ref, o_ref):
    c = c_ref[...]
    sc = c * jax.nn.sigmoid(c)
    o_ref[0] = jnp.dot(sc, w_ref[0], precision=HIGHEST, preferred_element_type=F32) + b_ref[0]


def _modulation(cond, ada_w, ada_b):
    depth, d, n = ada_w.shape
    nt = 1536
    return pl.pallas_call(
        _mod_kernel,
        out_shape=jax.ShapeDtypeStruct((depth, 8, n), F32),
        grid=(depth, n // nt),
        in_specs=[pl.BlockSpec((8, d), lambda l, j: (0, 0)),
                  pl.BlockSpec((1, d, nt), lambda l, j: (l, 0, j)),
                  pl.BlockSpec((1, 1, nt), lambda l, j: (l, 0, j))],
        out_specs=pl.BlockSpec((1, 8, nt), lambda l, j: (l, 0, j)),
        compiler_params=_cparams(("arbitrary", "arbitrary")),
        name="ada_mod",
    )(cond, ada_w, ada_b.reshape(depth, 1, n))


def _proj_kernel(*refs, blocks, tiles):
    per_tile = (2 if tiles else 1) + 4
    g_ref, w_ref, hg_ref, gm_ref, o_ref = refs[PROJ_SUB * per_tile:]
    hs, cos, sin = [], [], []
    for s in range(PROJ_SUB):
        sub = refs[s * per_tile:(s + 1) * per_tile]
        if tiles:
            xc_ref, x_ref, sh_ref, sc_ref, cos_ref, sin_ref = sub
            x = jnp.where((pl.program_id(0) * PROJ_SUB + s) % tiles == 0, xc_ref[...], x_ref[...])
        else:
            x_ref, sh_ref, sc_ref, cos_ref, sin_ref = sub
            x = x_ref[...]
        hs.append((_rms(x, g_ref[...]) * (1.0 + sc_ref[...]) + sh_ref[...]).astype(BF16))
        cos.append(cos_ref[...])
        sin.append(sin_ref[...])
    h = jnp.concatenate(hs, axis=0)
    cos = jnp.concatenate(cos, axis=0)
    sin = jnp.concatenate(sin, axis=0)
    y = jnp.dot(h, w_ref[...], preferred_element_type=F32)
    lane = lax.broadcasted_iota(I32, (h.shape[0], LANES), 1)
    first_half = (lane % HEAD_DIM) < (HEAD_DIM // 2)
    for jb, (norm, rope, scale) in enumerate(blocks):
        cols = slice(jb * LANES, (jb + 1) * LANES)
        yb = y[:, cols]
        if norm:
            ms = jnp.dot(yb * yb, gm_ref[...], precision=HIGHEST, preferred_element_type=F32)
            yb = yb * lax.rsqrt(ms + EPS) * hg_ref[:, cols]
        if rope:
            swapped = jnp.where(first_half, pltpu.roll(yb, LANES - HEAD_DIM // 2, 1),
                                pltpu.roll(yb, HEAD_DIM // 2, 1))
            yb = yb * cos + swapped * sin
        if scale != 1.0:
            yb = yb * scale
        o_ref[:, cols] = yb.astype(BF16)


def _pair_specs(pair, tiles):
    ctx_rows, lat_rows = pair
    return [pl.BlockSpec((ROW_TILE, ctx_rows.shape[1]), lambda i: (i // tiles, 0)),
            pl.BlockSpec((ROW_TILE, lat_rows.shape[1]),
                         lambda i: ((i // tiles) * (tiles - 1) + jnp.maximum(i % tiles - 1, 0), 0))]


def _project(x, modr, mod_base, row_of_tile, g, w, cos_t, sin_t, head_gain, blocks, tiles_per_batch):
    paired = isinstance(x, tuple)
    xs = list(x) if paired else [x]
    t, d = sum(a.shape[0] for a in xs), xs[0].shape[1]
    n = w.shape[1]
    assert (t // ROW_TILE) % PROJ_SUB == 0
    group_mean = jnp.asarray(np.kron(np.eye(LANES // HEAD_DIM), np.full((HEAD_DIM, HEAD_DIM), 1.0 / HEAD_DIM)), F32)

    def at_tile(spec, s):
        return pl.BlockSpec(spec.block_shape, lambda i: spec.index_map(i * PROJ_SUB + s))

    tile_specs = (_pair_specs(x, tiles_per_batch) if paired else [pl.BlockSpec((ROW_TILE, d), lambda i: (i, 0))])
    tile_specs += [pl.BlockSpec((None, 1, d), lambda i: (mod_base + row_of_tile(i) * 6 + 0, 0, 0)),
                   pl.BlockSpec((None, 1, d), lambda i: (mod_base + row_of_tile(i) * 6 + 1, 0, 0)),
                   pl.BlockSpec((ROW_TILE, LANES), lambda i: (i % tiles_per_batch, 0)),
                   pl.BlockSpec((ROW_TILE, LANES), lambda i: (i % tiles_per_batch, 0))]
    tile_args = xs + [modr, modr, cos_t, sin_t]
    in_specs, args = [], []
    for s in range(PROJ_SUB):
        in_specs += [at_tile(spec, s) for spec in tile_specs]
        args += tile_args
    in_specs += [pl.BlockSpec((1, d), lambda i: (0, 0)), pl.BlockSpec((d, n), lambda i: (0, 0)),
                 pl.BlockSpec((1, n), lambda i: (0, 0)), pl.BlockSpec((LANES, LANES), lambda i: (0, 0))]
    args += [g, w, head_gain, group_mean]
    return pl.pallas_call(
        functools.partial(_proj_kernel, blocks=blocks, tiles=tiles_per_batch if paired else None),
        out_shape=jax.ShapeDtypeStruct((t, n), BF16),
        grid=(t // (ROW_TILE * PROJ_SUB),),
        in_specs=in_specs,
        out_specs=pl.BlockSpec((ROW_TILE * PROJ_SUB, n), lambda i: (i, 0)),
        compiler_params=_cparams(("arbitrary",)),
        name="prenorm_proj",
    )(*args)


def _attn_kernel(*refs, mode, queries, seq, ctx_len, lambda_init, q_parts):
    q_refs, refs = refs[:q_parts], refs[q_parts:]
    if mode == "window":
        k_ref, v_ref, sink_ref, o_ref = refs
    elif mode == "diff":
        k_ref, v_ref, lam_ref, sg_ref, o_ref = refs
    else:
        k_ref, v_ref, o_ref = refs
    tq = q_parts * q_refs[0].shape[0]
    j = pl.program_id(1)
    qi = pl.program_id(2)

    q = q_refs[0][...] if q_parts == 1 else jnp.concatenate([r[...] for r in q_refs], axis=0)
    lane = lax.broadcasted_iota(I32, (tq, LANES), 1)
    low = lane < HEAD_DIM
    zero = jnp.zeros_like(q)
    q2 = jnp.concatenate([jnp.where(low, q, zero), jnp.where(low, zero, q)], axis=0)

    if mode == "window":
        s0 = sink_ref[pl.ds(2 * j, 1), :]
        s1 = sink_ref[pl.ds(2 * j + 1, 1), :]
        row = lax.broadcasted_iota(I32, (2 * tq, LANES), 0)
        sink = jnp.where(row < tq, s0, s1) * LOG2E
        m0 = sink
    else:
        m0 = jnp.full((2 * tq, LANES), NEG_INF, F32)
    state = (m0, jnp.zeros((2 * tq, LANES), F32), jnp.zeros((2 * tq, LANES), F32))

    def chunk(state, start, valid, size=KV_TILE):
        m_prev, l_prev, acc_prev = state
        k = k_ref[pl.ds(start, size), :]
        v = v_ref[pl.ds(start, size), :]
        s = lax.dot_general(q2, k, (((1,), (1,)), ((), ())), preferred_element_type=F32)
        if valid is not None:
            s = jnp.where(valid, s, NEG_INF)
        m_new = jnp.maximum(m_prev, jnp.max(s, axis=1, keepdims=True))
        alpha = jnp.exp2(m_prev - m_new)
        p = jnp.exp2(s - jnp.concatenate([m_new] * (size // LANES), axis=1))
        part = p[:, :LANES]
        for c in range(1, size // LANES):
            part = part + p[:, c * LANES:(c + 1) * LANES]
        return (m_new, alpha * l_prev + part,
                alpha * acc_prev + jnp.dot(p.astype(BF16), v, preferred_element_type=F32))

    state = chunk(state, 0, None)
    if queries == "latent" and mode == "window":
        n_win = (tq + 2 * WINDOW) // KV_TILE
        q0 = qi * tq
        kstart = jnp.clip(q0 - WINDOW, 0, seq - n_win * KV_TILE)
        r = lax.broadcasted_iota(I32, (2 * tq, KV_TILE), 0)
        qpos = q0 + jnp.where(r >= tq, r - tq, r)
        col = lax.broadcasted_iota(I32, (2 * tq, KV_TILE), 1)
        for w in range(n_win):
            kpos = kstart + w * KV_TILE + col
            state = chunk(state, pl.multiple_of(ctx_len + kstart + w * KV_TILE, WINDOW),
                          jnp.abs(qpos - kpos) <= WINDOW)
    elif queries == "latent":
        for c in range(seq // LAT_KV_TILE):
            state = chunk(state, ctx_len + c * LAT_KV_TILE, None, LAT_KV_TILE)

    m_fin, l_part, acc = state
    l = jnp.sum(l_part, axis=1, keepdims=True)
    if mode == "window":
        l = l + jnp.exp2(sink - m_fin)[:, :1]
    o2 = acc / l
    if mode == "diff":
        lp = lam_ref[...]
        lam = (jnp.exp(jnp.sum(lp[0:1] * lp[1:2], axis=1, keepdims=True))
               - jnp.exp(jnp.sum(lp[2:3] * lp[3:4], axis=1, keepdims=True)) + lambda_init)
        o = o2[:tq] - lam * o2[tq:]
        o = _rms(o, sg_ref[...]) * (1.0 - lambda_init)
    else:
        o = jnp.where(low, o2[:tq], o2[tq:])
    o_ref[...] = o.astype(BF16)


def _attention(p, extra, *, mode, queries, batch, seq, ctx_len, n_qblocks, q_col0, k_col, v_col,
               lambda_init=0.0, q_parts=1):
    n_p = ctx_len + seq
    tiles = n_p // Q_TILE
    tq = q_parts * Q_TILE
    n_rows = seq if queries == "latent" else ctx_len
    qt = n_rows // tq
    q_off = ctx_len // Q_TILE if queries == "latent" else 0
    in_specs = [pl.BlockSpec((Q_TILE, LANES),
                             lambda b, j, qi, part=part: (b * tiles + q_off + qi * q_parts + part, q_col0 + j))
                for part in range(q_parts)]
    in_specs += [pl.BlockSpec((n_p, LANES), lambda b, j, qi: (b, k_col(j))),
                 pl.BlockSpec((n_p, LANES), lambda b, j, qi: (b, v_col(j)))]
    args = [p] * (q_parts + 2)
    for e in extra:
        in_specs.append(pl.BlockSpec(e.shape, lambda b, j, qi: (0, 0)))
        args.append(e)
    kern = functools.partial(_attn_kernel, mode=mode, queries=queries, seq=seq, ctx_len=ctx_len,
                             lambda_init=lambda_init, q_parts=q_parts)
    return pl.pallas_call(
        kern,
        out_shape=jax.ShapeDtypeStruct((batch * n_rows, n_qblocks * LANES), BF16),
        grid=(batch, n_qblocks, qt),
        in_specs=in_specs,
        out_specs=pl.BlockSpec((tq, LANES), lambda b, j, qi: (b * qt + qi, j)),
        compiler_params=_cparams(("arbitrary", "arbitrary", "arbitrary")),
        name="attn_" + mode + "_" + queries,
    )(*args)


def _out_kernel(*refs, n_o, tiles):
    n_in = 2 * n_o if tiles else n_o
    o_refs = refs[:n_in]
    w_refs = refs[n_in:n_in + n_o]
    rest = refs[n_in + n_o:]
    x_refs, rest = (rest[:2], rest[2:]) if tiles else (rest[:1], rest[1:])
    g1_ref, gate_ref, g2_ref, sh_ref, sc_ref, rw_ref, xn_ref, hp_ref, lg_ref = rest
    is_ctx = (pl.program_id(0) % tiles == 0) if tiles else None

    def residual_rows(rows):
        if not tiles:
            return x_refs[0][rows, :]
        return jnp.where(is_ctx, x_refs[0][rows, :], x_refs[1][rows, :])

    def mixer_rows(a, rows):
        if not tiles:
            return o_refs[a][rows, :]
        return jnp.where(is_ctx, o_refs[2 * a][rows, :], o_refs[2 * a + 1][rows, :])

    slab = LANES
    for r0 in range(0, xn_ref.shape[0], slab):
        rows = slice(r0, r0 + slab)
        y = jnp.dot(mixer_rows(0, rows), w_refs[0][...], preferred_element_type=F32)
        for a in range(1, n_o):
            y = y + jnp.dot(mixer_rows(a, rows), w_refs[a][...], preferred_element_type=F32)
        xn = residual_rows(rows) + gate_ref[...] * _rms(y, g1_ref[...])
        xn_ref[rows, :] = xn
        h = _rms(xn, g2_ref[...]) * (1.0 + sc_ref[...]) + sh_ref[...]
        hp_ref[rows, :] = _pack_halves(h)
        lg_ref[:, rows] = lax.dot_general(rw_ref[...], h, (((1,), (1,)), ((), ())), precision=HIGHEST,
                                          preferred_element_type=F32)


def _out_project(os_, ws, x, x_tile, modr, mod_base, row_of_tile, g1, g2, rw_t, n_tiles, tiles=None):
    d = x[0].shape[1] if tiles else x.shape[1]
    n_o = len(os_)
    tm = n_tiles * ROW_TILE
    mspec = lambda which: pl.BlockSpec((None, 1, d), lambda i: (mod_base + row_of_tile(i) * 6 + which, 0, 0))
    if tiles:
        o_specs, o_args = [], []
        for pair in os_:
            o_specs += _pair_specs(pair, tiles)
            o_args += list(pair)
        x_specs, x_args = _pair_specs(x, tiles), list(x)
    else:
        o_specs = [pl.BlockSpec((ROW_TILE, o.shape[1]), lambda i: (i, 0)) for o in os_]
        o_args = list(os_)
        x_specs, x_args = [pl.BlockSpec((ROW_TILE, d), lambda i: (x_tile(i), 0))], [x]
    in_specs = (o_specs
                + [pl.BlockSpec(w.shape, lambda i: (0, 0)) for w in ws]
                + x_specs
                + [pl.BlockSpec((1, d), lambda i: (0, 0)), mspec(2),
                   pl.BlockSpec((1, d), lambda i: (0, 0)), mspec(3), mspec(4),
                   pl.BlockSpec(rw_t.shape, lambda i: (0, 0))])
    return pl.pallas_call(
        functools.partial(_out_kernel, n_o=n_o, tiles=tiles),
        out_shape=(jax.ShapeDtypeStruct((tm, d), F32), jax.ShapeDtypeStruct((tm, HALF), U32),
                   jax.ShapeDtypeStruct((N_EXPERTS, tm), F32)),
        grid=(n_tiles,),
        in_specs=in_specs,
        out_specs=(pl.BlockSpec((ROW_TILE, d), lambda i: (i, 0)),
                   pl.BlockSpec((ROW_TILE, HALF), lambda i: (i, 0)),
                   pl.BlockSpec((N_EXPERTS, ROW_TILE), lambda i: (0, i))),
        compiler_params=_cparams(("arbitrary",)),
        name="out_proj",
    )(*o_args, *ws, *x_args, g1, modr, g2, modr, modr, rw_t)


def _route_kernel(lg_ref, bias_ref, tri_ref, ltri_ref, ls_ref, w_ref, segn_ref, segc_ref, cnt_ref, carry_ref):
    i = pl.program_id(0)

    @pl.when(i == 0)
    def _():
        carry_ref[...] = jnp.zeros(carry_ref.shape, F32)

    tr = lg_ref.shape[1]
    score = jax.nn.sigmoid(lg_ref[...])
    sel = score + bias_ref[...][:, :1]
    sel_j = [sel[j * GROUP_SIZE:(j + 1) * GROUP_SIZE] for j in range(GROUP_SIZE)]
    sc_j = [score[j * GROUP_SIZE:(j + 1) * GROUP_SIZE] for j in range(GROUP_SIZE)]
    gi = lax.broadcasted_iota(I32, (N_GROUPS, tr), 0)

    m1 = sel_j[0]
    m2 = jnp.full_like(m1, -jnp.inf)
    for j in range(1, GROUP_SIZE):
        m2 = jnp.maximum(m2, jnp.minimum(m1, sel_j[j]))
        m1 = jnp.maximum(m1, sel_j[j])
    gs = m1 + m2

    grank = jnp.zeros((N_GROUPS, tr), I32)
    for gp in range(N_GROUPS):
        rowv = gs[gp:gp + 1, :]
        grank = grank + jnp.where(gi > gp, jnp.where(rowv >= gs, 1, 0), jnp.where(rowv > gs, 1, 0))
    gmask = grank < TOPK_GROUPS
    val_j = [jnp.where(gmask, sel_j[j], NEG_INF) for j in range(GROUP_SIZE)]

    e_j = [(gi * GROUP_SIZE + j).astype(F32) for j in range(GROUP_SIZE)]
    work_j = list(val_j)
    picked_j = [jnp.zeros((N_GROUPS, tr), F32) for _ in range(GROUP_SIZE)]
    idx_k = []
    for k in range(TOP_K):
        best = work_j[0]
        for j in range(1, GROUP_SIZE):
            best = jnp.maximum(best, work_j[j])
        best = jnp.max(best, axis=0, keepdims=True)
        first = jnp.where(work_j[0] == best, e_j[0], float(N_EXPERTS))
        for j in range(1, GROUP_SIZE):
            first = jnp.minimum(first, jnp.where(work_j[j] == best, e_j[j], float(N_EXPERTS)))
        first = jnp.min(first, axis=0, keepdims=True)
        idx_k.append(first)
        for j in range(GROUP_SIZE):
            hit = e_j[j] == first
            work_j[j] = jnp.where(hit, -jnp.inf, work_j[j])
            picked_j[j] = jnp.where(hit, 1.0, picked_j[j])

    chosen = jnp.concatenate(picked_j, axis=0)
    n_run = jnp.floor((jnp.sum(chosen, axis=1, keepdims=True) + (SEG_ALIGN - 1.0)) * (1.0 / SEG_ALIGN)) * SEG_ALIGN
    n_run = jnp.broadcast_to(n_run, (N_EXPERTS, LANES))
    run_start = jnp.dot(ltri_ref[...], n_run.astype(BF16), preferred_element_type=F32)
    local = jnp.dot(chosen.astype(BF16), tri_ref[...], preferred_element_type=F32) + run_start[:, :1]
    segn_ref[0] = n_run.astype(I32)
    segc_ref[0] = carry_ref[...].astype(I32)
    carry_ref[...] = carry_ref[...] + n_run
    cnt_ref[...] = carry_ref[...].astype(I32)

    w_rows, ls_rows = [], []
    for k in range(TOP_K):
        w_acc = jnp.zeros((N_GROUPS, tr), F32)
        p_acc = jnp.zeros((N_GROUPS, tr), F32)
        for j in range(GROUP_SIZE):
            hit = e_j[j] == idx_k[k]
            w_acc = w_acc + jnp.where(hit, sc_j[j], 0.0)
            p_acc = p_acc + jnp.where(hit, local[j * GROUP_SIZE:(j + 1) * GROUP_SIZE], 0.0)
        w_rows.append(jnp.sum(w_acc, axis=0, keepdims=True))
        ls_rows.append(jnp.sum(p_acc, axis=0, keepdims=True))
    w_all = jnp.concatenate(w_rows, axis=0)
    w_ref[...] = w_all / jnp.sum(w_all, axis=0, keepdims=True) * ROUTED_SCALE
    ls_ref[...] = jnp.concatenate(ls_rows, axis=0).astype(I32)


def _route(lg_t, bias_rep):
    tm = lg_t.shape[1]
    n_tiles = tm // MOVE_TILE
    tri = jnp.asarray(np.triu(np.ones((MOVE_TILE, MOVE_TILE), np.float32), 1), BF16)
    ltri = jnp.asarray(np.tril(np.ones((N_EXPERTS, N_EXPERTS), np.float32), -1), BF16)
    tok = lambda i: (0, i)
    per_tile = pl.BlockSpec((1, N_EXPERTS, LANES), lambda i: (i, 0, 0))
    return pl.pallas_call(
        _route_kernel,
        out_shape=(jax.ShapeDtypeStruct((TOP_K, tm), I32), jax.ShapeDtypeStruct((TOP_K, tm), F32),
                   jax.ShapeDtypeStruct((n_tiles, N_EXPERTS, LANES), I32),
                   jax.ShapeDtypeStruct((n_tiles, N_EXPERTS, LANES), I32),
                   jax.ShapeDtypeStruct((N_EXPERTS, LANES), I32)),
        grid=(n_tiles,),
        in_specs=[pl.BlockSpec((N_EXPERTS, MOVE_TILE), tok),
                  pl.BlockSpec((N_EXPERTS, LANES), lambda i: (0, 0)),
                  pl.BlockSpec((MOVE_TILE, MOVE_TILE), lambda i: (0, 0)),
                  pl.BlockSpec((N_EXPERTS, N_EXPERTS), lambda i: (0, 0))],
        out_specs=(pl.BlockSpec((TOP_K, MOVE_TILE), tok), pl.BlockSpec((TOP_K, MOVE_TILE), tok),
                   per_tile, per_tile, pl.BlockSpec((N_EXPERTS, LANES), lambda i: (0, 0))),
        scratch_shapes=[pltpu.VMEM((N_EXPERTS, LANES), F32)],
        compiler_params=_cparams(("arbitrary",)),
        name="route",
    )(lg_t, bias_rep, tri, ltri)


def _start_pieces(n, src, dst, make, pieces):
    for which, size in enumerate(pieces):
        above = n & (-2 * size)

        @pl.when((n & size) != 0)
        def _():
            make(pl.multiple_of(src + above, SEG_ALIGN), pl.multiple_of(dst + above, SEG_ALIGN),
                 size).start(priority=which % 2)


def _start_run_copies(i, segn_ref, segd_ref, make, straight_line=False):
    split = RUN_PIECES.index(64)

    def per_run(r, src):
        n = segn_ref[i * N_EXPERTS + r]
        dst = segd_ref[i * N_EXPERTS + r]

        @pl.when(n >= RUN_PIECES[split - 1])
        def _():
            _start_pieces(n, src, dst, make, RUN_PIECES[:split])

        _start_pieces(n, src, dst, make, RUN_PIECES[split:])
        return src + n

    if not straight_line:
        return lax.fori_loop(0, N_EXPERTS, per_run, 0, unroll=2)
    src = 0
    for r in range(N_EXPERTS):
        src = per_run(r, src)
    return src


def _wait_rows(total, make):
    for size in WAIT_PIECES:
        @pl.when((total & size) != 0)
        def _():
            make(0, 0, size).wait()


def _fill_copies(zstart_ref, zlen_ref, make):
    pieces = tuple(p for p in WAIT_PIECES if p < EXPERT_BLOCK)

    def per_expert(r, carry):
        n = zlen_ref[r]
        _start_pieces(n, 0, zstart_ref[r], make, pieces)
        _wait_rows(n, make)
        return carry

    lax.fori_loop(0, N_EXPERTS, per_expert, 0)


def _dispatch_kernel(segn_ref, segd_ref, zstart_ref, zlen_ref, ls_ref, hp_ref, xs_hbm, stage_ref, zero_ref,
                     rows_ref, sem, zsem):
    i = pl.program_id(0)
    buf = i % 2
    lo, hi = _unpack_halves(hp_ref[...])
    lo = lo.astype(BF16)
    hi = hi.astype(BF16)
    ls = ls_ref[...]
    pack_rows = 16
    ls16 = [jnp.tile(jnp.broadcast_to(ls[k:k + 1, :], (pack_rows, MOVE_TILE)).astype(jnp.int16),
                     (STAGE_CHUNK // pack_rows, 1)) for k in range(TOP_K)]
    slot0 = lax.broadcasted_iota(I32, (STAGE_CHUNK, MOVE_TILE), 0).astype(jnp.int16)
    one = jnp.ones((STAGE_CHUNK, MOVE_TILE), BF16)
    for c in range(STAGE_ROWS // STAGE_CHUNK):
        slot = slot0 + jnp.int16(c * STAGE_CHUNK)
        onehot = jnp.zeros((STAGE_CHUNK, MOVE_TILE), BF16)
        for k in range(TOP_K):
            onehot = jnp.where(ls16[k] == slot, one, onehot)
        a = lax.bitcast_convert_type(jnp.dot(onehot, lo, preferred_element_type=F32), U32) >> 16
        b = lax.bitcast_convert_type(jnp.dot(onehot, hi, preferred_element_type=F32), U32)
        stage_ref[buf, c * STAGE_CHUNK:(c + 1) * STAGE_CHUNK, :] = b | a

    def to_slots(which):
        def make(src, dst, size):
            return pltpu.make_async_copy(stage_ref.at[which, pl.ds(src, size), :],
                                         xs_hbm.at[pl.ds(dst, size), :], sem.at[which])
        return make

    rows_ref[buf] = _start_run_copies(i, segn_ref, segd_ref, to_slots(buf))

    @pl.when(i > 0)
    def _():
        _wait_rows(rows_ref[1 - buf], to_slots(1 - buf))

    @pl.when(i == pl.num_programs(0) - 1)
    def _():
        _wait_rows(rows_ref[buf], to_slots(buf))

    @pl.when(i == 0)
    def _():
        zero_ref[...] = jnp.zeros(zero_ref.shape, U32)

        def zeros_to_slots(src, dst, size):
            return pltpu.make_async_copy(zero_ref.at[pl.ds(0, size), :], xs_hbm.at[pl.ds(dst, size), :], zsem)

        _fill_copies(zstart_ref, zlen_ref, zeros_to_slots)


def _dispatch(segn, segd, zstart, zlen, ls, hp, n_slots):
    tm = hp.shape[0]
    tile = lambda i, *_: (i, 0)
    return pl.pallas_call(
        _dispatch_kernel,
        out_shape=jax.ShapeDtypeStruct((n_slots, HALF), U32),
        grid_spec=pltpu.PrefetchScalarGridSpec(
            num_scalar_prefetch=4,
            grid=(tm // MOVE_TILE,),
            in_specs=[pl.BlockSpec((TOP_K, MOVE_TILE), lambda i, *_: (0, i)),
                      pl.BlockSpec((MOVE_TILE, HALF), tile)],
            out_specs=pl.BlockSpec(memory_space=pl.ANY),
            scratch_shapes=[pltpu.VMEM((2, STAGE_ROWS, HALF), U32), pltpu.VMEM((EXPERT_BLOCK // 2, HALF), U32),
                            pltpu.SMEM((2,), I32), pltpu.SemaphoreType.DMA((2,)), pltpu.SemaphoreType.DMA]),
        compiler_params=_cparams(("arbitrary",)),
        name="dispatch",
    )(segn, segd, zstart, zlen, ls, hp)


def _expert_kernel(be_ref, nu_ref, nxt_ref, xs_ref, w1_hbm, w3_hbm, w2_hbm, ys_ref, w1f_ref, w3f_ref, w2f_ref,
                   w1b_ref, w3b_ref, w2b_ref, run_ref, sem, *, layer):
    i = pl.program_id(0)

    def fetch(e, slot):
        return [pltpu.make_async_copy(src.at[layer, e], dst.at[slot], sem.at[slot])
                for src, dst in ((w1_hbm, w1f_ref), (w3_hbm, w3f_ref), (w2_hbm, w2f_ref))]

    @pl.when(i == 0)
    def _():
        run_ref[0] = 0
        for cp in fetch(be_ref[0], 0):
            cp.start()

    first = jnp.logical_or(i == 0, be_ref[i] != be_ref[jnp.maximum(i - 1, 0)])

    @pl.when(jnp.logical_and(first, i < nu_ref[0]))
    def _():
        slot = run_ref[0] % 2
        for cp in fetch(be_ref[i], slot):
            cp.wait()
        w1b_ref[...] = w1f_ref[slot].astype(BF16)
        w3b_ref[...] = w3f_ref[slot].astype(BF16)
        w2b_ref[...] = w2f_ref[slot].astype(BF16)
        nxt = nxt_ref[i]

        @pl.when(nxt >= 0)
        def _():
            for cp in fetch(nxt, 1 - slot):
                cp.start()

        run_ref[0] = run_ref[0] + 1

    @pl.when(i < nu_ref[0])
    def _():
        for rows in (slice(r0, r0 + EXPERT_SLAB) for r0 in range(0, EXPERT_BLOCK, EXPERT_SLAB)):
            lo, hi = _unpack_halves(xs_ref[rows, :])
            lo = lo.astype(BF16)
            hi = hi.astype(BF16)
            h1 = (jnp.dot(lo, w1b_ref[:HALF, :], preferred_element_type=F32)
                  + jnp.dot(hi, w1b_ref[HALF:, :], preferred_element_type=F32))
            h3 = (jnp.dot(lo, w3b_ref[:HALF, :], preferred_element_type=F32)
                  + jnp.dot(hi, w3b_ref[HALF:, :], preferred_element_type=F32))
            g = (h1 * jax.nn.sigmoid(h1) * h3).astype(BF16)
            y = jnp.dot(g, w2b_ref[...], preferred_element_type=F32)
            ys_ref[rows, :] = _pack_halves(y)


def _experts(block_e, n_used, next_e, xs, w1, w3, w2, layer):
    n_slots = xs.shape[0]
    n_blocks = n_slots // EXPERT_BLOCK
    d, eh = w1.shape[2], w1.shape[3]
    rows = lambda i, be, nu, nx: (jnp.minimum(i, nu[0] - 1), 0)
    anyspace = pl.BlockSpec(memory_space=pl.ANY)
    return pl.pallas_call(
        functools.partial(_expert_kernel, layer=layer),
        out_shape=jax.ShapeDtypeStruct((n_slots, HALF), U32),
        grid_spec=pltpu.PrefetchScalarGridSpec(
            num_scalar_prefetch=3,
            grid=(n_blocks,),
            in_specs=[pl.BlockSpec((EXPERT_BLOCK, HALF), rows), anyspace, anyspace, anyspace],
            out_specs=pl.BlockSpec((EXPERT_BLOCK, HALF), rows),
            scratch_shapes=[pltpu.VMEM((2, d, eh), F32), pltpu.VMEM((2, d, eh), F32), pltpu.VMEM((2, eh, d), F32),
                            pltpu.VMEM((d, eh), BF16), pltpu.VMEM((d, eh), BF16), pltpu.VMEM((eh, d), BF16),
                            pltpu.SMEM((1,), I32), pltpu.SemaphoreType.DMA((2,))]),
        compiler_params=_cparams(("arbitrary",)),
        name="experts",
    )(block_e, n_used, next_e, xs, w1, w3, w2)


def _combine_kernel(segn_ref, segd_ref, ys_hbm, lst_ref, wt_ref, hp_ref, x_ref, s1_ref, s3_ref, s2_ref, g_ref,
                    gate_ref, o_ref, stage_ref, rows_ref, sem):
    i = pl.program_id(0)
    buf = i % 2

    def from_slots(which):
        def make(src, dst, size):
            return pltpu.make_async_copy(ys_hbm.at[pl.ds(dst, size), :],
                                         stage_ref.at[which, pl.ds(src, size), :], sem.at[which])
        return make

    @pl.when(i == 0)
    def _():
        stage_ref[...] = jnp.zeros(stage_ref.shape, U32)
        rows_ref[0] = _start_run_copies(0, segn_ref, segd_ref, from_slots(0))

    last = pl.num_programs(0) - 1
    rows_ref[1 - buf] = _start_run_copies(jnp.minimum(i + 1, last), segn_ref, segd_ref, from_slots(1 - buf),
                                          straight_line=True)

    lo, hi = _unpack_halves(hp_ref[...])
    lo = lo.astype(BF16)
    hi = hi.astype(BF16)
    h1 = (jnp.dot(lo, s1_ref[:HALF, :], preferred_element_type=F32)
          + jnp.dot(hi, s1_ref[HALF:, :], preferred_element_type=F32))
    h3 = (jnp.dot(lo, s3_ref[:HALF, :], preferred_element_type=F32)
          + jnp.dot(hi, s3_ref[HALF:, :], preferred_element_type=F32))
    y = jnp.dot((h1 * jax.nn.sigmoid(h1) * h3).astype(BF16), s2_ref[...], preferred_element_type=F32)

    _wait_rows(rows_ref[buf], from_slots(buf))

    @pl.when(i == last)
    def _():
        _wait_rows(rows_ref[1 - buf], from_slots(1 - buf))

    lst = lst_ref[...]
    wt = wt_ref[...]
    lst16 = [jnp.broadcast_to(lst[:, k:k + 1], (MOVE_TILE, STAGE_CHUNK)).astype(jnp.int16) for k in range(TOP_K)]
    wt16 = [jnp.broadcast_to(wt[:, k:k + 1], (MOVE_TILE, STAGE_CHUNK)).astype(BF16) for k in range(TOP_K)]
    slot0 = lax.broadcasted_iota(I32, (MOVE_TILE, STAGE_CHUNK), 1).astype(jnp.int16)
    r_lo = jnp.zeros((MOVE_TILE, HALF), F32)
    r_hi = jnp.zeros((MOVE_TILE, HALF), F32)
    for c in range(STAGE_ROWS // STAGE_CHUNK):
        slot = slot0 + jnp.int16(c * STAGE_CHUNK)
        wsel = jnp.zeros((MOVE_TILE, STAGE_CHUNK), BF16)
        for k in range(TOP_K):
            wsel = jnp.where(lst16[k] == slot, wt16[k], wsel)
        a, b = _unpack_halves(stage_ref[buf, c * STAGE_CHUNK:(c + 1) * STAGE_CHUNK, :])
        r_lo = r_lo + jnp.dot(wsel, a.astype(BF16), preferred_element_type=F32)
        r_hi = r_hi + jnp.dot(wsel, b.astype(BF16), preferred_element_type=F32)
    y = y + jnp.concatenate([r_lo, r_hi], axis=1)
    o_ref[...] = x_ref[...] + gate_ref[...] * _rms(y, g_ref[...])


def _combine(segn, segd, ys, lst, wt, hp, x, s1, s3, s2, g, modr, mod_base, row_of_tile):
    tm, d = x.shape
    full = lambda a: pl.BlockSpec(a.shape, lambda i, *_: (0, 0))
    tile = lambda i, *_: (i, 0)
    return pl.pallas_call(
        _combine_kernel,
        out_shape=jax.ShapeDtypeStruct((tm, d), F32),
        grid_spec=pltpu.PrefetchScalarGridSpec(
            num_scalar_prefetch=2,
            grid=(tm // MOVE_TILE,),
            in_specs=[pl.BlockSpec(memory_space=pl.ANY),
                      pl.BlockSpec((MOVE_TILE, TOP_K), tile), pl.BlockSpec((MOVE_TILE, TOP_K), tile),
                      pl.BlockSpec((MOVE_TILE, HALF), tile), pl.BlockSpec((MOVE_TILE, d), tile),
                      full(s1), full(s3), full(s2), full(g),
                      pl.BlockSpec((None, 1, d), lambda i, *_: (mod_base + row_of_tile(i) * 6 + 5, 0, 0))],
            out_specs=pl.BlockSpec((MOVE_TILE, d), tile),
            scratch_shapes=[pltpu.VMEM((2, STAGE_ROWS, HALF), U32), pltpu.SMEM((2,), I32),
                            pltpu.SemaphoreType.DMA((2,))]),
        compiler_params=_cparams(("arbitrary",)),
        name="combine",
    )(segn, segd, ys, lst, wt, hp, x, s1, s3, s2, g, modr)


def _moe(xn, hp, lg_t, bias, expert_weights, s1, s3, s2, g, modr, mod_base, row_of_tile):
    tm = xn.shape[0]
    n_tiles = tm // MOVE_TILE
    perm = np.array([(r % GROUP_SIZE) * GROUP_SIZE + r // GROUP_SIZE for r in range(N_EXPERTS)])
    bias_rep = jnp.broadcast_to(bias.astype(F32)[perm][:, None], (N_EXPERTS, LANES))
    ls, w, segn, segc, cnt = _route(lg_t, bias_rep)

    total = cnt[:, 0]
    padded = (total + EXPERT_BLOCK - 1) // EXPERT_BLOCK * EXPERT_BLOCK
    pad_end = jnp.cumsum(padded)
    pad_start = pad_end - padded
    n_slots = tm * TOP_K + (SEG_ALIGN - 1) * N_EXPERTS * n_tiles + N_EXPERTS * EXPERT_BLOCK
    n_blocks = -(-n_slots // EXPERT_BLOCK)
    starts = jnp.arange(n_blocks, dtype=I32) * EXPERT_BLOCK
    region = jnp.minimum(jnp.sum((pad_end[None, :] <= starts[:, None]).astype(I32), axis=1), N_EXPERTS - 1)
    block_e = jnp.asarray(perm, I32)[region]
    n_used = (pad_end[-1:] // EXPERT_BLOCK).astype(I32)
    after = pad_end[region] // EXPERT_BLOCK
    next_e = jnp.where(after < n_used[0], block_e[jnp.minimum(after, n_blocks - 1)], -1).astype(I32)
    segn = segn[:, :, 0].reshape(-1)
    segd = (segc[:, :, 0] + pad_start[None, :]).reshape(-1).astype(I32)

    xs = _dispatch(segn, segd, (pad_start + total).astype(I32), (padded - total).astype(I32), ls, hp,
                   n_blocks * EXPERT_BLOCK)
    ys = _experts(block_e, n_used, next_e, xs, *expert_weights)
    return _combine(segn, segd, ys, ls.T, w.T, hp, xn, s1.astype(BF16), s3.astype(BF16), s2.astype(BF16), g, modr,
                    mod_base, row_of_tile)


_DEINTERLEAVE = np.concatenate([np.arange(0, HEAD_DIM, 2), np.arange(1, HEAD_DIM, 2)])


def _rope_tables(seq, ctx_len):
    t = np.arange(seq)
    n_pair = HEAD_DIM // 4
    inv = jnp.asarray(ROPE_THETA, F32) ** (-jnp.arange(n_pair, dtype=F32) / n_pair)
    r = jnp.asarray(t // GRID_W, F32)
    c = jnp.asarray(t % GRID_W, F32)
    ang = jnp.concatenate([r[:, None] * inv, c[:, None] * inv], axis=-1)
    cos, sin = jnp.cos(ang), jnp.sin(ang)
    cos_t = jnp.tile(jnp.concatenate([cos, cos], axis=-1), (1, LANES // HEAD_DIM))
    sin_t = jnp.tile(jnp.concatenate([-sin, sin], axis=-1), (1, LANES // HEAD_DIM))
    cos_t = jnp.concatenate([jnp.ones((ctx_len, LANES), F32), cos_t], axis=0)
    sin_t = jnp.concatenate([jnp.zeros((ctx_len, LANES), F32), sin_t], axis=0)
    return cos_t, sin_t


def _ab_layout():
    cols, blocks, gain_kind = [], [], []
    for base_q, base_k, base_v, normed in ((0, 512, 640, True), (768, 1280, 1408, False)):
        for jb in range(4):
            cols += [base_q + h * HEAD_DIM + _DEINTERLEAVE for h in (2 * jb, 2 * jb + 1)]
            blocks.append((normed, True, Q_SCALE))
            gain_kind.append("q" if normed else None)
        for kvh in range(2):
            cols += [base_k + kvh * HEAD_DIM + _DEINTERLEAVE] * 2
            blocks.append((normed, True, 1.0))
            gain_kind.append("k" if normed else None)
        for kvh in range(2):
            cols += [base_v + kvh * HEAD_DIM + np.arange(HEAD_DIM)] * 2
            blocks.append((False, False, 1.0))
            gain_kind.append(None)
    return np.concatenate(cols), tuple(blocks), gain_kind


def _c_layout():
    cols, blocks = [], []
    for base, rope, scale in ((0, True, Q_SCALE), (1024, True, 1.0)):
        for h in range(8):
            cols += [base + (2 * h + m) * HEAD_DIM + _DEINTERLEAVE for m in range(2)]
            blocks.append((False, rope, scale))
    for h in range(8):
        cols.append(2048 + h * LANES + np.arange(LANES))
        blocks.append((False, False, 1.0))
    return np.concatenate(cols), tuple(blocks)


def kernel(x, c, ctx, c_ctx, ada_w, ada_b, norm_g, ab_w_in, ab_w_out, a_q_norm, a_k_norm, b_sink,
           c_w_in, c_w_out, c_lambda, c_subln_g, router_w, router_bias, exp_w1, exp_w3, exp_w2,
           sh_w1, sh_w3, sh_w2):
    batch, seq, d = x.shape
    ctx_len = ctx.shape[1]
    depth = ada_w.shape[0]
    assert d == D_MODEL and ctx_len == ROW_TILE and depth == 2
    assert seq % (4 * Q_TILE) == 0 and batch <= 4
    n_p = ctx_len + seq
    tiles = n_p // ROW_TILE
    lat_tiles = seq // ROW_TILE
    t_all = batch * n_p

    cond = jnp.zeros((8, d), F32).at[:batch].set(c).at[4].set(c_ctx)
    modr = _modulation(cond, ada_w, ada_b).reshape(depth * 8 * 6, 1, d)
    row_all = lambda i: jnp.where(i % tiles == 0, 4, i // tiles)
    cos_t, sin_t = _rope_tables(seq, ctx_len)
    rperm = np.array([(r % GROUP_SIZE) * GROUP_SIZE + r // GROUP_SIZE for r in range(N_EXPERTS)])
    xt = (ctx.reshape(batch * ctx_len, d), x.reshape(batch * seq, d))

    cols, blocks, gain_kind = _ab_layout()
    w0 = ab_w_in[0][:, cols].astype(BF16)
    gq = jnp.tile(a_q_norm[0][_DEINTERLEAVE], 2)
    gk = jnp.tile(a_k_norm[0][_DEINTERLEAVE], 2)
    ones = jnp.ones((LANES,), F32)
    head_gain = jnp.concatenate([{"q": gq, "k": gk, None: ones}[kind] for kind in gain_kind])[None, :]
    p0 = _project(xt, modr, 0, row_all, norm_g[0, 0][None, :], w0, cos_t, sin_t, head_gain, blocks, tiles)
    common = dict(batch=batch, seq=seq, ctx_len=ctx_len, n_qblocks=4)
    dense = dict(mode="dense", q_col0=0, k_col=lambda j: 4 + j // 2, v_col=lambda j: 6 + j // 2, **common)
    window = dict(mode="window", q_col0=8, k_col=lambda j: 12 + j // 2, v_col=lambda j: 14 + j // 2, **common)
    sink = jnp.broadcast_to(b_sink[0].astype(F32)[:, None], (8, LANES))
    oa = (_attention(p0, [], queries="context", **dense), _attention(p0, [], queries="latent", q_parts=4, **dense))
    ob = (_attention(p0, [sink], queries="context", **window),
          _attention(p0, [sink], queries="latent", **window))
    w_out = ab_w_out[0].astype(BF16)
    x1, hp, lg = _out_project([oa, ob], [w_out[:512], w_out[512:]], xt, lambda i: i, modr, 0, row_all,
                              norm_g[0, 1][None, :], norm_g[0, 2][None, :], router_w[0].T[rperm],
                              t_all // ROW_TILE, tiles=tiles)
    x2 = _moe(x1, hp, lg, router_bias[0], (exp_w1, exp_w3, exp_w2, 0), sh_w1[0], sh_w3[0], sh_w2[0],
              norm_g[0, 3][None, :], modr, 0, lambda i: row_all(i // (ROW_TILE // MOVE_TILE)))

    base1 = 8 * 6
    lambda_init = 0.8 - 0.6 * math.exp(-0.3 * 1)
    cols1, blocks1 = _c_layout()
    w1p = c_w_in[0][:, cols1].astype(BF16)
    p1 = _project(x2, modr, base1, row_all, norm_g[1, 0][None, :], w1p, cos_t, sin_t,
                  jnp.ones((1, w1p.shape[1]), F32), blocks1, tiles)
    lam = jnp.zeros((8, LANES), F32).at[:4, :HEAD_DIM].set(c_lambda[0].astype(F32))
    oc = _attention(p1, [lam, c_subln_g[0][None, :]], mode="diff", queries="latent", batch=batch, seq=seq,
                    ctx_len=ctx_len, n_qblocks=8, q_col0=0, k_col=lambda j: 8 + j, v_col=lambda j: 16 + j,
                    lambda_init=lambda_init, q_parts=4)
    row_lat = lambda i: i // lat_tiles
    x3, hp1, lg1 = _out_project([oc], [c_w_out[0].astype(BF16)], x2,
                                lambda i: (i // lat_tiles) * tiles + 1 + i % lat_tiles, modr, base1, row_lat,
                                norm_g[1, 1][None, :], norm_g[1, 2][None, :], router_w[1].T[rperm],
                                batch * lat_tiles)
    out = _moe(x3, hp1, lg1, router_bias[1], (exp_w1, exp_w3, exp_w2, 1), sh_w1[1], sh_w3[1], sh_w2[1],
               norm_g[1, 3][None, :], modr, base1, lambda i: row_lat(i // (ROW_TILE // MOVE_TILE)))
    return out.reshape(batch, seq, d)
```
